```python
import math
import jax
import jax.numpy as jnp
from jax import lax
import numpy as np


D_MODEL = 1024
BATCH = 8
SEQ = 4096
DEPTH = 4

GRID_W = 64
CTX_LEN = 256

HEAD_DIM = 128
N_Q_HEADS = 8
N_KV_HEADS = 2
GQA_GROUP = N_Q_HEADS // N_KV_HEADS
ATTN_WIDTH = N_Q_HEADS * HEAD_DIM
KV_WIDTH = N_KV_HEADS * HEAD_DIM
Q_BLOCK = 128
ROPE_THETA = 10000.0
ROPE_FREQS = HEAD_DIM // 4
ATTN_SCALE = HEAD_DIM ** -0.5

SSM_WIDTH = D_MODEL // 2
SSM_GROUP = 16
SSM_GROUPS = SSM_WIDTH // SSM_GROUP
SSM_STATE = 64
DT_MIN = 0.001
DT_MAX = 0.1

IN_SPLITS = (ATTN_WIDTH,
             ATTN_WIDTH + KV_WIDTH,
             ATTN_WIDTH + 2 * KV_WIDTH,
             ATTN_WIDTH + 2 * KV_WIDTH + SSM_WIDTH,
             ATTN_WIDTH + 2 * KV_WIDTH + SSM_WIDTH + D_MODEL)
IN_WIDTH = ATTN_WIDTH + 2 * KV_WIDTH + SSM_WIDTH + 2 * D_MODEL

PEER_HEADS = 8
PEER_N_KEYS = 128
PEER_N_EXPERTS = PEER_N_KEYS ** 2
PEER_QUERY_DIM = 256
PEER_HALF = PEER_QUERY_DIM // 2
PEER_TOPK = 16
PEER_CHUNK = 128

N_MOD = 6
EPS = 1e-6

kernel_name = 'hybrid_gqa_s5_peer_diffusion_block'


def rms_norm(x, g):
    xf = x.astype(jnp.float32)
    y = xf * lax.rsqrt(jnp.mean(xf * xf, axis=-1, keepdims=True) + EPS)
    return (y * g.astype(jnp.float32)).astype(x.dtype)


def modulate(x, shift, scale):
    return x * (1.0 + scale) + shift


def rope_1d(x, ang):
    x1, x2 = jnp.split(x, 2, axis=-1)
    cos = jnp.cos(ang)[:, None, :].astype(x.dtype)
    sin = jnp.sin(ang)[:, None, :].astype(x.dtype)
    return jnp.concatenate([x1 * cos - x2 * sin, x1 * sin + x2 * cos], axis=-1)


def axial_rope(x, ang_row, ang_col):
    x_row, x_col = jnp.split(x, 2, axis=-1)
    return jnp.concatenate([rope_1d(x_row, ang_row), rope_1d(x_col, ang_col)], axis=-1)


def split_heads(t, n_heads):
    return t.reshape(t.shape[0], t.shape[1], n_heads, HEAD_DIM)


def gqa_attend(q, k, v):
    s = jnp.einsum('bqkgd,bskd->bkgqs', q, k).astype(jnp.float32) * ATTN_SCALE
    p = jax.nn.softmax(s, axis=-1).astype(v.dtype)
    return jnp.einsum('bkgqs,bskd->bqkgd', p, v)


def latent_attention(q, k_all, v_all):
    bsz, n = q.shape[:2]
    nb = n // Q_BLOCK
    qb = q.reshape(bsz, nb, Q_BLOCK, N_KV_HEADS, GQA_GROUP, HEAD_DIM).transpose(1, 0, 2, 3, 4, 5)
    o = lax.map(lambda blk: gqa_attend(blk, k_all, v_all), qb)
    return o.transpose(1, 0, 2, 3, 4, 5).reshape(bsz, n, ATTN_WIDTH)


def _cmul(ar, ai, br, bi):
    return ar * br - ai * bi, ar * bi + ai * br


def _scan_combine(e_i, e_j):
    ai_r, ai_i, bi_r, bi_i = e_i
    aj_r, aj_i, bj_r, bj_i = e_j
    a_r, a_i = _cmul(aj_r, aj_i, ai_r, ai_i)
    t_r, t_i = _cmul(aj_r, aj_i, bi_r, bi_i)
    return a_r, a_i, t_r + bj_r, t_i + bj_i


def s5_discretise(a_re, a_im, log_dt, b_re, b_im):
    a_re = a_re.astype(jnp.float32)
    a_im = a_im.astype(jnp.float32)
    dt = jnp.exp(log_dt.astype(jnp.float32))[:, None]
    mag = jnp.exp(a_re * dt)
    ab_r = mag * jnp.cos(a_im * dt)
    ab_i = mag * jnp.sin(a_im * dt)
    den = a_re * a_re + a_im * a_im
    nr, ni = ab_r - 1.0, ab_i
    k_r = (nr * a_re + ni * a_im) / den
    k_i = (ni * a_re - nr * a_im) / den
    bb_r, bb_i = _cmul(k_r[..., None], k_i[..., None],
                       b_re.astype(jnp.float32), b_im.astype(jnp.float32))
    return ab_r, ab_i, bb_r, bb_i


def s5_scan(u, ab_r, ab_i, bb_r, bb_i, h0, reverse):
    bu_r = jnp.einsum('blgp,gnp->lbgn', u, bb_r)
    bu_i = jnp.einsum('blgp,gnp->lbgn', u, bb_i)
    if h0 is not None:
        first = -1 if reverse else 0
        s_r, s_i = _cmul(ab_r, ab_i, h0[0], h0[1])
        bu_r = bu_r.at[first].add(s_r)
        bu_i = bu_i.at[first].add(s_i)
    length = u.shape[1]
    a_r = jnp.broadcast_to(ab_r[None, None], (length, 1) + ab_r.shape)
    a_i = jnp.broadcast_to(ab_i[None, None], (length, 1) + ab_i.shape)
    _, _, h_r, h_i = lax.associative_scan(_scan_combine, (a_r, a_i, bu_r, bu_i),
                                          reverse=reverse, axis=0)
    return h_r, h_i


def s5_readout(c_re, c_im, h_r, h_i):
    return (jnp.einsum('gpn,lbgn->blgp', c_re.astype(jnp.float32), h_r)
            - jnp.einsum('gpn,lbgn->blgp', c_im.astype(jnp.float32), h_i))


def s5_output(y, u, d_skip, w_glu, b_glu):
    bsz, n = u.shape[:2]
    y = y.reshape(bsz, n, SSM_WIDTH).astype(u.dtype) + d_skip * u
    y = jax.nn.gelu(y)
    return y * jax.nn.sigmoid(y @ w_glu + b_glu)


def s5_mixer(u_ctx, u_lat, a_re, a_im, log_dt, b_re, b_im, c_re, c_im,
             d_skip, w_glu, b_glu, ctx_out):
    bsz, n_lat = u_lat.shape[:2]
    n_ctx = u_ctx.shape[1]
    uc = u_ctx.astype(jnp.float32).reshape(bsz, n_ctx, SSM_GROUPS, SSM_GROUP)
    ul = u_lat.astype(jnp.float32).reshape(bsz, n_lat, SSM_GROUPS, SSM_GROUP)
    ys_lat, ys_ctx = [], []
    for d, reverse in enumerate((False, True)):
        disc = s5_discretise(a_re[d], a_im[d], log_dt[d], b_re[d], b_im[d])
        hc_r, hc_i = s5_scan(uc, *disc, None, reverse)
        end = 0 if reverse else -1
        hl_r, hl_i = s5_scan(ul, *disc, (hc_r[end], hc_i[end]), reverse)
        ys_lat.append(s5_readout(c_re[d], c_im[d], hl_r, hl_i))
        if ctx_out:
            ys_ctx.append(s5_readout(c_re[d], c_im[d], hc_r, hc_i))
    y_lat = s5_output(ys_lat[0] + ys_lat[1], u_lat, d_skip, w_glu, b_glu)
    y_ctx = s5_output(ys_ctx[0] + ys_ctx[1], u_ctx, d_skip, w_glu, b_glu) if ctx_out else None
    return y_lat, y_ctx


def branch_merge(attn, ssm, gate_a, gate_s, w_ba, w_bs, w_o):
    y = jax.nn.sigmoid(gate_a) * (attn @ w_ba) + jax.nn.sigmoid(gate_s) * (ssm @ w_bs)
    return y @ w_o


def peer_ffn(x, w_q, keys, u_tab, v_tab):
    shape = x.shape
    chunks = x.reshape(-1, PEER_CHUNK, D_MODEL)

    def one_chunk(xc):
        q = (xc @ w_q).reshape(PEER_CHUNK, PEER_HEADS, 2, PEER_HALF)
        s = jnp.einsum('thpd,pkd->thpk', q, keys).astype(jnp.float32)
        s_top, i_top = lax.top_k(s, PEER_TOPK)
        cand = (s_top[:, :, 0, :, None] + s_top[:, :, 1, None, :]).reshape(
            PEER_CHUNK, PEER_HEADS, PEER_TOPK * PEER_TOPK)
        best, pos = lax.top_k(cand, PEER_TOPK)
        i1 = jnp.take_along_axis(i_top[:, :, 0], pos // PEER_TOPK, axis=-1)
        i2 = jnp.take_along_axis(i_top[:, :, 1], pos % PEER_TOPK, axis=-1)
        expert = i1 * PEER_N_KEYS + i2
        g = jax.nn.softmax(best, axis=-1).astype(xc.dtype)
        u_e = u_tab[expert]
        v_e = v_tab[expert]
        act = jax.nn.gelu(jnp.einsum('thkd,td->thk', u_e, xc))
        return jnp.einsum('thk,thkd->td', g * act, v_e)

    return lax.map(one_chunk, chunks).reshape(shape)


def setup_inputs(seed: int = 0) -> dict:
    key = jax.random.key(seed)
    ks = iter(jax.random.split(key, 32))

    def nrm(shape, scale):
        return jax.random.normal(next(ks), shape, jnp.float32) * scale

    L = DEPTH
    x = nrm((BATCH, SEQ, D_MODEL), 1.0)
    c = nrm((BATCH, D_MODEL), 1.0)
    ctx = nrm((BATCH, CTX_LEN, D_MODEL), 1.0)
    c_ctx = nrm((D_MODEL,), 1.0)
    w_mod = nrm((L, D_MODEL, N_MOD * D_MODEL), 0.5 * D_MODEL ** -0.5)
    b_mod = nrm((L, N_MOD * D_MODEL), 0.02)
    norm_mix_g = 1.0 + nrm((L, D_MODEL), 0.02)
    w_in = nrm((L, D_MODEL, IN_WIDTH), D_MODEL ** -0.5)
    q_norm_g = 1.0 + nrm((L, HEAD_DIM), 0.02)
    k_norm_g = 1.0 + nrm((L, HEAD_DIM), 0.02)
    ssm_a_re = -0.5 + nrm((L, 2, SSM_GROUPS, SSM_STATE), 0.01)
    ssm_a_im = (jnp.pi * jnp.arange(SSM_STATE, dtype=jnp.float32)
                + nrm((L, 2, SSM_GROUPS, SSM_STATE), 0.01))
    ssm_log_dt = jax.random.uniform(next(ks), (L, 2, SSM_GROUPS), jnp.float32,
                                    minval=math.log(DT_MIN), maxval=math.log(DT_MAX))
    ssm_b_re = nrm((L, 2, SSM_GROUPS, SSM_STATE, SSM_GROUP), (2 * SSM_GROUP) ** -0.5)
    ssm_b_im = nrm((L, 2, SSM_GROUPS, SSM_STATE, SSM_GROUP), (2 * SSM_GROUP) ** -0.5)
    ssm_c_re = nrm((L, 2, SSM_GROUPS, SSM_GROUP, SSM_STATE), SSM_STATE ** -0.5)
    ssm_c_im = nrm((L, 2, SSM_GROUPS, SSM_GROUP, SSM_STATE), SSM_STATE ** -0.5)
    ssm_d = nrm((L, SSM_WIDTH), 0.5)
    w_glu = nrm((L, SSM_WIDTH, SSM_WIDTH), SSM_WIDTH ** -0.5)
    b_glu = nrm((L, SSM_WIDTH), 0.02)
    w_branch_attn = nrm((L, ATTN_WIDTH, D_MODEL), ATTN_WIDTH ** -0.5)
    w_branch_ssm = nrm((L, SSM_WIDTH, D_MODEL), SSM_WIDTH ** -0.5)
    w_out = nrm((L, D_MODEL, D_MODEL), D_MODEL ** -0.5)
    norm_ffn_g = 1.0 + nrm((L, D_MODEL), 0.02)
    peer_w_q = nrm((L, D_MODEL, PEER_HEADS * PEER_QUERY_DIM), D_MODEL ** -0.5)
    peer_keys = nrm((L, 2, PEER_N_KEYS, PEER_HALF), PEER_HALF ** -0.5)
    peer_u = nrm((L, PEER_N_EXPERTS, D_MODEL), D_MODEL ** -0.5)
    peer_v = nrm((L, PEER_N_EXPERTS, D_MODEL), 0.5)
    return {'x': x, 'c': c, 'ctx': ctx, 'c_ctx': c_ctx,
            'w_mod': w_mod, 'b_mod': b_mod, 'norm_mix_g': norm_mix_g, 'w_in': w_in,
            'q_norm_g': q_norm_g, 'k_norm_g': k_norm_g,
            'ssm_a_re': ssm_a_re, 'ssm_a_im': ssm_a_im, 'ssm_log_dt': ssm_log_dt,
            'ssm_b_re': ssm_b_re, 'ssm_b_im': ssm_b_im, 'ssm_c_re': ssm_c_re, 'ssm_c_im': ssm_c_im,
            'ssm_d': ssm_d, 'w_glu': w_glu, 'b_glu': b_glu,
            'w_branch_attn': w_branch_attn, 'w_branch_ssm': w_branch_ssm, 'w_out': w_out,
            'norm_ffn_g': norm_ffn_g, 'peer_w_q': peer_w_q, 'peer_keys': peer_keys,
            'peer_u': peer_u, 'peer_v': peer_v}


def reference(x, c, ctx, c_ctx, w_mod, b_mod, norm_mix_g, w_in, q_norm_g, k_norm_g,
              ssm_a_re, ssm_a_im, ssm_log_dt, ssm_b_re, ssm_b_im, ssm_c_re, ssm_c_im,
              ssm_d, w_glu, b_glu, w_branch_attn, w_branch_ssm, w_out, norm_ffn_g,
              peer_w_q, peer_keys, peer_u, peer_v):
    bsz, n_lat, _ = x.shape
    n_ctx = ctx.shape[1]
    ROWS = n_lat // GRID_W
    row = jnp.repeat(jnp.arange(ROWS), GRID_W)
    col = jnp.tile(jnp.arange(GRID_W), ROWS)
    inv_freq = ROPE_THETA ** (-jnp.arange(ROPE_FREQS, dtype=jnp.float32) / ROPE_FREQS)
    ang_row = row.astype(jnp.float32)[:, None] * inv_freq
    ang_col = col.astype(jnp.float32)[:, None] * inv_freq

    h_lat, h_ctx = x, ctx
    for i in range(DEPTH):
        last = i == DEPTH - 1
        mod_l = [m[:, None, :] for m in jnp.split(jax.nn.silu(c) @ w_mod[i] + b_mod[i], N_MOD, axis=-1)]
        mod_c = jnp.split(jax.nn.silu(c_ctx) @ w_mod[i] + b_mod[i], N_MOD, axis=-1)
        sh1_l, sc1_l, gt1_l, sh2_l, sc2_l, gt2_l = mod_l
        sh1_c, sc1_c, gt1_c, sh2_c, sc2_c, gt2_c = mod_c

        z_l = modulate(rms_norm(h_lat, norm_mix_g[i]), sh1_l, sc1_l) @ w_in[i]
        z_c = modulate(rms_norm(h_ctx, norm_mix_g[i]), sh1_c, sc1_c) @ w_in[i]
        q_l, k_l, v_l, u_l, ga_l, gb_l = jnp.split(z_l, IN_SPLITS, axis=-1)
        q_c, k_c, v_c, u_c, ga_c, gb_c = jnp.split(z_c, IN_SPLITS, axis=-1)

        q_l = axial_rope(rms_norm(split_heads(q_l, N_Q_HEADS), q_norm_g[i]), ang_row, ang_col)
        k_l = axial_rope(rms_norm(split_heads(k_l, N_KV_HEADS), k_norm_g[i]), ang_row, ang_col)
        k_c = rms_norm(split_heads(k_c, N_KV_HEADS), k_norm_g[i])
        v_c = split_heads(v_c, N_KV_HEADS)
        k_all = jnp.concatenate([k_l, k_c], axis=1)
        v_all = jnp.concatenate([split_heads(v_l, N_KV_HEADS), v_c], axis=1)
        attn_l = latent_attention(q_l, k_all, v_all)

        ssm_l, ssm_c = s5_mixer(u_c, u_l, ssm_a_re[i], ssm_a_im[i], ssm_log_dt[i],
                                ssm_b_re[i], ssm_b_im[i], ssm_c_re[i], ssm_c_im[i],
                                ssm_d[i], w_glu[i], b_glu[i], not last)

        out_l = branch_merge(attn_l, ssm_l, ga_l, gb_l, w_branch_attn[i], w_branch_ssm[i], w_out[i])
        h_lat = h_lat + gt1_l * out_l
        if not last:
            q_c = rms_norm(split_heads(q_c, N_Q_HEADS), q_norm_g[i]).reshape(
                bsz, n_ctx, N_KV_HEADS, GQA_GROUP, HEAD_DIM)
            attn_c = gqa_attend(q_c, k_c, v_c).reshape(bsz, n_ctx, ATTN_WIDTH)
            out_c = branch_merge(attn_c, ssm_c, ga_c, gb_c, w_branch_attn[i], w_branch_ssm[i], w_out[i])
            h_ctx = h_ctx + gt1_c * out_c

        xn_l = modulate(rms_norm(h_lat, norm_ffn_g[i]), sh2_l, sc2_l)
        h_lat = h_lat + gt2_l * peer_ffn(xn_l, peer_w_q[i], peer_keys[i], peer_u[i], peer_v[i])
        if not last:
            xn_c = modulate(rms_norm(h_ctx, norm_ffn_g[i]), sh2_c, sc2_c)
            h_ctx = h_ctx + gt2_c * peer_ffn(xn_c, peer_w_q[i], peer_keys[i], peer_u[i], peer_v[i])
    return h_lat
```

```python
import functools
import math

import jax
import jax.numpy as jnp
from jax import lax
from jax.experimental import pallas as pl
from jax.experimental.pallas import tpu as pltpu

D_MODEL = 1024
GRID_W = 64
HEAD_DIM = 128
N_Q_HEADS = 8
N_KV_HEADS = 2
GQA_GROUP = N_Q_HEADS // N_KV_HEADS
ATTN_WIDTH = N_Q_HEADS * HEAD_DIM
KV_WIDTH = N_KV_HEADS * HEAD_DIM
ROPE_THETA = 10000.0
ROPE_FREQS = HEAD_DIM // 4
ATTN_SCALE = HEAD_DIM ** -0.5
SSM_WIDTH = D_MODEL // 2
SSM_GROUP = 16
SSM_GROUPS = SSM_WIDTH // SSM_GROUP
SSM_STATE = 64
IN_WIDTH = ATTN_WIDTH + 2 * KV_WIDTH + SSM_WIDTH + 2 * D_MODEL
PEER_HEADS = 8
PEER_N_KEYS = 128
PEER_N_EXPERTS = PEER_N_KEYS ** 2
PEER_QUERY_DIM = 256
PEER_HALF = PEER_QUERY_DIM // 2
PEER_TOPK = 16
PEER_PICKS = PEER_HEADS * PEER_TOPK
N_MOD = 6
EPS = 1e-6

ROW_BLOCK = 256
MOD_ROWS = 16
S5_CHUNK = 16
PEER_TOK = 128
EXPERT_ROWS = 4
VMEM_LIMIT = 56 * 1024 * 1024

_NT = (((1,), (1,)), ((), ()))


def _mod_row(b, j, ctx_blocks):
    return jnp.where(j < ctx_blocks, 0, b + 1)


def _mod_kernel(c_ref, w_ref, b_ref, o_ref):
    c = c_ref[...]
    s = c * jax.nn.sigmoid(c)
    o_ref[0] = jnp.dot(s.astype(jnp.bfloat16), w_ref[0].astype(jnp.bfloat16),
                       preferred_element_type=jnp.float32) + b_ref[0]


def modulation(cc, w_mod, b_mod):
    depth = w_mod.shape[0]
    nblk = 1536
    width = N_MOD * D_MODEL
    return pl.pallas_call(
        _mod_kernel,
        grid=(depth, width // nblk),
        in_specs=[pl.BlockSpec((MOD_ROWS, D_MODEL), lambda l, n: (0, 0)),
                  pl.BlockSpec((1, D_MODEL, nblk), lambda l, n: (l, 0, n)),
                  pl.BlockSpec((1, 1, nblk), lambda l, n: (l, 0, n))],
        out_specs=pl.BlockSpec((1, MOD_ROWS, nblk), lambda l, n: (l, 0, n)),
        out_shape=jax.ShapeDtypeStruct((depth, MOD_ROWS, width), jnp.float32),
        name="modulation",
    )(cc, w_mod, b_mod.reshape(depth, 1, width))


def _inproj_kernel(ctx_blocks, h_ref, mod_ref, g_ref, w_ref, cos_ref, sa_ref, sb_ref,
                   qg_ref, kg_ref, q_ref, k_ref, v_ref, u_ref, ga_ref, gb_ref):
    b, j = pl.program_id(0), pl.program_id(1)
    row = _mod_row(b, j, ctx_blocks)
    shift = mod_ref[pl.ds(row, 1), 0:D_MODEL]
    scale = mod_ref[pl.ds(row, 1), D_MODEL:2 * D_MODEL]
    x = h_ref[0]
    xn = x * lax.rsqrt(jnp.mean(x * x, axis=-1, keepdims=True) + EPS) * g_ref[...]
    xm = (xn * (1.0 + scale) + shift).astype(jnp.bfloat16)

    cos, sa, sb = cos_ref[...], sa_ref[...], sb_ref[...]

    def norm_rope(z, g, out_scale):
        zn = z * lax.rsqrt(jnp.mean(z * z, axis=-1, keepdims=True) + EPS) * g
        r = zn * cos + pltpu.roll(zn, 96, 1) * sa + pltpu.roll(zn, 32, 1) * sb
        return r * out_scale

    for hd in range(N_Q_HEADS):
        lo = hd * HEAD_DIM
        z = jnp.dot(xm, w_ref[:, lo:lo + HEAD_DIM], preferred_element_type=jnp.float32)
        q_ref[0, :, lo:lo + HEAD_DIM] = norm_rope(z, qg_ref[...], ATTN_SCALE).astype(q_ref.dtype)
    for hd in range(N_KV_HEADS):
        lo = hd * HEAD_DIM
        z = jnp.dot(xm, w_ref[:, ATTN_WIDTH + lo:ATTN_WIDTH + lo + HEAD_DIM],
                    preferred_element_type=jnp.float32)
        k_ref[0, :, lo:lo + HEAD_DIM] = norm_rope(z, kg_ref[...], 1.0).astype(k_ref.dtype)
    off = ATTN_WIDTH + KV_WIDTH
    v_ref[0] = jnp.dot(xm, w_ref[:, off:off + KV_WIDTH],
                       preferred_element_type=jnp.float32).astype(v_ref.dtype)
    off += KV_WIDTH
    u_ref[0] = jnp.dot(xm, w_ref[:, off:off + SSM_WIDTH], preferred_element_type=jnp.float32)
    off += SSM_WIDTH
    ga_ref[0] = jnp.dot(xm, w_ref[:, off:off + D_MODEL], preferred_element_type=jnp.float32)
    off += D_MODEL
    gb_ref[0] = jnp.dot(xm, w_ref[:, off:off + D_MODEL], preferred_element_type=jnp.float32)


def in_projection(h, mod, g, w_in_bf16, rope, qg, kg, n_ctx):
    bsz, n_tok, _ = h.shape
    nblk = n_tok // ROW_BLOCK
    cos, sa, sb = rope
    row_spec = lambda w: pl.BlockSpec((1, ROW_BLOCK, w), lambda b, j: (b, j, 0))
    full = lambda a: pl.BlockSpec(a.shape, lambda b, j: (0,) * a.ndim)
    rope_spec = pl.BlockSpec((ROW_BLOCK, HEAD_DIM), lambda b, j: (j, 0))
    f32, bf16 = jnp.float32, jnp.bfloat16
    out_shape = [jax.ShapeDtypeStruct((bsz, n_tok, ATTN_WIDTH), bf16),
                 jax.ShapeDtypeStruct((bsz, n_tok, KV_WIDTH), bf16),
                 jax.ShapeDtypeStruct((bsz, n_tok, KV_WIDTH), bf16),
                 jax.ShapeDtypeStruct((bsz, n_tok, SSM_WIDTH), f32),
                 jax.ShapeDtypeStruct((bsz, n_tok, D_MODEL), f32),
                 jax.ShapeDtypeStruct((bsz, n_tok, D_MODEL), f32)]
    return pl.pallas_call(
        functools.partial(_inproj_kernel, n_ctx // ROW_BLOCK),
        grid=(bsz, nblk),
        in_specs=[row_spec(D_MODEL), full(mod), full(g), full(w_in_bf16),
                  rope_spec, rope_spec, rope_spec, full(qg), full(kg)],
        out_specs=[row_spec(ATTN_WIDTH), row_spec(KV_WIDTH), row_spec(KV_WIDTH),
                   row_spec(SSM_WIDTH), row_spec(D_MODEL), row_spec(D_MODEL)],
        out_shape=out_shape,
        compiler_params=pltpu.CompilerParams(vmem_limit_bytes=VMEM_LIMIT),
        name="in_projection",
    )(h, mod, g, w_in_bf16, cos, sa, sb, qg, kg)


def rope_tables(n_ctx, n_lat):
    pos = jnp.arange(n_lat)
    inv_freq = ROPE_THETA ** (-jnp.arange(ROPE_FREQS, dtype=jnp.float32) / ROPE_FREQS)
    ang_row = (pos // GRID_W).astype(jnp.float32)[:, None] * inv_freq
    ang_col = (pos % GRID_W).astype(jnp.float32)[:, None] * inv_freq
    ang = jnp.concatenate([ang_row, ang_row, ang_col, ang_col], axis=-1)
    cos, sin = jnp.cos(ang), jnp.sin(ang)
    even = ((jnp.arange(HEAD_DIM) // ROPE_FREQS) % 2 == 0)[None, :]
    sa = jnp.where(even, -sin, 0.0)
    sb = jnp.where(even, 0.0, sin)
    pad = lambda t, v: jnp.concatenate([jnp.full((n_ctx, HEAD_DIM), v, jnp.float32), t], axis=0)
    return pad(cos, 1.0), pad(sa, 0.0), pad(sb, 0.0)


def _attn_kernel(ctx_blocks, n_blocks, q_ref, k_ref, v_ref, o_ref):
    j = pl.program_id(2)
    n_chunks = jnp.where(j < ctx_blocks, ctx_blocks, n_blocks)
    for hd in range(GQA_GROUP):
        lo = hd * HEAD_DIM
        q = q_ref[0, :, lo:lo + HEAD_DIM]

        def body(c, carry):
            m, l, acc = carry
            r0 = pl.multiple_of(c * ROW_BLOCK, ROW_BLOCK)
            kc = k_ref[0, pl.ds(r0, ROW_BLOCK), :]
            vc = v_ref[0, pl.ds(r0, ROW_BLOCK), :]
            s = lax.dot_general(q, kc, _NT, preferred_element_type=jnp.float32)
            m_new = jnp.maximum(m, jnp.max(s, axis=-1, keepdims=True))
            alpha = jnp.exp(m - m_new)
            p = jnp.exp(s - m_new)
            l_new = alpha * l + jnp.sum(p, axis=-1, keepdims=True)
            acc_new = alpha * acc + jnp.dot(p.astype(vc.dtype), vc,
                                            preferred_element_type=jnp.float32)
            return m_new, l_new, acc_new

        init = (jnp.full((ROW_BLOCK, 1), -jnp.inf, jnp.float32),
                jnp.zeros((ROW_BLOCK, 1), jnp.float32),
                jnp.zeros((ROW_BLOCK, HEAD_DIM), jnp.float32))
        _, l, acc = lax.fori_loop(0, n_chunks, body, init)
        o_ref[0, :, lo:lo + HEAD_DIM] = (acc / l).astype(o_ref.dtype)


def attention(q, k, v, n_ctx):
    bsz, n_tok, _ = q.shape
    nblk = n_tok // ROW_BLOCK
    gw = GQA_GROUP * HEAD_DIM
    return pl.pallas_call(
        functools.partial(_attn_kernel, n_ctx // ROW_BLOCK, nblk),
        grid=(bsz, N_KV_HEADS, nblk),
        in_specs=[pl.BlockSpec((1, ROW_BLOCK, gw), lambda b, g, j: (b, j, g)),
                  pl.BlockSpec((1, n_tok, HEAD_DIM), lambda b, g, j: (b, 0, g)),
                  pl.BlockSpec((1, n_tok, HEAD_DIM), lambda b, g, j: (b, 0, g))],
        out_specs=pl.BlockSpec((1, ROW_BLOCK, gw), lambda b, g, j: (b, j, g)),
        out_shape=jax.ShapeDtypeStruct((bsz, n_tok, ATTN_WIDTH), jnp.bfloat16),
        name="attention",
    )(q, k, v)


def _cpow(e, lam_re_dt, lam_im_dt):
    mag = jnp.exp(e * lam_re_dt)
    ang = e * lam_im_dt
    return mag * jnp.cos(ang), mag * jnp.sin(ang)


def _s5_kernel(n_batch, ctx_chunks, n_chunks,
               u_ref, arow_ref, acol_ref, b_ref, c_ref, ct_ref, y_ref,
               sfr, sfi, sbr, sbi, hfr, hfi, hbr, hbi):
    f32, bf16 = jnp.float32, jnp.bfloat16
    width = S5_CHUNK * SSM_GROUP
    lane = lax.broadcasted_iota(jnp.int32, (1, width), 1)
    lane_blk = lane // SSM_GROUP
    expand = (lax.broadcasted_iota(jnp.int32, (SSM_GROUP, width), 1) % SSM_GROUP
              == lax.broadcasted_iota(jnp.int32, (SSM_GROUP, width), 0)).astype(f32)
    rowi = lax.broadcasted_iota(jnp.int32, (width, 1), 0) // SSM_GROUP
    u = u_ref[0].astype(bf16)

    m_tot = jnp.zeros((width, width), f32)
    proj, read, step = [], [], []
    for d in range(2):
        fwd = d == 0
        are_c, aim_c, dt_c = acol_ref[d, 0, 0], acol_ref[d, 0, 1], jnp.exp(acol_ref[d, 0, 2])
        lr_c, li_c = are_c * dt_c, aim_c * dt_c
        abr, abi = _cpow(1.0, lr_c, li_c)
        den = are_c * are_c + aim_c * aim_c
        nr, ni = abr - 1.0, abi
        k_r = (nr * are_c + ni * aim_c) / den
        k_i = (ni * are_c - nr * aim_c) / den
        b_re, b_im = b_ref[d, 0, 0], b_ref[d, 0, 1]
        bb_r = k_r * b_re - k_i * b_im
        bb_i = k_r * b_im + k_i * b_re
        bt_r = jnp.dot(bb_r, expand, preferred_element_type=f32)
        bt_i = jnp.dot(bb_i, expand, preferred_element_type=f32)
        e_in = (S5_CHUNK - 1 - lane_blk if fwd else lane_blk).astype(f32)
        ap_r, ap_i = _cpow(e_in, lr_c, li_c)
        proj.append(((ap_r * bt_r - ap_i * bt_i).astype(bf16),
                     (ap_r * bt_i + ap_i * bt_r).astype(bf16)))
        ct_r = jnp.dot(ct_ref[d, 0, 0], expand, preferred_element_type=f32)
        ct_i = jnp.dot(ct_ref[d, 0, 1], expand, preferred_element_type=f32)
        e_out = (lane_blk + 1 if fwd else S5_CHUNK - lane_blk).astype(f32)
        aq_r, aq_i = _cpow(e_out, lr_c, li_c)
        read.append(((ct_r * aq_r - ct_i * aq_i).astype(bf16),
                     (-(ct_r * aq_i + ct_i * aq_r)).astype(bf16)))
        are_r, aim_r, dt_r = arow_ref[d, 0, 0], arow_ref[d, 0, 1], jnp.exp(arow_ref[d, 0, 2])
        lr_r, li_r = are_r * dt_r, aim_r * dt_r
        step.append(_cpow(float(S5_CHUNK), lr_r, li_r))
        lag = (rowi if fwd else S5_CHUNK - 1 - rowi).astype(f32)
        al_r, al_i = _cpow(lag, lr_r, li_r)
        c_re = jnp.concatenate([c_ref[d, 0, 0]] * S5_CHUNK, axis=0)
        c_im = jnp.concatenate([c_ref[d, 0, 1]] * S5_CHUNK, axis=0)
        ca_r = c_re * al_r - c_im * al_i
        ca_i = c_re * al_i + c_im * al_r
        kmat = (jnp.dot(ca_r, bb_r, preferred_element_type=f32)
                - jnp.dot(ca_i, bb_i, preferred_element_type=f32))
        kt = jnp.dot(kmat, expand, preferred_element_type=f32)
        for s in range(S5_CHUNK):
            sh = (s if fwd else S5_CHUNK - 1 - s) * SSM_GROUP
            if sh == 0:
                shifted = kt
            elif fwd:
                shifted = jnp.concatenate([jnp.zeros((sh, width), f32), kt[:width - sh]], axis=0)
            else:
                shifted = jnp.concatenate([kt[sh:], jnp.zeros((sh, width), f32)], axis=0)
            m_tot = m_tot + jnp.where(lane_blk == s, shifted, 0.0)

    y_ref[0] = lax.dot_general(u, m_tot.astype(bf16), _NT, preferred_element_type=f32)
    sfr[...] = lax.dot_general(u, proj[0][0], _NT, preferred_element_type=f32)
    sfi[...] = lax.dot_general(u, proj[0][1], _NT, preferred_element_type=f32)
    sbr[...] = lax.dot_general(u, proj[1][0], _NT, preferred_element_type=f32)
    sbi[...] = lax.dot_general(u, proj[1][1], _NT, preferred_element_type=f32)

    (afr, afi), (abr_, abi_) = step

    def scan_body(i, carry):
        fr, fi, br, bi = carry
        rf = pl.multiple_of(i * n_batch, n_batch)
        hfr[pl.ds(rf, n_batch), :] = fr
        hfi[pl.ds(rf, n_batch), :] = fi
        nfr = afr * fr - afi * fi + sfr[pl.ds(rf, n_batch), :]
        nfi = afr * fi + afi * fr + sfi[pl.ds(rf, n_batch), :]
        cb = jnp.where(i < ctx_chunks, ctx_chunks - 1 - i, n_chunks - 1 + ctx_chunks - i)
        rb = pl.multiple_of(cb * n_batch, n_batch)
        hbr[pl.ds(rb, n_batch), :] = br
        hbi[pl.ds(rb, n_batch), :] = bi
        nbr = abr_ * br - abi_ * bi + sbr[pl.ds(rb, n_batch), :]
        nbi = abr_ * bi + abi_ * br + sbi[pl.ds(rb, n_batch), :]
        return nfr, nfi, nbr, nbi

    z = jnp.zeros((n_batch, SSM_STATE), f32)
    lax.fori_loop(0, n_chunks, scan_body, (z, z, z, z))

    y_ref[0] += (jnp.dot(hfr[...].astype(bf16), read[0][0], preferred_element_type=f32)
                 + jnp.dot(hfi[...].astype(bf16), read[0][1], preferred_element_type=f32)
                 + jnp.dot(hbr[...].astype(bf16), read[1][0], preferred_element_type=f32)
                 + jnp.dot(hbi[...].astype(bf16), read[1][1], preferred_element_type=f32))


def s5_scan(u, a_re, a_im, log_dt, b_re, b_im, c_re, c_im, n_ctx):
    bsz, n_tok, _ = u.shape
    n_chunks = n_tok // S5_CHUNK
    rows = n_chunks * bsz
    width = S5_CHUNK * SSM_GROUP
    ug = u.reshape(bsz, n_chunks, S5_CHUNK, SSM_GROUPS, SSM_GROUP)
    ug = ug.transpose(3, 1, 0, 2, 4).reshape(SSM_GROUPS, rows, width)
    ldt = jnp.broadcast_to(log_dt[..., None], a_re.shape)
    arow = jnp.stack([a_re, a_im, ldt], axis=2)[:, :, :, None, :]
    acol = jnp.stack([a_re, a_im, ldt], axis=2)[..., None]
    bmat = jnp.stack([b_re, b_im], axis=2)
    cmat = jnp.stack([c_re, c_im], axis=2)
    ctmat = jnp.swapaxes(cmat, -1, -2)
    grp = lambda a: pl.BlockSpec((2, 1) + a.shape[2:], lambda g: (0, g) + (0,) * (a.ndim - 2))
    state = pltpu.VMEM((rows, SSM_STATE), jnp.float32)
    y = pl.pallas_call(
        functools.partial(_s5_kernel, bsz, n_ctx // S5_CHUNK, n_chunks),
        grid=(SSM_GROUPS,),
        in_specs=[pl.BlockSpec((1, rows, width), lambda g: (g, 0, 0)),
                  grp(arow), grp(acol), grp(bmat), grp(cmat), grp(ctmat)],
        out_specs=pl.BlockSpec((1, rows, width), lambda g: (g, 0, 0)),
        out_shape=jax.ShapeDtypeStruct((SSM_GROUPS, rows, width), jnp.float32),
        scratch_shapes=[state] * 8,
        compiler_params=pltpu.CompilerParams(vmem_limit_bytes=VMEM_LIMIT),
        name="s5_scan",
    )(ug, arow, acol, bmat, cmat, ctmat)
    y = y.reshape(SSM_GROUPS, n_chunks, bsz, S5_CHUNK, SSM_GROUP)
    return y.transpose(2, 1, 3, 0, 4).reshape(bsz, n_tok, SSM_WIDTH)


def _merge_kernel(ctx_blocks, h_ref, mod_ref, attn_ref, y_ref, u_ref, ga_ref, gb_ref,
                  dskip_ref, wglu_ref, bglu_ref, wba_ref, wbs_ref, wo_ref, o_ref):
    f32, bf16 = jnp.float32, jnp.bfloat16
    b, j = pl.program_id(0), pl.program_id(1)
    row = _mod_row(b, j, ctx_blocks)
    gate = mod_ref[pl.ds(row, 1), 2 * D_MODEL:3 * D_MODEL]
    y = jax.nn.gelu(y_ref[0] + dskip_ref[...] * u_ref[0])
    glu = jnp.dot(y.astype(bf16), wglu_ref[...], preferred_element_type=f32) + bglu_ref[...]
    ssm = y * jax.nn.sigmoid(glu)
    ba = jnp.dot(attn_ref[0], wba_ref[...], preferred_element_type=f32)
    bs = jnp.dot(ssm.astype(bf16), wbs_ref[...], preferred_element_type=f32)
    mix = jax.nn.sigmoid(ga_ref[0]) * ba + jax.nn.sigmoid(gb_ref[0]) * bs
    out = jnp.dot(mix.astype(bf16), wo_ref[...], preferred_element_type=f32)
    o_ref[0] = h_ref[0] + gate * out


def branch_merge(h, mod, attn, y, u, ga, gb, d_skip, w_glu, b_glu, w_ba, w_bs, w_o, n_ctx):
    bsz, n_tok, _ = h.shape
    nblk = n_tok // ROW_BLOCK
    row_spec = lambda w: pl.BlockSpec((1, ROW_BLOCK, w), lambda b, j: (b, j, 0))
    full = lambda a: pl.BlockSpec(a.shape, lambda b, j: (0,) * a.ndim)
    return pl.pallas_call(
        functools.partial(_merge_kernel, n_ctx // ROW_BLOCK),
        grid=(bsz, nblk),
        in_specs=[row_spec(D_MODEL), full(mod), row_spec(ATTN_WIDTH), row_spec(SSM_WIDTH),
                  row_spec(SSM_WIDTH), row_spec(D_MODEL), row_spec(D_MODEL),
                  full(d_skip), full(w_glu), full(b_glu), full(w_ba), full(w_bs), full(w_o)],
        out_specs=row_spec(D_MODEL),
        out_shape=jax.ShapeDtypeStruct(h.shape, jnp.float32),
        compiler_params=pltpu.CompilerParams(vmem_limit_bytes=VMEM_LIMIT),
        name="branch_merge",
    )(h, mod, attn, y, u, ga, gb, d_skip, w_glu, b_glu, w_ba, w_bs, w_o)


def _top16(s, payload, n_rows, vals_ref, pay_ref, base):
    rows = lax.broadcasted_iota(jnp.int32, s.shape, 0)

    def body(r, s):
        m = jnp.max(s, axis=0, keepdims=True)
        pos = jnp.min(jnp.where(s == m, rows, n_rows), axis=0, keepdims=True)
        hit = rows == pos
        vals_ref[pl.ds(base + r, 1), :] = m
        if payload is None:
            pay_ref[pl.ds(base + r, 1), :] = pos
        else:
            pay_ref[pl.ds(base + r, 1), :] = jnp.max(jnp.where(hit, payload, -1), axis=0,
                                                     keepdims=True)
        return jnp.where(hit, -jnp.inf, s)

    lax.fori_loop(0, PEER_TOPK, body, s)


def _peer_query_kernel(ctx_blocks, h_ref, mod_ref, g_ref, wq_ref, keys_ref,
                       xn_ref, idx_ref, gate_ref, q_scr, sv, si, bv, be):
    f32, bf16 = jnp.float32, jnp.bfloat16
    b, j = pl.program_id(0), pl.program_id(1)
    row = _mod_row(b, j, ctx_blocks)
    shift = mod_ref[pl.ds(row, 1), 3 * D_MODEL:4 * D_MODEL]
    scale = mod_ref[pl.ds(row, 1), 4 * D_MODEL:5 * D_MODEL]
    x = h_ref[0]
    xn = x * lax.rsqrt(jnp.mean(x * x, axis=-1, keepdims=True) + EPS) * g_ref[...]
    xm = xn * (1.0 + scale) + shift
    xn_ref[0] = xm
    xb = xm.astype(bf16)
    for hp in range(2 * PEER_HEADS):
        lo = hp * PEER_HALF
        q_scr[hp] = jnp.dot(xb, wq_ref[:, lo:lo + PEER_HALF],
                            preferred_element_type=f32).astype(bf16)

    def head_body(hd, carry):
        for p in range(2):
            s = lax.dot_general(keys_ref[p], q_scr[2 * hd + p], _NT,
                                preferred_element_type=f32)
            _top16(s, None, PEER_N_KEYS, sv, si, p * PEER_TOPK)
        s1, s2 = sv[0:PEER_TOPK, :], sv[PEER_TOPK:2 * PEER_TOPK, :]
        i1, i2 = si[0:PEER_TOPK, :], si[PEER_TOPK:2 * PEER_TOPK, :]
        cand = jnp.concatenate([s1[i:i + 1, :] + s2 for i in range(PEER_TOPK)], axis=0)
        cexp = jnp.concatenate([i1[i:i + 1, :] * PEER_N_KEYS + i2 for i in range(PEER_TOPK)],
                               axis=0)
        base = pl.multiple_of(hd * PEER_TOPK, PEER_TOPK)
        _top16(cand, cexp, PEER_TOPK * PEER_TOPK, bv, be, base)
        return carry

    lax.fori_loop(0, PEER_HEADS, head_body, 0)

    for hd in range(PEER_HEADS):
        best = bv[hd * PEER_TOPK:(hd + 1) * PEER_TOPK, :]
        e = jnp.exp(best - jnp.max(best, axis=0, keepdims=True))
        bv[hd * PEER_TOPK:(hd + 1) * PEER_TOPK, :] = e / jnp.sum(e, axis=0, keepdims=True)
    for t in range(ROW_BLOCK // PEER_TOK):
        idx_ref[t] = be[:, t * PEER_TOK:(t + 1) * PEER_TOK]
        gate_ref[t] = bv[:, t * PEER_TOK:(t + 1) * PEER_TOK]


def peer_query(h, mod, g, wq_bf16, keys_bf16, n_ctx):
    bsz, n_tok, _ = h.shape
    nblk = n_tok // ROW_BLOCK
    sub = ROW_BLOCK // PEER_TOK
    n_pblk = bsz * n_tok // PEER_TOK
    row_spec = pl.BlockSpec((1, ROW_BLOCK, D_MODEL), lambda b, j: (b, j, 0))
    full = lambda a: pl.BlockSpec(a.shape, lambda b, j: (0,) * a.ndim)
    pick_spec = pl.BlockSpec((sub, PEER_PICKS, PEER_TOK), lambda b, j: (b * nblk + j, 0, 0))
    f32, i32 = jnp.float32, jnp.int32
    return pl.pallas_call(
        functools.partial(_peer_query_kernel, n_ctx // ROW_BLOCK),
        grid=(bsz, nblk),
        in_specs=[row_spec, full(mod), full(g), full(wq_bf16), full(keys_bf16)],
        out_specs=[row_spec, pick_spec, pick_spec],
        out_shape=[jax.ShapeDtypeStruct(h.shape, f32),
                   jax.ShapeDtypeStruct((n_pblk, PEER_PICKS, PEER_TOK), i32),
                   jax.ShapeDtypeStruct((n_pblk, PEER_PICKS, PEER_TOK), f32)],
        scratch_shapes=[pltpu.VMEM((2 * PEER_HEADS, ROW_BLOCK, PEER_HALF), jnp.bfloat16),
                        pltpu.VMEM((2 * PEER_TOPK, ROW_BLOCK), f32),
                        pltpu.VMEM((2 * PEER_TOPK, ROW_BLOCK), i32),
                        pltpu.VMEM((PEER_PICKS, ROW_BLOCK), f32),
                        pltpu.VMEM((PEER_PICKS, ROW_BLOCK), i32)],
        compiler_params=pltpu.CompilerParams(vmem_limit_bytes=VMEM_LIMIT),
        name="peer_query",
    )(h, mod, g, wq_bf16, keys_bf16)


def pack_table(tab):
    bits = lax.bitcast_convert_type(tab.astype(jnp.bfloat16), jnp.uint16).astype(jnp.uint32)
    half = D_MODEL // 2
    words = bits[:, :half] | (bits[:, half:] << 16)
    return lax.bitcast_convert_type(words, jnp.int32).reshape(-1, EXPERT_ROWS, 128)


def _unpack(words):
    lo = lax.bitcast_convert_type(words << 16, jnp.float32)
    hi = lax.bitcast_convert_type(words & jnp.int32(-65536), jnp.float32)
    return lo, hi


def _peer_score_kernel(idx_ref, x_ref, gate_ref, tab_ref, w_ref, pbuf):
    f32, bf16 = jnp.float32, jnp.bfloat16
    rows = PEER_PICKS * EXPERT_ROWS
    sel = (lax.broadcasted_iota(jnp.int32, (PEER_PICKS, rows), 1) // EXPERT_ROWS
           == lax.broadcasted_iota(jnp.int32, (PEER_PICKS, rows), 0)).astype(bf16)
    lane = lax.broadcasted_iota(jnp.int32, (PEER_PICKS, PEER_TOK), 1)

    def tok_body(t, acc):
        xt = x_ref[t]
        x_lo, x_hi = xt[0:EXPERT_ROWS], xt[EXPERT_ROWS:2 * EXPERT_ROWS]
        for k in range(PEER_PICKS):
            e = idx_ref[0, 0, k * PEER_TOK + t]
            u_lo, u_hi = _unpack(tab_ref[e])
            pbuf[k * EXPERT_ROWS:(k + 1) * EXPERT_ROWS, :] = u_lo * x_lo + u_hi * x_hi
        part = jnp.dot(sel, pbuf[...].astype(bf16), preferred_element_type=f32)
        score = jnp.sum(part, axis=1, keepdims=True)
        return jnp.where(lane == t, score, acc)

    scores = lax.fori_loop(0, PEER_TOK, tok_body, jnp.zeros((PEER_PICKS, PEER_TOK), f32))
    w_ref[0] = gate_ref[0] * jax.nn.gelu(scores)


def _peer_mix_kernel(idx_ref, wgt_ref, h_ref, gate_ref, tab_ref, o_ref):
    f32 = jnp.float32

    def tok_body(t, carry):
        lo = jnp.zeros((EXPERT_ROWS, 128), f32)
        hi = jnp.zeros((EXPERT_ROWS, 128), f32)
        for k in range(PEER_PICKS):
            e = idx_ref[0, 0, k * PEER_TOK + t]
            w = wgt_ref[0, 0, k * PEER_TOK + t]
            v_lo, v_hi = _unpack(tab_ref[e])
            lo = lo + w * v_lo
            hi = hi + w * v_hi
        out = jnp.concatenate([lo, hi], axis=0)
        o_ref[t] = h_ref[t] + gate_ref[0, 0] * out
        return carry

    lax.fori_loop(0, PEER_TOK, tok_body, 0)


def _table_spec(tab):
    return pl.BlockSpec(tab.shape, lambda i: (0, 0, 0), pipeline_mode=pl.Buffered(1))


def peer_experts(h, xn, idx, gates, mod, u_packed, v_packed, n_ctx):
    bsz, n_tok, _ = h.shape
    n_pblk = idx.shape[0]
    blk_per_batch = n_tok // PEER_TOK
    ctx_pblk = n_ctx // PEER_TOK
    flat = PEER_PICKS * PEER_TOK
    tiles = D_MODEL // 128
    smem_spec = pl.BlockSpec((1, 1, flat), lambda i: (i, 0, 0), memory_space=pltpu.SMEM)
    pick_spec = pl.BlockSpec((1, PEER_PICKS, PEER_TOK), lambda i: (i, 0, 0))
    tok_spec = pl.BlockSpec((PEER_TOK, tiles, 128), lambda i: (i, 0, 0))
    idx_flat = idx.reshape(n_pblk, 1, flat)
    w = pl.pallas_call(
        _peer_score_kernel,
        grid=(n_pblk,),
        in_specs=[smem_spec, tok_spec, pick_spec, _table_spec(u_packed)],
        out_specs=pick_spec,
        out_shape=jax.ShapeDtypeStruct(gates.shape, jnp.float32),
        scratch_shapes=[pltpu.VMEM((PEER_PICKS * EXPERT_ROWS, 128), jnp.float32)],
        compiler_params=pltpu.CompilerParams(vmem_limit_bytes=VMEM_LIMIT),
        name="peer_scores",
    )(idx_flat, xn.reshape(-1, tiles, 128), gates, u_packed)

    def gate_map(i):
        bi, ji = i // blk_per_batch, i % blk_per_batch
        return (jnp.where(ji < ctx_pblk, 0, bi + 1), N_MOD - 1, 0, 0)

    mod_tiles = mod.reshape(MOD_ROWS, N_MOD, tiles, 128)
    out = pl.pallas_call(
        _peer_mix_kernel,
        grid=(n_pblk,),
        in_specs=[smem_spec, smem_spec, tok_spec,
                  pl.BlockSpec((1, 1, tiles, 128), gate_map), _table_spec(v_packed)],
        out_specs=tok_spec,
        out_shape=jax.ShapeDtypeStruct((bsz * n_tok, tiles, 128), jnp.float32),
        compiler_params=pltpu.CompilerParams(vmem_limit_bytes=VMEM_LIMIT),
        name="peer_mix",
    )(idx_flat, w.reshape(n_pblk, 1, flat), h.reshape(-1, tiles, 128), mod_tiles, v_packed)
    return out.reshape(h.shape)


def kernel(x, c, ctx, c_ctx, w_mod, b_mod, norm_mix_g, w_in, q_norm_g, k_norm_g, ssm_a_re, ssm_a_im, ssm_log_dt, ssm_b_re, ssm_b_im, ssm_c_re, ssm_c_im, ssm_d, w_glu, b_glu, w_branch_attn, w_branch_ssm, w_out, norm_ffn_g, peer_w_q, peer_keys, peer_u, peer_v):
    bf16 = jnp.bfloat16
    bsz, n_lat, _ = x.shape
    n_ctx = ctx.shape[1]
    depth = w_mod.shape[0]
    assert bsz + 1 <= MOD_ROWS and n_ctx % ROW_BLOCK == 0 and n_lat % ROW_BLOCK == 0

    cc = jnp.zeros((MOD_ROWS, D_MODEL), jnp.float32).at[0].set(c_ctx).at[1:bsz + 1].set(c)
    mods = modulation(cc, w_mod, b_mod)
    rope = rope_tables(n_ctx, n_lat)
    h = jnp.concatenate([ctx, x], axis=1)
    row = lambda a: a.reshape(1, -1)

    for i in range(depth):
        mod = mods[i]
        q, k, v, u, ga, gb = in_projection(h, mod, row(norm_mix_g[i]), w_in[i].astype(bf16), rope,
                                           row(q_norm_g[i]), row(k_norm_g[i]), n_ctx)
        attn = attention(q, k, v, n_ctx)
        y = s5_scan(u, ssm_a_re[i], ssm_a_im[i], ssm_log_dt[i], ssm_b_re[i], ssm_b_im[i],
                    ssm_c_re[i], ssm_c_im[i], n_ctx)
        h = branch_merge(h, mod, attn, y, u, ga, gb, row(ssm_d[i]), w_glu[i].astype(bf16),
                         row(b_glu[i]), w_branch_attn[i].astype(bf16),
                         w_branch_ssm[i].astype(bf16), w_out[i].astype(bf16), n_ctx)
        xn, idx, gates = peer_query(h, mod, row(norm_ffn_g[i]), peer_w_q[i].astype(bf16),
                                    peer_keys[i].astype(bf16), n_ctx)
        h = peer_experts(h, xn, idx, gates, mod, pack_table(peer_u[i]), pack_table(peer_v[i]),
                         n_ctx)
    return h[:, n_ctx:, :]
```

```python
import functools
import math

import jax
import jax.numpy as jnp
from jax import lax
from jax.experimental import pallas as pl
from jax.experimental.pallas import tpu as pltpu

D_MODEL = 1024
GRID_W = 64
HEAD_DIM = 128
N_Q_HEADS = 8
N_KV_HEADS = 2
GQA_GROUP = N_Q_HEADS // N_KV_HEADS
ATTN_WIDTH = N_Q_HEADS * HEAD_DIM
KV_WIDTH = N_KV_HEADS * HEAD_DIM
ROPE_THETA = 10000.0
ROPE_FREQS = HEAD_DIM // 4
ATTN_SCALE = HEAD_DIM ** -0.5
SSM_WIDTH = D_MODEL // 2
SSM_GROUP = 16
SSM_GROUPS = SSM_WIDTH // SSM_GROUP
SSM_STATE = 64
IN_WIDTH = ATTN_WIDTH + 2 * KV_WIDTH + SSM_WIDTH + 2 * D_MODEL
PEER_HEADS = 8
PEER_N_KEYS = 128
PEER_N_EXPERTS = PEER_N_KEYS ** 2
PEER_QUERY_DIM = 256
PEER_HALF = PEER_QUERY_DIM // 2
PEER_TOPK = 16
PEER_PICKS = PEER_HEADS * PEER_TOPK
N_MOD = 6
EPS = 1e-6

ROW_BLOCK = 256
ATTN_KEYS = 1024
MOD_ROWS = 16
S5_CHUNK = 16
PEER_TOK = 128
EXPERT_ROWS = 4
PEER_STAGE = 4
VMEM_LIMIT = 56 * 1024 * 1024

_NT = (((1,), (1,)), ((), ()))


def _mod_row(b, j, ctx_blocks):
    return jnp.where(j < ctx_blocks, 0, b + 1)


def _mod_kernel(c_ref, w_ref, b_ref, o_ref):
    c = c_ref[...]
    s = c * jax.nn.sigmoid(c)
    o_ref[0] = jnp.dot(s.astype(jnp.bfloat16), w_ref[0].astype(jnp.bfloat16),
                       preferred_element_type=jnp.float32) + b_ref[0]


def modulation(cc, w_mod, b_mod):
    depth = w_mod.shape[0]
    nblk = 1536
    width = N_MOD * D_MODEL
    return pl.pallas_call(
        _mod_kernel,
        grid=(depth, width // nblk),
        in_specs=[pl.BlockSpec((MOD_ROWS, D_MODEL), lambda l, n: (0, 0)),
                  pl.BlockSpec((1, D_MODEL, nblk), lambda l, n: (l, 0, n)),
                  pl.BlockSpec((1, 1, nblk), lambda l, n: (l, 0, n))],
        out_specs=pl.BlockSpec((1, MOD_ROWS, nblk), lambda l, n: (l, 0, n)),
        out_shape=jax.ShapeDtypeStruct((depth, MOD_ROWS, width), jnp.float32),
        name="modulation",
    )(cc, w_mod, b_mod.reshape(depth, 1, width))


def _inproj_kernel(ctx_blocks, h_ref, mod_ref, g_ref, w_ref, cos_ref, sa_ref, sb_ref,
                   qg_ref, kg_ref, q_ref, k_ref, v_ref, u_ref, ga_ref, gb_ref):
    b, j = pl.program_id(0), pl.program_id(1)
    row = _mod_row(b, j, ctx_blocks)
    shift = mod_ref[pl.ds(row, 1), 0:D_MODEL]
    scale = mod_ref[pl.ds(row, 1), D_MODEL:2 * D_MODEL]
    x = h_ref[0]
    xn = x * lax.rsqrt(jnp.mean(x * x, axis=-1, keepdims=True) + EPS) * g_ref[...]
    xm = (xn * (1.0 + scale) + shift).astype(jnp.bfloat16)

    cos, sa, sb = cos_ref[...], sa_ref[...], sb_ref[...]

    def norm_rope(z, g, out_scale):
        zn = z * lax.rsqrt(jnp.mean(z * z, axis=-1, keepdims=True) + EPS) * g
        r = zn * cos + pltpu.roll(zn, 96, 1) * sa + pltpu.roll(zn, 32, 1) * sb
        return r * out_scale

    for hd in range(N_Q_HEADS):
        lo = hd * HEAD_DIM
        z = jnp.dot(xm, w_ref[:, lo:lo + HEAD_DIM], preferred_element_type=jnp.float32)
        q_ref[0, :, lo:lo + HEAD_DIM] = norm_rope(z, qg_ref[...], ATTN_SCALE * math.log2(math.e)).astype(q_ref.dtype)
    for hd in range(N_KV_HEADS):
        lo = hd * HEAD_DIM
        z = jnp.dot(xm, w_ref[:, ATTN_WIDTH + lo:ATTN_WIDTH + lo + HEAD_DIM],
                    preferred_element_type=jnp.float32)
        k_ref[0, :, lo:lo + HEAD_DIM] = norm_rope(z, kg_ref[...], 1.0).astype(k_ref.dtype)
    off = ATTN_WIDTH + KV_WIDTH
    v_ref[0] = jnp.dot(xm, w_ref[:, off:off + KV_WIDTH],
                       preferred_element_type=jnp.float32).astype(v_ref.dtype)
    off += KV_WIDTH
    u_ref[0] = jnp.dot(xm, w_ref[:, off:off + SSM_WIDTH], preferred_element_type=jnp.float32)
    off += SSM_WIDTH
    ga_ref[0] = jnp.dot(xm, w_ref[:, off:off + D_MODEL], preferred_element_type=jnp.float32)
    off += D_MODEL
    gb_ref[0] = jnp.dot(xm, w_ref[:, off:off + D_MODEL], preferred_element_type=jnp.float32)


def in_projection(h, mod, g, w_in_bf16, rope, qg, kg, n_ctx):
    bsz, n_tok, _ = h.shape
    nblk = n_tok // ROW_BLOCK
    cos, sa, sb = rope
    row_spec = lambda w: pl.BlockSpec((1, ROW_BLOCK, w), lambda b, j: (b, j, 0))
    full = lambda a: pl.BlockSpec(a.shape, lambda b, j: (0,) * a.ndim)
    rope_spec = pl.BlockSpec((ROW_BLOCK, HEAD_DIM), lambda b, j: (j, 0))
    f32, bf16 = jnp.float32, jnp.bfloat16
    out_shape = [jax.ShapeDtypeStruct((bsz, n_tok, ATTN_WIDTH), bf16),
                 jax.ShapeDtypeStruct((bsz, n_tok, KV_WIDTH), bf16),
                 jax.ShapeDtypeStruct((bsz, n_tok, KV_WIDTH), bf16),
                 jax.ShapeDtypeStruct((bsz, n_tok, SSM_WIDTH), f32),
                 jax.ShapeDtypeStruct((bsz, n_tok, D_MODEL), f32),
                 jax.ShapeDtypeStruct((bsz, n_tok, D_MODEL), f32)]
    return pl.pallas_call(
        functools.partial(_inproj_kernel, n_ctx // ROW_BLOCK),
        grid=(bsz, nblk),
        in_specs=[row_spec(D_MODEL), full(mod), full(g), full(w_in_bf16),
                  rope_spec, rope_spec, rope_spec, full(qg), full(kg)],
        out_specs=[row_spec(ATTN_WIDTH), row_spec(KV_WIDTH), row_spec(KV_WIDTH),
                   row_spec(SSM_WIDTH), row_spec(D_MODEL), row_spec(D_MODEL)],
        out_shape=out_shape,
        compiler_params=pltpu.CompilerParams(vmem_limit_bytes=VMEM_LIMIT),
        name="in_projection",
    )(h, mod, g, w_in_bf16, cos, sa, sb, qg, kg)


def rope_tables(n_ctx, n_lat):
    pos = jnp.arange(n_lat)
    inv_freq = ROPE_THETA ** (-jnp.arange(ROPE_FREQS, dtype=jnp.float32) / ROPE_FREQS)
    ang_row = (pos // GRID_W).astype(jnp.float32)[:, None] * inv_freq
    ang_col = (pos % GRID_W).astype(jnp.float32)[:, None] * inv_freq
    ang = jnp.concatenate([ang_row, ang_row, ang_col, ang_col], axis=-1)
    cos, sin = jnp.cos(ang), jnp.sin(ang)
    even = ((jnp.arange(HEAD_DIM) // ROPE_FREQS) % 2 == 0)[None, :]
    sa = jnp.where(even, -sin, 0.0)
    sb = jnp.where(even, 0.0, sin)
    pad = lambda t, v: jnp.concatenate([jnp.full((n_ctx, HEAD_DIM), v, jnp.float32), t], axis=0)
    return pad(cos, 1.0), pad(sa, 0.0), pad(sb, 0.0)


def _attn_kernel(n_ctx, n_lat, key_chunk, ctx_blocks, q_ref, k_ref, v_ref, o_ref, m_scr, l_scr, acc_scr):
    f32 = jnp.float32
    j = pl.program_id(2)
    q = jnp.concatenate([q_ref[0, :, hd * HEAD_DIM:(hd + 1) * HEAD_DIM]
                         for hd in range(GQA_GROUP)], axis=0)

    kc, vc = k_ref[0, 0:n_ctx, :], v_ref[0, 0:n_ctx, :]
    s = lax.dot_general(q, kc, _NT, preferred_element_type=f32)
    m = jnp.max(s, axis=-1, keepdims=True)
    p = jnp.exp2(s - m)
    m_scr[...] = m
    l_scr[...] = jnp.sum(p, axis=-1, keepdims=True)
    acc_scr[...] = jnp.dot(p.astype(vc.dtype), vc, preferred_element_type=f32)

    @pl.when(j >= ctx_blocks)
    def _():
        def body(c, carry):
            r0 = pl.multiple_of(n_ctx + c * key_chunk, ROW_BLOCK)
            kc = k_ref[0, pl.ds(r0, key_chunk), :]
            vc = v_ref[0, pl.ds(r0, key_chunk), :]
            s = lax.dot_general(q, kc, _NT, preferred_element_type=f32)
            m_old = m_scr[...]
            m_new = jnp.maximum(m_old, jnp.max(s, axis=-1, keepdims=True))
            alpha = jnp.exp2(m_old - m_new)
            p = jnp.exp2(s - m_new)
            m_scr[...] = m_new
            l_scr[...] = alpha * l_scr[...] + jnp.sum(p, axis=-1, keepdims=True)
            acc_scr[...] = alpha * acc_scr[...] + jnp.dot(p.astype(vc.dtype), vc,
                                                          preferred_element_type=f32)
            return carry

        lax.fori_loop(0, n_lat // key_chunk, body, 0)

    out = acc_scr[...] / l_scr[...]
    for hd in range(GQA_GROUP):
        o_ref[0, :, hd * HEAD_DIM:(hd + 1) * HEAD_DIM] = (
            out[hd * ROW_BLOCK:(hd + 1) * ROW_BLOCK]).astype(o_ref.dtype)


def attention(q, k, v, n_ctx):
    bsz, n_tok, _ = q.shape
    nblk = n_tok // ROW_BLOCK
    gw = GQA_GROUP * HEAD_DIM
    rows = GQA_GROUP * ROW_BLOCK
    key_chunk = math.gcd(n_tok - n_ctx, ATTN_KEYS)
    return pl.pallas_call(
        functools.partial(_attn_kernel, n_ctx, n_tok - n_ctx, key_chunk, n_ctx // ROW_BLOCK),
        grid=(bsz, N_KV_HEADS, nblk),
        in_specs=[pl.BlockSpec((1, ROW_BLOCK, gw), lambda b, g, j: (b, j, g)),
                  pl.BlockSpec((1, n_tok, HEAD_DIM), lambda b, g, j: (b, 0, g)),
                  pl.BlockSpec((1, n_tok, HEAD_DIM), lambda b, g, j: (b, 0, g))],
        out_specs=pl.BlockSpec((1, ROW_BLOCK, gw), lambda b, g, j: (b, j, g)),
        out_shape=jax.ShapeDtypeStruct((bsz, n_tok, ATTN_WIDTH), jnp.bfloat16),
        scratch_shapes=[pltpu.VMEM((rows, 1), jnp.float32), pltpu.VMEM((rows, 1), jnp.float32),
                        pltpu.VMEM((rows, HEAD_DIM), jnp.float32)],
        compiler_params=pltpu.CompilerParams(vmem_limit_bytes=VMEM_LIMIT),
        name="attention",
    )(q, k, v)


def _cpow(e, lam_re_dt, lam_im_dt):
    mag = jnp.exp(e * lam_re_dt)
    ang = e * lam_im_dt
    return mag * jnp.cos(ang), mag * jnp.sin(ang)


def _s5_kernel(n_batch, ctx_chunks, n_chunks,
               u_ref, arow_ref, acol_ref, b_ref, c_ref, ct_ref, y_ref,
               sfr, sfi, sbr, sbi, hfr, hfi, hbr, hbi):
    f32, bf16 = jnp.float32, jnp.bfloat16
    width = S5_CHUNK * SSM_GROUP
    lane = lax.broadcasted_iota(jnp.int32, (1, width), 1)
    lane_blk = lane // SSM_GROUP
    expand = (lax.broadcasted_iota(jnp.int32, (SSM_GROUP, width), 1) % SSM_GROUP
              == lax.broadcasted_iota(jnp.int32, (SSM_GROUP, width), 0)).astype(f32)
    rowi = lax.broadcasted_iota(jnp.int32, (width, 1), 0) // SSM_GROUP
    u = u_ref[0].astype(bf16)

    m_tot = jnp.zeros((width, width), f32)
    proj, read, step = [], [], []
    for d in range(2):
        fwd = d == 0
        are_c, aim_c, dt_c = acol_ref[d, 0, 0], acol_ref[d, 0, 1], jnp.exp(acol_ref[d, 0, 2])
        lr_c, li_c = are_c * dt_c, aim_c * dt_c
        abr, abi = _cpow(1.0, lr_c, li_c)
        den = are_c * are_c + aim_c * aim_c
        nr, ni = abr - 1.0, abi
        k_r = (nr * are_c + ni * aim_c) / den
        k_i = (ni * are_c - nr * aim_c) / den
        b_re, b_im = b_ref[d, 0, 0], b_ref[d, 0, 1]
        bb_r = k_r * b_re - k_i * b_im
        bb_i = k_r * b_im + k_i * b_re
        bt_r = jnp.dot(bb_r, expand, preferred_element_type=f32)
        bt_i = jnp.dot(bb_i, expand, preferred_element_type=f32)
        e_in = (S5_CHUNK - 1 - lane_blk if fwd else lane_blk).astype(f32)
        ap_r, ap_i = _cpow(e_in, lr_c, li_c)
        proj.append(((ap_r * bt_r - ap_i * bt_i).astype(bf16),
                     (ap_r * bt_i + ap_i * bt_r).astype(bf16)))
        ct_r = jnp.dot(ct_ref[d, 0, 0], expand, preferred_element_type=f32)
        ct_i = jnp.dot(ct_ref[d, 0, 1], expand, preferred_element_type=f32)
        e_out = (lane_blk + 1 if fwd else S5_CHUNK - lane_blk).astype(f32)
        aq_r, aq_i = _cpow(e_out, lr_c, li_c)
        read.append(((ct_r * aq_r - ct_i * aq_i).astype(bf16),
                     (-(ct_r * aq_i + ct_i * aq_r)).astype(bf16)))
        are_r, aim_r, dt_r = arow_ref[d, 0, 0], arow_ref[d, 0, 1], jnp.exp(arow_ref[d, 0, 2])
        lr_r, li_r = are_r * dt_r, aim_r * dt_r
        step.append(_cpow(float(S5_CHUNK), lr_r, li_r))
        lag = (rowi if fwd else S5_CHUNK - 1 - rowi).astype(f32)
        al_r, al_i = _cpow(lag, lr_r, li_r)
        c_re = jnp.concatenate([c_ref[d, 0, 0]] * S5_CHUNK, axis=0)
        c_im = jnp.concatenate([c_ref[d, 0, 1]] * S5_CHUNK, axis=0)
        ca_r = c_re * al_r - c_im * al_i
        ca_i = c_re * al_i + c_im * al_r
        kmat = (jnp.dot(ca_r, bb_r, preferred_element_type=f32)
                - jnp.dot(ca_i, bb_i, preferred_element_type=f32))
        kt = jnp.dot(kmat, expand, preferred_element_type=f32)
        for s in range(S5_CHUNK):
            sh = (s if fwd else S5_CHUNK - 1 - s) * SSM_GROUP
            if sh == 0:
                shifted = kt
            elif fwd:
                shifted = jnp.concatenate([jnp.zeros((sh, width), f32), kt[:width - sh]], axis=0)
            else:
                shifted = jnp.concatenate([kt[sh:], jnp.zeros((sh, width), f32)], axis=0)
            m_tot = m_tot + jnp.where(lane_blk == s, shifted, 0.0)

    y_ref[0] = lax.dot_general(u, m_tot.astype(bf16), _NT, preferred_element_type=f32)
    sfr[...] = lax.dot_general(u, proj[0][0], _NT, preferred_element_type=f32)
    sfi[...] = lax.dot_general(u, proj[0][1], _NT, preferred_element_type=f32)
    sbr[...] = lax.dot_general(u, proj[1][0], _NT, preferred_element_type=f32)
    sbi[...] = lax.dot_general(u, proj[1][1], _NT, preferred_element_type=f32)

    (afr, afi), (abr_, abi_) = step

    def scan_body(i, carry):
        fr, fi, br, bi = carry
        rf = pl.multiple_of(i * n_batch, n_batch)
        hfr[pl.ds(rf, n_batch), :] = fr
        hfi[pl.ds(rf, n_batch), :] = fi
        nfr = afr * fr - afi * fi + sfr[pl.ds(rf, n_batch), :]
        nfi = afr * fi + afi * fr + sfi[pl.ds(rf, n_batch), :]
        cb = jnp.where(i < ctx_chunks, ctx_chunks - 1 - i, n_chunks - 1 + ctx_chunks - i)
        rb = pl.multiple_of(cb * n_batch, n_batch)
        hbr[pl.ds(rb, n_batch), :] = br
        hbi[pl.ds(rb, n_batch), :] = bi
        nbr = abr_ * br - abi_ * bi + sbr[pl.ds(rb, n_batch), :]
        nbi = abr_ * bi + abi_ * br + sbi[pl.ds(rb, n_batch), :]
        return nfr, nfi, nbr, nbi

    z = jnp.zeros((n_batch, SSM_STATE), f32)
    lax.fori_loop(0, n_chunks, scan_body, (z, z, z, z))

    y_ref[0] += (jnp.dot(hfr[...].astype(bf16), read[0][0], preferred_element_type=f32)
                 + jnp.dot(hfi[...].astype(bf16), read[0][1], preferred_element_type=f32)
                 + jnp.dot(hbr[...].astype(bf16), read[1][0], preferred_element_type=f32)
                 + jnp.dot(hbi[...].astype(bf16), read[1][1], preferred_element_type=f32))


def s5_scan(u, a_re, a_im, log_dt, b_re, b_im, c_re, c_im, n_ctx):
    bsz, n_tok, _ = u.shape
    n_chunks = n_tok // S5_CHUNK
    rows = n_chunks * bsz
    width = S5_CHUNK * SSM_GROUP
    ug = u.reshape(bsz, n_chunks, S5_CHUNK, SSM_GROUPS, SSM_GROUP)
    ug = ug.transpose(3, 1, 0, 2, 4).reshape(SSM_GROUPS, rows, width)
    ldt = jnp.broadcast_to(log_dt[..., None], a_re.shape)
    arow = jnp.stack([a_re, a_im, ldt], axis=2)[:, :, :, None, :]
    acol = jnp.stack([a_re, a_im, ldt], axis=2)[..., None]
    bmat = jnp.stack([b_re, b_im], axis=2)
    cmat = jnp.stack([c_re, c_im], axis=2)
    ctmat = jnp.swapaxes(cmat, -1, -2)
    grp = lambda a: pl.BlockSpec((2, 1) + a.shape[2:], lambda g: (0, g) + (0,) * (a.ndim - 2))
    state = pltpu.VMEM((rows, SSM_STATE), jnp.float32)
    y = pl.pallas_call(
        functools.partial(_s5_kernel, bsz, n_ctx // S5_CHUNK, n_chunks),
        grid=(SSM_GROUPS,),
        in_specs=[pl.BlockSpec((1, rows, width), lambda g: (g, 0, 0)),
                  grp(arow), grp(acol), grp(bmat), grp(cmat), grp(ctmat)],
        out_specs=pl.BlockSpec((1, rows, width), lambda g: (g, 0, 0)),
        out_shape=jax.ShapeDtypeStruct((SSM_GROUPS, rows, width), jnp.float32),
        scratch_shapes=[state] * 8,
        compiler_params=pltpu.CompilerParams(vmem_limit_bytes=VMEM_LIMIT),
        name="s5_scan",
    )(ug, arow, acol, bmat, cmat, ctmat)
    y = y.reshape(SSM_GROUPS, n_chunks, bsz, S5_CHUNK, SSM_GROUP)
    return y.transpose(2, 1, 3, 0, 4).reshape(bsz, n_tok, SSM_WIDTH)


def _merge_kernel(ctx_blocks, h_ref, mod_ref, attn_ref, y_ref, u_ref, ga_ref, gb_ref,
                  dskip_ref, wglu_ref, bglu_ref, wba_ref, wbs_ref, wo_ref, o_ref):
    f32, bf16 = jnp.float32, jnp.bfloat16
    b, j = pl.program_id(0), pl.program_id(1)
    row = _mod_row(b, j, ctx_blocks)
    gate = mod_ref[pl.ds(row, 1), 2 * D_MODEL:3 * D_MODEL]
    y = jax.nn.gelu(y_ref[0] + dskip_ref[...] * u_ref[0])
    glu = jnp.dot(y.astype(bf16), wglu_ref[...], preferred_element_type=f32) + bglu_ref[...]
    ssm = y * jax.nn.sigmoid(glu)
    ba = jnp.dot(attn_ref[0], wba_ref[...], preferred_element_type=f32)
    bs = jnp.dot(ssm.astype(bf16), wbs_ref[...], preferred_element_type=f32)
    mix = jax.nn.sigmoid(ga_ref[0]) * ba + jax.nn.sigmoid(gb_ref[0]) * bs
    out = jnp.dot(mix.astype(bf16), wo_ref[...], preferred_element_type=f32)
    o_ref[0] = h_ref[0] + gate * out


def branch_merge(h, mod, attn, y, u, ga, gb, d_skip, w_glu, b_glu, w_ba, w_bs, w_o, n_ctx):
    bsz, n_tok, _ = h.shape
    nblk = n_tok // ROW_BLOCK
    row_spec = lambda w: pl.BlockSpec((1, ROW_BLOCK, w), lambda b, j: (b, j, 0))
    full = lambda a: pl.BlockSpec(a.shape, lambda b, j: (0,) * a.ndim)
    return pl.pallas_call(
        functools.partial(_merge_kernel, n_ctx // ROW_BLOCK),
        grid=(bsz, nblk),
        in_specs=[row_spec(D_MODEL), full(mod), row_spec(ATTN_WIDTH), row_spec(SSM_WIDTH),
                  row_spec(SSM_WIDTH), row_spec(D_MODEL), row_spec(D_MODEL),
                  full(d_skip), full(w_glu), full(b_glu), full(w_ba), full(w_bs), full(w_o)],
        out_specs=row_spec(D_MODEL),
        out_shape=jax.ShapeDtypeStruct(h.shape, jnp.float32),
        compiler_params=pltpu.CompilerParams(vmem_limit_bytes=VMEM_LIMIT),
        name="branch_merge",
    )(h, mod, attn, y, u, ga, gb, d_skip, w_glu, b_glu, w_ba, w_bs, w_o)


def _top16(s, payload, n_rows, vals_ref, pay_ref, base):
    rows = lax.broadcasted_iota(jnp.int32, s.shape, 0)

    def body(r, s):
        m = jnp.max(s, axis=0, keepdims=True)
        pos = jnp.min(jnp.where(s == m, rows, n_rows), axis=0, keepdims=True)
        hit = rows == pos
        vals_ref[pl.ds(base + r, 1), :] = m
        if payload is None:
            pay_ref[pl.ds(base + r, 1), :] = pos
        else:
            pay_ref[pl.ds(base + r, 1), :] = jnp.max(jnp.where(hit, payload, -1), axis=0,
                                                     keepdims=True)
        return jnp.where(hit, -jnp.inf, s)

    lax.fori_loop(0, PEER_TOPK, body, s)


def _peer_query_kernel(ctx_blocks, h_ref, mod_ref, g_ref, wq_ref, keys_ref,
                       xn_ref, idx_ref, gate_ref, q_scr, sv, si, bv, be):
    f32, bf16 = jnp.float32, jnp.bfloat16
    b, j = pl.program_id(0), pl.program_id(1)
    row = _mod_row(b, j, ctx_blocks)
    shift = mod_ref[pl.ds(row, 1), 3 * D_MODEL:4 * D_MODEL]
    scale = mod_ref[pl.ds(row, 1), 4 * D_MODEL:5 * D_MODEL]
    x = h_ref[0]
    xn = x * lax.rsqrt(jnp.mean(x * x, axis=-1, keepdims=True) + EPS) * g_ref[...]
    xm = xn * (1.0 + scale) + shift
    xn_ref[0] = xm
    xb = xm.astype(bf16)
    for hp in range(2 * PEER_HEADS):
        lo = hp * PEER_HALF
        q_scr[hp] = jnp.dot(xb, wq_ref[:, lo:lo + PEER_HALF],
                            preferred_element_type=f32).astype(bf16)

    half = PEER_TOPK // 2

    def head_body(hd, carry):
        for p in range(2):
            s = lax.dot_general(keys_ref[p], q_scr[2 * hd + p], _NT,
                                preferred_element_type=f32)
            _top16(s, None, PEER_N_KEYS, sv, si, p * PEER_TOPK)
        s1, s2 = sv[0:PEER_TOPK, :], sv[PEER_TOPK:2 * PEER_TOPK, :]
        i1, i2 = si[0:PEER_TOPK, :], si[PEER_TOPK:2 * PEER_TOPK, :]
        cand, cexp = [], []
        for i in range(half):
            size = PEER_TOPK if i == 0 else half
            blk = s1[i:i + 1, :] + s2[0:size, :]
            keep = PEER_TOPK // (i + 1)
            if keep < size:
                jrow = lax.broadcasted_iota(jnp.int32, (size, 1), 0)
                blk = jnp.where(jrow < keep, blk, -jnp.inf)
            cand.append(blk)
            cexp.append(i1[i:i + 1, :] * PEER_N_KEYS + i2[0:size, :])
        cand.append(s1[half:PEER_TOPK, :] + s2[0:1, :])
        cexp.append(i1[half:PEER_TOPK, :] * PEER_N_KEYS + i2[0:1, :])
        cand = jnp.concatenate(cand, axis=0)
        cexp = jnp.concatenate(cexp, axis=0)
        base = pl.multiple_of(hd * PEER_TOPK, PEER_TOPK)
        _top16(cand, cexp, cand.shape[0], bv, be, base)
        return carry

    lax.fori_loop(0, PEER_HEADS, head_body, 0)

    for hd in range(PEER_HEADS):
        best = bv[hd * PEER_TOPK:(hd + 1) * PEER_TOPK, :]
        e = jnp.exp(best - jnp.max(best, axis=0, keepdims=True))
        bv[hd * PEER_TOPK:(hd + 1) * PEER_TOPK, :] = e / jnp.sum(e, axis=0, keepdims=True)
    for t in range(ROW_BLOCK // PEER_TOK):
        idx_ref[t] = be[:, t * PEER_TOK:(t + 1) * PEER_TOK].T
        gate_ref[t] = bv[:, t * PEER_TOK:(t + 1) * PEER_TOK]


def peer_query(h, mod, g, wq_bf16, keys_bf16, n_ctx):
    bsz, n_tok, _ = h.shape
    nblk = n_tok // ROW_BLOCK
    sub = ROW_BLOCK // PEER_TOK
    n_pblk = bsz * n_tok // PEER_TOK
    row_spec = pl.BlockSpec((1, ROW_BLOCK, D_MODEL), lambda b, j: (b, j, 0))
    full = lambda a: pl.BlockSpec(a.shape, lambda b, j: (0,) * a.ndim)
    pick_spec = pl.BlockSpec((sub, PEER_PICKS, PEER_TOK), lambda b, j: (b * nblk + j, 0, 0))
    f32, i32 = jnp.float32, jnp.int32
    return pl.pallas_call(
        functools.partial(_peer_query_kernel, n_ctx // ROW_BLOCK),
        grid=(bsz, nblk),
        in_specs=[row_spec, full(mod), full(g), full(wq_bf16), full(keys_bf16)],
        out_specs=[row_spec, pick_spec, pick_spec],
        out_shape=[jax.ShapeDtypeStruct(h.shape, f32),
                   jax.ShapeDtypeStruct((n_pblk, PEER_PICKS, PEER_TOK), i32),
                   jax.ShapeDtypeStruct((n_pblk, PEER_PICKS, PEER_TOK), f32)],
        scratch_shapes=[pltpu.VMEM((2 * PEER_HEADS, ROW_BLOCK, PEER_HALF), jnp.bfloat16),
                        pltpu.VMEM((2 * PEER_TOPK, ROW_BLOCK), f32),
                        pltpu.VMEM((2 * PEER_TOPK, ROW_BLOCK), i32),
                        pltpu.VMEM((PEER_PICKS, ROW_BLOCK), f32),
                        pltpu.VMEM((PEER_PICKS, ROW_BLOCK), i32)],
        compiler_params=pltpu.CompilerParams(vmem_limit_bytes=VMEM_LIMIT),
        name="peer_query",
    )(h, mod, g, wq_bf16, keys_bf16)


def pack_rows(rows):
    bits = lax.bitcast_convert_type(rows.astype(jnp.bfloat16), jnp.uint16).astype(jnp.uint32)
    bits = bits.reshape(-1, EXPERT_ROWS, 2, 128)
    words = bits[:, :, 0, :] | (bits[:, :, 1, :] << 16)
    return lax.bitcast_convert_type(words, jnp.int32)


def _stage_experts(idx_ref, tab_ref, t, buf, fn):
    base = t * PEER_PICKS
    for k in range(PEER_PICKS):
        e = idx_ref[0, 0, base + k]
        buf[k * EXPERT_ROWS:(k + 1) * EXPERT_ROWS, :] = fn(tab_ref[e])


def _pick_rows_mask():
    tile_rows = 2 * EXPERT_ROWS
    shape = (PEER_PICKS, PEER_PICKS * tile_rows)
    return (lax.broadcasted_iota(jnp.int32, shape, 1) // tile_rows
            == lax.broadcasted_iota(jnp.int32, shape, 0))


def _peer_score_kernel(idx_ref, x_ref, gate_ref, tab_ref, w_ref, *pbufs):
    f32, bf16, i32 = jnp.float32, jnp.bfloat16, jnp.int32
    sel = _pick_rows_mask().astype(bf16)
    lane = lax.broadcasted_iota(i32, (PEER_PICKS, PEER_TOK), 1)

    def group_body(g, acc):
        for a, pbuf in enumerate(pbufs):
            t = g * len(pbufs) + a
            xt = pltpu.bitcast(x_ref[t], bf16)
            _stage_experts(idx_ref, tab_ref, t, pbuf,
                           lambda w: pltpu.bitcast(pltpu.bitcast(w, bf16) * xt, i32))
        for a, pbuf in enumerate(pbufs):
            t = g * len(pbufs) + a
            part = jnp.dot(sel, pltpu.bitcast(pbuf[...], bf16), preferred_element_type=f32)
            acc = jnp.where(lane == t, jnp.sum(part, axis=1, keepdims=True), acc)
        return acc

    scores = lax.fori_loop(0, PEER_TOK // len(pbufs), group_body,
                           jnp.zeros((PEER_PICKS, PEER_TOK), f32))
    w = gate_ref[0] * jax.nn.gelu(scores)
    w_ref[...] = jnp.dot(w.T.astype(bf16), sel, preferred_element_type=f32)


def _peer_mix_kernel(idx_ref, wexp_ref, h_ref, gate_ref, tab_ref, o_ref, *vbufs):
    f32, bf16, i32 = jnp.float32, jnp.bfloat16, jnp.int32
    tile_rows = 2 * EXPERT_ROWS
    shape = (tile_rows, PEER_PICKS * tile_rows)
    diag = (lax.broadcasted_iota(i32, shape, 1) % tile_rows
            == lax.broadcasted_iota(i32, shape, 0))

    def group_body(g, carry):
        for a, vbuf in enumerate(vbufs):
            _stage_experts(idx_ref, tab_ref, g * len(vbufs) + a, vbuf, lambda w: w)
        for a, vbuf in enumerate(vbufs):
            t = g * len(vbufs) + a
            wmat = jnp.where(diag, wexp_ref[pl.ds(t, 1), :], 0.0).astype(bf16)
            out = jnp.dot(wmat, pltpu.bitcast(vbuf[...], bf16), preferred_element_type=f32)
            o_ref[t] = h_ref[t] + gate_ref[0, 0] * out
        return carry

    lax.fori_loop(0, PEER_TOK // len(vbufs), group_body, 0)


def _table_spec(tab):
    return pl.BlockSpec(tab.shape, lambda i: (0, 0, 0), pipeline_mode=pl.Buffered(1))


def peer_experts(h, xn, idx, gates, mod, u_packed, v_packed, n_ctx):
    bsz, n_tok, _ = h.shape
    n_pblk = idx.shape[0]
    blk_per_batch = n_tok // PEER_TOK
    ctx_pblk = n_ctx // PEER_TOK
    flat = PEER_PICKS * PEER_TOK
    tiles = D_MODEL // 128
    smem_spec = pl.BlockSpec((1, 1, flat), lambda i: (i, 0, 0), memory_space=pltpu.SMEM)
    pick_spec = pl.BlockSpec((1, PEER_PICKS, PEER_TOK), lambda i: (i, 0, 0))
    tok_spec = pl.BlockSpec((PEER_TOK, tiles, 128), lambda i: (i, 0, 0))
    word_spec = pl.BlockSpec((PEER_TOK, EXPERT_ROWS, 128), lambda i: (i, 0, 0))
    wexp_spec = pl.BlockSpec((PEER_TOK, D_MODEL), lambda i: (i, 0))
    stage = [pltpu.VMEM((PEER_PICKS * EXPERT_ROWS, 128), jnp.int32)] * PEER_STAGE
    idx_flat = idx.reshape(n_pblk, 1, flat)
    wexp = pl.pallas_call(
        _peer_score_kernel,
        grid=(n_pblk,),
        in_specs=[smem_spec, word_spec, pick_spec, _table_spec(u_packed)],
        out_specs=wexp_spec,
        out_shape=jax.ShapeDtypeStruct((bsz * n_tok, D_MODEL), jnp.float32),
        scratch_shapes=stage,
        compiler_params=pltpu.CompilerParams(vmem_limit_bytes=VMEM_LIMIT),
        name="peer_scores",
    )(idx_flat, pack_rows(xn.reshape(-1, D_MODEL)), gates, u_packed)

    def gate_map(i):
        bi, ji = i // blk_per_batch, i % blk_per_batch
        return (jnp.where(ji < ctx_pblk, 0, bi + 1), N_MOD - 1, 0, 0)

    mod_tiles = mod.reshape(MOD_ROWS, N_MOD, tiles, 128)
    out = pl.pallas_call(
        _peer_mix_kernel,
        grid=(n_pblk,),
        in_specs=[smem_spec, wexp_spec, tok_spec,
                  pl.BlockSpec((1, 1, tiles, 128), gate_map), _table_spec(v_packed)],
        out_specs=tok_spec,
        out_shape=jax.ShapeDtypeStruct((bsz * n_tok, tiles, 128), jnp.float32),
        scratch_shapes=stage,
        compiler_params=pltpu.CompilerParams(vmem_limit_bytes=VMEM_LIMIT),
        name="peer_mix",
    )(idx_flat, wexp, h.reshape(-1, tiles, 128), mod_tiles, v_packed)
    return out.reshape(h.shape)


def kernel(x, c, ctx, c_ctx, w_mod, b_mod, norm_mix_g, w_in, q_norm_g, k_norm_g, ssm_a_re, ssm_a_im, ssm_log_dt, ssm_b_re, ssm_b_im, ssm_c_re, ssm_c_im, ssm_d, w_glu, b_glu, w_branch_attn, w_branch_ssm, w_out, norm_ffn_g, peer_w_q, peer_keys, peer_u, peer_v):
    bf16 = jnp.bfloat16
    bsz, n_lat, _ = x.shape
    n_ctx = ctx.shape[1]
    depth = w_mod.shape[0]
    assert bsz + 1 <= MOD_ROWS and n_ctx % ROW_BLOCK == 0 and n_lat % ROW_BLOCK == 0

    cc = jnp.zeros((MOD_ROWS, D_MODEL), jnp.float32).at[0].set(c_ctx).at[1:bsz + 1].set(c)
    mods = modulation(cc, w_mod, b_mod)
    rope = rope_tables(n_ctx, n_lat)
    h = jnp.concatenate([ctx, x], axis=1)
    row = lambda a: a.reshape(1, -1)

    for i in range(depth):
        mod = mods[i]
        q, k, v, u, ga, gb = in_projection(h, mod, row(norm_mix_g[i]), w_in[i].astype(bf16), rope,
                                           row(q_norm_g[i]), row(k_norm_g[i]), n_ctx)
        attn = attention(q, k, v, n_ctx)
        y = s5_scan(u, ssm_a_re[i], ssm_a_im[i], ssm_log_dt[i], ssm_b_re[i], ssm_b_im[i],
                    ssm_c_re[i], ssm_c_im[i], n_ctx)
        h = branch_merge(h, mod, attn, y, u, ga, gb, row(ssm_d[i]), w_glu[i].astype(bf16),
                         row(b_glu[i]), w_branch_attn[i].astype(bf16),
                         w_branch_ssm[i].astype(bf16), w_out[i].astype(bf16), n_ctx)
        xn, idx, gates = peer_query(h, mod, row(norm_ffn_g[i]), peer_w_q[i].astype(bf16),
                                    peer_keys[i].astype(bf16), n_ctx)
        h = peer_experts(h, xn, idx, gates, mod, pack_rows(peer_u[i]), pack_rows(peer_v[i]),
                         n_ctx)
    return h[:, n_ctx:, :]
```

```python
import functools
import math

import jax
import jax.numpy as jnp
from jax import lax
from jax.experimental import pallas as pl
from jax.experimental.pallas import tpu as pltpu

D_MODEL = 1024
GRID_W = 64
HEAD_DIM = 128
N_Q_HEADS = 8
N_KV_HEADS = 2
GQA_GROUP = N_Q_HEADS // N_KV_HEADS
ATTN_WIDTH = N_Q_HEADS * HEAD_DIM
KV_WIDTH = N_KV_HEADS * HEAD_DIM
ROPE_THETA = 10000.0
ROPE_FREQS = HEAD_DIM // 4
ATTN_SCALE = HEAD_DIM ** -0.5
SSM_WIDTH = D_MODEL // 2
SSM_GROUP = 16
SSM_GROUPS = SSM_WIDTH // SSM_GROUP
SSM_STATE = 64
IN_WIDTH = ATTN_WIDTH + 2 * KV_WIDTH + SSM_WIDTH + 2 * D_MODEL
PEER_HEADS = 8
PEER_N_KEYS = 128
PEER_N_EXPERTS = PEER_N_KEYS ** 2
PEER_QUERY_DIM = 256
PEER_HALF = PEER_QUERY_DIM // 2
PEER_TOPK = 16
PEER_PICKS = PEER_HEADS * PEER_TOPK
N_MOD = 6
EPS = 1e-6

ROW_BLOCK = 256
ATTN_KEYS = 1024
MOD_ROWS = 16
S5_CHUNK = 16
PEER_TOK = 128
EXPERT_ROWS = 4
PEER_STAGE = 8
PEER_CHUNK = 32
VMEM_LIMIT = 56 * 1024 * 1024

_NT = (((1,), (1,)), ((), ()))


def _mod_row(b, j, ctx_blocks):
    return jnp.where(j < ctx_blocks, 0, b + 1)


def _mod_kernel(c_ref, w_ref, b_ref, o_ref):
    c = c_ref[...]
    s = c * jax.nn.sigmoid(c)
    o_ref[0] = jnp.dot(s.astype(jnp.bfloat16), w_ref[0].astype(jnp.bfloat16),
                       preferred_element_type=jnp.float32) + b_ref[0]


def modulation(cc, w_mod, b_mod):
    depth = w_mod.shape[0]
    nblk = 1536
    width = N_MOD * D_MODEL
    return pl.pallas_call(
        _mod_kernel,
        grid=(depth, width // nblk),
        in_specs=[pl.BlockSpec((MOD_ROWS, D_MODEL), lambda l, n: (0, 0)),
                  pl.BlockSpec((1, D_MODEL, nblk), lambda l, n: (l, 0, n)),
                  pl.BlockSpec((1, 1, nblk), lambda l, n: (l, 0, n))],
        out_specs=pl.BlockSpec((1, MOD_ROWS, nblk), lambda l, n: (l, 0, n)),
        out_shape=jax.ShapeDtypeStruct((depth, MOD_ROWS, width), jnp.float32),
        name="modulation",
    )(cc, w_mod, b_mod.reshape(depth, 1, width))


def _inproj_kernel(ctx_blocks, h_ref, mod_ref, g_ref, w_ref, cos_ref, sa_ref, sb_ref,
                   qg_ref, kg_ref, q_ref, k_ref, v_ref, u_ref, ga_ref, gb_ref):
    b, j = pl.program_id(0), pl.program_id(1)
    row = _mod_row(b, j, ctx_blocks)
    shift = mod_ref[pl.ds(row, 1), 0:D_MODEL]
    scale = mod_ref[pl.ds(row, 1), D_MODEL:2 * D_MODEL]
    x = h_ref[0]
    xn = x * lax.rsqrt(jnp.mean(x * x, axis=-1, keepdims=True) + EPS) * g_ref[...]
    xm = (xn * (1.0 + scale) + shift).astype(jnp.bfloat16)

    cos, sa, sb = cos_ref[...], sa_ref[...], sb_ref[...]

    def norm_rope(z, g, out_scale):
        zn = z * lax.rsqrt(jnp.mean(z * z, axis=-1, keepdims=True) + EPS) * g
        r = zn * cos + pltpu.roll(zn, 96, 1) * sa + pltpu.roll(zn, 32, 1) * sb
        return r * out_scale

    for hd in range(N_Q_HEADS):
        lo = hd * HEAD_DIM
        z = jnp.dot(xm, w_ref[:, lo:lo + HEAD_DIM], preferred_element_type=jnp.float32)
        q_ref[0, :, lo:lo + HEAD_DIM] = norm_rope(z, qg_ref[...], ATTN_SCALE * math.log2(math.e)).astype(q_ref.dtype)
    for hd in range(N_KV_HEADS):
        lo = hd * HEAD_DIM
        z = jnp.dot(xm, w_ref[:, ATTN_WIDTH + lo:ATTN_WIDTH + lo + HEAD_DIM],
                    preferred_element_type=jnp.float32)
        k_ref[0, :, lo:lo + HEAD_DIM] = norm_rope(z, kg_ref[...], 1.0).astype(k_ref.dtype)
    off = ATTN_WIDTH + KV_WIDTH
    v_ref[0] = jnp.dot(xm, w_ref[:, off:off + KV_WIDTH],
                       preferred_element_type=jnp.float32).astype(v_ref.dtype)
    off += KV_WIDTH
    u_ref[0] = jnp.dot(xm, w_ref[:, off:off + SSM_WIDTH], preferred_element_type=jnp.float32)
    off += SSM_WIDTH
    ga_ref[0] = jnp.dot(xm, w_ref[:, off:off + D_MODEL], preferred_element_type=jnp.float32)
    off += D_MODEL
    gb_ref[0] = jnp.dot(xm, w_ref[:, off:off + D_MODEL], preferred_element_type=jnp.float32)


def in_projection(h, mod, g, w_in_bf16, rope, qg, kg, n_ctx):
    bsz, n_tok, _ = h.shape
    nblk = n_tok // ROW_BLOCK
    cos, sa, sb = rope
    row_spec = lambda w: pl.BlockSpec((1, ROW_BLOCK, w), lambda b, j: (b, j, 0))
    full = lambda a: pl.BlockSpec(a.shape, lambda b, j: (0,) * a.ndim)
    rope_spec = pl.BlockSpec((ROW_BLOCK, HEAD_DIM), lambda b, j: (j, 0))
    f32, bf16 = jnp.float32, jnp.bfloat16
    out_shape = [jax.ShapeDtypeStruct((bsz, n_tok, ATTN_WIDTH), bf16),
                 jax.ShapeDtypeStruct((bsz, n_tok, KV_WIDTH), bf16),
                 jax.ShapeDtypeStruct((bsz, n_tok, KV_WIDTH), bf16),
                 jax.ShapeDtypeStruct((bsz, n_tok, SSM_WIDTH), f32),
                 jax.ShapeDtypeStruct((bsz, n_tok, D_MODEL), f32),
                 jax.ShapeDtypeStruct((bsz, n_tok, D_MODEL), f32)]
    return pl.pallas_call(
        functools.partial(_inproj_kernel, n_ctx // ROW_BLOCK),
        grid=(bsz, nblk),
        in_specs=[row_spec(D_MODEL), full(mod), full(g), full(w_in_bf16),
                  rope_spec, rope_spec, rope_spec, full(qg), full(kg)],
        out_specs=[row_spec(ATTN_WIDTH), row_spec(KV_WIDTH), row_spec(KV_WIDTH),
                   row_spec(SSM_WIDTH), row_spec(D_MODEL), row_spec(D_MODEL)],
        out_shape=out_shape,
        compiler_params=pltpu.CompilerParams(vmem_limit_bytes=VMEM_LIMIT),
        name="in_projection",
    )(h, mod, g, w_in_bf16, cos, sa, sb, qg, kg)


def rope_tables(n_ctx, n_lat):
    pos = jnp.arange(n_lat)
    inv_freq = ROPE_THETA ** (-jnp.arange(ROPE_FREQS, dtype=jnp.float32) / ROPE_FREQS)
    ang_row = (pos // GRID_W).astype(jnp.float32)[:, None] * inv_freq
    ang_col = (pos % GRID_W).astype(jnp.float32)[:, None] * inv_freq
    ang = jnp.concatenate([ang_row, ang_row, ang_col, ang_col], axis=-1)
    cos, sin = jnp.cos(ang), jnp.sin(ang)
    even = ((jnp.arange(HEAD_DIM) // ROPE_FREQS) % 2 == 0)[None, :]
    sa = jnp.where(even, -sin, 0.0)
    sb = jnp.where(even, 0.0, sin)
    pad = lambda t, v: jnp.concatenate([jnp.full((n_ctx, HEAD_DIM), v, jnp.float32), t], axis=0)
    return pad(cos, 1.0), pad(sa, 0.0), pad(sb, 0.0)


def _attn_kernel(n_ctx, n_lat, key_chunk, ctx_blocks, q_ref, k_ref, v_ref, o_ref, m_scr, l_scr, acc_scr):
    f32 = jnp.float32
    j = pl.program_id(2)
    q = jnp.concatenate([q_ref[0, :, hd * HEAD_DIM:(hd + 1) * HEAD_DIM]
                         for hd in range(GQA_GROUP)], axis=0)

    kc, vc = k_ref[0, 0:n_ctx, :], v_ref[0, 0:n_ctx, :]
    s = lax.dot_general(q, kc, _NT, preferred_element_type=f32)
    m = jnp.max(s, axis=-1, keepdims=True)
    p = jnp.exp2(s - m)
    m_scr[...] = m
    l_scr[...] = jnp.sum(p, axis=-1, keepdims=True)
    acc_scr[...] = jnp.dot(p.astype(vc.dtype), vc, preferred_element_type=f32)

    @pl.when(j >= ctx_blocks)
    def _():
        def body(c, carry):
            r0 = pl.multiple_of(n_ctx + c * key_chunk, ROW_BLOCK)
            kc = k_ref[0, pl.ds(r0, key_chunk), :]
            vc = v_ref[0, pl.ds(r0, key_chunk), :]
            s = lax.dot_general(q, kc, _NT, preferred_element_type=f32)
            m_old = m_scr[...]
            m_new = jnp.maximum(m_old, jnp.max(s, axis=-1, keepdims=True))
            alpha = jnp.exp2(m_old - m_new)
            p = jnp.exp2(s - m_new)
            m_scr[...] = m_new
            l_scr[...] = alpha * l_scr[...] + jnp.sum(p, axis=-1, keepdims=True)
            acc_scr[...] = alpha * acc_scr[...] + jnp.dot(p.astype(vc.dtype), vc,
                                                          preferred_element_type=f32)
            return carry

        lax.fori_loop(0, n_lat // key_chunk, body, 0)

    out = acc_scr[...] / l_scr[...]
    for hd in range(GQA_GROUP):
        o_ref[0, :, hd * HEAD_DIM:(hd + 1) * HEAD_DIM] = (
            out[hd * ROW_BLOCK:(hd + 1) * ROW_BLOCK]).astype(o_ref.dtype)


def attention(q, k, v, n_ctx):
    bsz, n_tok, _ = q.shape
    nblk = n_tok // ROW_BLOCK
    gw = GQA_GROUP * HEAD_DIM
    rows = GQA_GROUP * ROW_BLOCK
    key_chunk = math.gcd(n_tok - n_ctx, ATTN_KEYS)
    return pl.pallas_call(
        functools.partial(_attn_kernel, n_ctx, n_tok - n_ctx, key_chunk, n_ctx // ROW_BLOCK),
        grid=(bsz, N_KV_HEADS, nblk),
        in_specs=[pl.BlockSpec((1, ROW_BLOCK, gw), lambda b, g, j: (b, j, g)),
                  pl.BlockSpec((1, n_tok, HEAD_DIM), lambda b, g, j: (b, 0, g)),
                  pl.BlockSpec((1, n_tok, HEAD_DIM), lambda b, g, j: (b, 0, g))],
        out_specs=pl.BlockSpec((1, ROW_BLOCK, gw), lambda b, g, j: (b, j, g)),
        out_shape=jax.ShapeDtypeStruct((bsz, n_tok, ATTN_WIDTH), jnp.bfloat16),
        scratch_shapes=[pltpu.VMEM((rows, 1), jnp.float32), pltpu.VMEM((rows, 1), jnp.float32),
                        pltpu.VMEM((rows, HEAD_DIM), jnp.float32)],
        compiler_params=pltpu.CompilerParams(vmem_limit_bytes=VMEM_LIMIT),
        name="attention",
    )(q, k, v)


def _cpow(e, lam_re_dt, lam_im_dt):
    mag = jnp.exp(e * lam_re_dt)
    ang = e * lam_im_dt
    return mag * jnp.cos(ang), mag * jnp.sin(ang)


def _s5_kernel(n_batch, ctx_chunks, n_chunks,
               u_ref, arow_ref, acol_ref, b_ref, c_ref, ct_ref, y_ref,
               sfr, sfi, sbr, sbi, hfr, hfi, hbr, hbi):
    f32, bf16 = jnp.float32, jnp.bfloat16
    width = S5_CHUNK * SSM_GROUP
    lane = lax.broadcasted_iota(jnp.int32, (1, width), 1)
    lane_blk = lane // SSM_GROUP
    expand = (lax.broadcasted_iota(jnp.int32, (SSM_GROUP, width), 1) % SSM_GROUP
              == lax.broadcasted_iota(jnp.int32, (SSM_GROUP, width), 0)).astype(f32)
    rowi = lax.broadcasted_iota(jnp.int32, (width, 1), 0) // SSM_GROUP
    u = u_ref[0].astype(bf16)

    m_tot = jnp.zeros((width, width), f32)
    proj, read, step = [], [], []
    for d in range(2):
        fwd = d == 0
        are_c, aim_c, dt_c = acol_ref[d, 0, 0], acol_ref[d, 0, 1], jnp.exp(acol_ref[d, 0, 2])
        lr_c, li_c = are_c * dt_c, aim_c * dt_c
        abr, abi = _cpow(1.0, lr_c, li_c)
        den = are_c * are_c + aim_c * aim_c
        nr, ni = abr - 1.0, abi
        k_r = (nr * are_c + ni * aim_c) / den
        k_i = (ni * are_c - nr * aim_c) / den
        b_re, b_im = b_ref[d, 0, 0], b_ref[d, 0, 1]
        bb_r = k_r * b_re - k_i * b_im
        bb_i = k_r * b_im + k_i * b_re
        bt_r = jnp.dot(bb_r, expand, preferred_element_type=f32)
        bt_i = jnp.dot(bb_i, expand, preferred_element_type=f32)
        e_in = (S5_CHUNK - 1 - lane_blk if fwd else lane_blk).astype(f32)
        ap_r, ap_i = _cpow(e_in, lr_c, li_c)
        proj.append(((ap_r * bt_r - ap_i * bt_i).astype(bf16),
                     (ap_r * bt_i + ap_i * bt_r).astype(bf16)))
        ct_r = jnp.dot(ct_ref[d, 0, 0], expand, preferred_element_type=f32)
        ct_i = jnp.dot(ct_ref[d, 0, 1], expand, preferred_element_type=f32)
        e_out = (lane_blk + 1 if fwd else S5_CHUNK - lane_blk).astype(f32)
        aq_r, aq_i = _cpow(e_out, lr_c, li_c)
        read.append(((ct_r * aq_r - ct_i * aq_i).astype(bf16),
                     (-(ct_r * aq_i + ct_i * aq_r)).astype(bf16)))
        are_r, aim_r, dt_r = arow_ref[d, 0, 0], arow_ref[d, 0, 1], jnp.exp(arow_ref[d, 0, 2])
        lr_r, li_r = are_r * dt_r, aim_r * dt_r
        step.append(_cpow(float(S5_CHUNK), lr_r, li_r))
        lag = (rowi if fwd else S5_CHUNK - 1 - rowi).astype(f32)
        al_r, al_i = _cpow(lag, lr_r, li_r)
        c_re = jnp.concatenate([c_ref[d, 0, 0]] * S5_CHUNK, axis=0)
        c_im = jnp.concatenate([c_ref[d, 0, 1]] * S5_CHUNK, axis=0)
        ca_r = c_re * al_r - c_im * al_i
        ca_i = c_re * al_i + c_im * al_r
        kmat = (jnp.dot(ca_r, bb_r, preferred_element_type=f32)
                - jnp.dot(ca_i, bb_i, preferred_element_type=f32))
        kt = jnp.dot(kmat, expand, preferred_element_type=f32)
        for s in range(S5_CHUNK):
            sh = (s if fwd else S5_CHUNK - 1 - s) * SSM_GROUP
            if sh == 0:
                shifted = kt
            elif fwd:
                shifted = jnp.concatenate([jnp.zeros((sh, width), f32), kt[:width - sh]], axis=0)
            else:
                shifted = jnp.concatenate([kt[sh:], jnp.zeros((sh, width), f32)], axis=0)
            m_tot = m_tot + jnp.where(lane_blk == s, shifted, 0.0)

    y_ref[0] = lax.dot_general(u, m_tot.astype(bf16), _NT, preferred_element_type=f32)
    sfr[...] = lax.dot_general(u, proj[0][0], _NT, preferred_element_type=f32)
    sfi[...] = lax.dot_general(u, proj[0][1], _NT, preferred_element_type=f32)
    sbr[...] = lax.dot_general(u, proj[1][0], _NT, preferred_element_type=f32)
    sbi[...] = lax.dot_general(u, proj[1][1], _NT, preferred_element_type=f32)

    (afr, afi), (abr_, abi_) = step

    def scan_body(i, carry):
        fr, fi, br, bi = carry
        rf = pl.multiple_of(i * n_batch, n_batch)
        hfr[pl.ds(rf, n_batch), :] = fr
        hfi[pl.ds(rf, n_batch), :] = fi
        nfr = afr * fr - afi * fi + sfr[pl.ds(rf, n_batch), :]
        nfi = afr * fi + afi * fr + sfi[pl.ds(rf, n_batch), :]
        cb = jnp.where(i < ctx_chunks, ctx_chunks - 1 - i, n_chunks - 1 + ctx_chunks - i)
        rb = pl.multiple_of(cb * n_batch, n_batch)
        hbr[pl.ds(rb, n_batch), :] = br
        hbi[pl.ds(rb, n_batch), :] = bi
        nbr = abr_ * br - abi_ * bi + sbr[pl.ds(rb, n_batch), :]
        nbi = abr_ * bi + abi_ * br + sbi[pl.ds(rb, n_batch), :]
        return nfr, nfi, nbr, nbi

    z = jnp.zeros((n_batch, SSM_STATE), f32)
    lax.fori_loop(0, n_chunks, scan_body, (z, z, z, z))

    y_ref[0] += (jnp.dot(hfr[...].astype(bf16), read[0][0], preferred_element_type=f32)
                 + jnp.dot(hfi[...].astype(bf16), read[0][1], preferred_element_type=f32)
                 + jnp.dot(hbr[...].astype(bf16), read[1][0], preferred_element_type=f32)
                 + jnp.dot(hbi[...].astype(bf16), read[1][1], preferred_element_type=f32))


def s5_scan(u, a_re, a_im, log_dt, b_re, b_im, c_re, c_im, n_ctx):
    bsz, n_tok, _ = u.shape
    n_chunks = n_tok // S5_CHUNK
    rows = n_chunks * bsz
    width = S5_CHUNK * SSM_GROUP
    ug = u.reshape(bsz, n_chunks, S5_CHUNK, SSM_GROUPS, SSM_GROUP)
    ug = ug.transpose(3, 1, 0, 2, 4).reshape(SSM_GROUPS, rows, width)
    ldt = jnp.broadcast_to(log_dt[..., None], a_re.shape)
    arow = jnp.stack([a_re, a_im, ldt], axis=2)[:, :, :, None, :]
    acol = jnp.stack([a_re, a_im, ldt], axis=2)[..., None]
    bmat = jnp.stack([b_re, b_im], axis=2)
    cmat = jnp.stack([c_re, c_im], axis=2)
    ctmat = jnp.swapaxes(cmat, -1, -2)
    grp = lambda a: pl.BlockSpec((2, 1) + a.shape[2:], lambda g: (0, g) + (0,) * (a.ndim - 2))
    state = pltpu.VMEM((rows, SSM_STATE), jnp.float32)
    y = pl.pallas_call(
        functools.partial(_s5_kernel, bsz, n_ctx // S5_CHUNK, n_chunks),
        grid=(SSM_GROUPS,),
        in_specs=[pl.BlockSpec((1, rows, width), lambda g: (g, 0, 0)),
                  grp(arow), grp(acol), grp(bmat), grp(cmat), grp(ctmat)],
        out_specs=pl.BlockSpec((1, rows, width), lambda g: (g, 0, 0)),
        out_shape=jax.ShapeDtypeStruct((SSM_GROUPS, rows, width), jnp.float32),
        scratch_shapes=[state] * 8,
        compiler_params=pltpu.CompilerParams(vmem_limit_bytes=VMEM_LIMIT),
        name="s5_scan",
    )(ug, arow, acol, bmat, cmat, ctmat)
    y = y.reshape(SSM_GROUPS, n_chunks, bsz, S5_CHUNK, SSM_GROUP)
    return y.transpose(2, 1, 3, 0, 4).reshape(bsz, n_tok, SSM_WIDTH)


def _merge_kernel(ctx_blocks, h_ref, mod_ref, attn_ref, y_ref, u_ref, ga_ref, gb_ref,
                  dskip_ref, wglu_ref, bglu_ref, wba_ref, wbs_ref, wo_ref, o_ref):
    f32, bf16 = jnp.float32, jnp.bfloat16
    b, j = pl.program_id(0), pl.program_id(1)
    row = _mod_row(b, j, ctx_blocks)
    gate = mod_ref[pl.ds(row, 1), 2 * D_MODEL:3 * D_MODEL]
    y = jax.nn.gelu(y_ref[0] + dskip_ref[...] * u_ref[0])
    glu = jnp.dot(y.astype(bf16), wglu_ref[...], preferred_element_type=f32) + bglu_ref[...]
    ssm = y * jax.nn.sigmoid(glu)
    ba = jnp.dot(attn_ref[0], wba_ref[...], preferred_element_type=f32)
    bs = jnp.dot(ssm.astype(bf16), wbs_ref[...], preferred_element_type=f32)
    mix = jax.nn.sigmoid(ga_ref[0]) * ba + jax.nn.sigmoid(gb_ref[0]) * bs
    out = jnp.dot(mix.astype(bf16), wo_ref[...], preferred_element_type=f32)
    o_ref[0] = h_ref[0] + gate * out


def branch_merge(h, mod, attn, y, u, ga, gb, d_skip, w_glu, b_glu, w_ba, w_bs, w_o, n_ctx):
    bsz, n_tok, _ = h.shape
    nblk = n_tok // ROW_BLOCK
    row_spec = lambda w: pl.BlockSpec((1, ROW_BLOCK, w), lambda b, j: (b, j, 0))
    full = lambda a: pl.BlockSpec(a.shape, lambda b, j: (0,) * a.ndim)
    return pl.pallas_call(
        functools.partial(_merge_kernel, n_ctx // ROW_BLOCK),
        grid=(bsz, nblk),
        in_specs=[row_spec(D_MODEL), full(mod), row_spec(ATTN_WIDTH), row_spec(SSM_WIDTH),
                  row_spec(SSM_WIDTH), row_spec(D_MODEL), row_spec(D_MODEL),
                  full(d_skip), full(w_glu), full(b_glu), full(w_ba), full(w_bs), full(w_o)],
        out_specs=row_spec(D_MODEL),
        out_shape=jax.ShapeDtypeStruct(h.shape, jnp.float32),
        compiler_params=pltpu.CompilerParams(vmem_limit_bytes=VMEM_LIMIT),
        name="branch_merge",
    )(h, mod, attn, y, u, ga, gb, d_skip, w_glu, b_glu, w_ba, w_bs, w_o)


def _top16(jobs, vals_ref, pay_ref):
    rows = [lax.broadcasted_iota(jnp.int32, s.shape, 0) for s, _, _ in jobs]

    def body(r, carry):
        nxt = []
        for s, (s0, payload, base), row in zip(carry, jobs, rows):
            m = jnp.max(s, axis=0, keepdims=True)
            pos = jnp.min(jnp.where(s == m, row, s0.shape[0]), axis=0, keepdims=True)
            hit = row == pos
            vals_ref[pl.ds(base + r, 1), :] = m
            if payload is None:
                pay_ref[pl.ds(base + r, 1), :] = pos
            else:
                pay_ref[pl.ds(base + r, 1), :] = jnp.max(jnp.where(hit, payload, -1), axis=0,
                                                         keepdims=True)
            nxt.append(jnp.where(hit, -jnp.inf, s))
        return tuple(nxt)

    lax.fori_loop(0, PEER_TOPK, body, tuple(s for s, _, _ in jobs))


def _peer_query_kernel(ctx_blocks, h_ref, mod_ref, g_ref, wq_ref, keys_ref,
                       xn_ref, idx_ref, gate_ref, q_scr, sv, si, bv, be):
    f32, bf16 = jnp.float32, jnp.bfloat16
    b, j = pl.program_id(0), pl.program_id(1)
    row = _mod_row(b, j, ctx_blocks)
    shift = mod_ref[pl.ds(row, 1), 3 * D_MODEL:4 * D_MODEL]
    scale = mod_ref[pl.ds(row, 1), 4 * D_MODEL:5 * D_MODEL]
    x = h_ref[0]
    xn = x * lax.rsqrt(jnp.mean(x * x, axis=-1, keepdims=True) + EPS) * g_ref[...]
    xm = xn * (1.0 + scale) + shift
    xn_ref[0] = xm
    xb = xm.astype(bf16)
    for hp in range(2 * PEER_HEADS):
        lo = hp * PEER_HALF
        q_scr[hp] = jnp.dot(xb, wq_ref[:, lo:lo + PEER_HALF],
                            preferred_element_type=f32).astype(bf16)

    half = PEER_TOPK // 2

    def head_body(hd, carry):
        _top16([(lax.dot_general(keys_ref[p], q_scr[2 * hd + p], _NT,
                                 preferred_element_type=f32),
                 None, p * PEER_TOPK) for p in range(2)], sv, si)
        s1, s2 = sv[0:PEER_TOPK, :], sv[PEER_TOPK:2 * PEER_TOPK, :]
        i1, i2 = si[0:PEER_TOPK, :], si[PEER_TOPK:2 * PEER_TOPK, :]
        cand, cexp = [], []
        for i in range(half):
            size = PEER_TOPK if i == 0 else half
            blk = s1[i:i + 1, :] + s2[0:size, :]
            keep = PEER_TOPK // (i + 1)
            if keep < size:
                jrow = lax.broadcasted_iota(jnp.int32, (size, 1), 0)
                blk = jnp.where(jrow < keep, blk, -jnp.inf)
            cand.append(blk)
            cexp.append(i1[i:i + 1, :] * PEER_N_KEYS + i2[0:size, :])
        cand.append(s1[half:PEER_TOPK, :] + s2[0:1, :])
        cexp.append(i1[half:PEER_TOPK, :] * PEER_N_KEYS + i2[0:1, :])
        cand = jnp.concatenate(cand, axis=0)
        cexp = jnp.concatenate(cexp, axis=0)
        base = pl.multiple_of(hd * PEER_TOPK, PEER_TOPK)
        _top16([(cand, cexp, base)], bv, be)
        return carry

    lax.fori_loop(0, PEER_HEADS, head_body, 0)

    for hd in range(PEER_HEADS):
        best = bv[hd * PEER_TOPK:(hd + 1) * PEER_TOPK, :]
        e = jnp.exp(best - jnp.max(best, axis=0, keepdims=True))
        bv[hd * PEER_TOPK:(hd + 1) * PEER_TOPK, :] = e / jnp.sum(e, axis=0, keepdims=True)
    for t in range(ROW_BLOCK // PEER_TOK):
        idx_ref[t] = be[:, t * PEER_TOK:(t + 1) * PEER_TOK].T * EXPERT_ROWS
        gate_ref[t] = bv[:, t * PEER_TOK:(t + 1) * PEER_TOK]


def peer_query(h, mod, g, wq_bf16, keys_bf16, n_ctx):
    bsz, n_tok, _ = h.shape
    nblk = n_tok // ROW_BLOCK
    sub = ROW_BLOCK // PEER_TOK
    n_pblk = bsz * n_tok // PEER_TOK
    row_spec = pl.BlockSpec((1, ROW_BLOCK, D_MODEL), lambda b, j: (b, j, 0))
    full = lambda a: pl.BlockSpec(a.shape, lambda b, j: (0,) * a.ndim)
    pick_spec = pl.BlockSpec((sub, PEER_PICKS, PEER_TOK), lambda b, j: (b * nblk + j, 0, 0))
    f32, i32 = jnp.float32, jnp.int32
    return pl.pallas_call(
        functools.partial(_peer_query_kernel, n_ctx // ROW_BLOCK),
        grid=(bsz, nblk),
        in_specs=[row_spec, full(mod), full(g), full(wq_bf16), full(keys_bf16)],
        out_specs=[row_spec, pick_spec, pick_spec],
        out_shape=[jax.ShapeDtypeStruct(h.shape, f32),
                   jax.ShapeDtypeStruct((n_pblk, PEER_PICKS, PEER_TOK), i32),
                   jax.ShapeDtypeStruct((n_pblk, PEER_PICKS, PEER_TOK), f32)],
        scratch_shapes=[pltpu.VMEM((2 * PEER_HEADS, ROW_BLOCK, PEER_HALF), jnp.bfloat16),
                        pltpu.VMEM((2 * PEER_TOPK, ROW_BLOCK), f32),
                        pltpu.VMEM((2 * PEER_TOPK, ROW_BLOCK), i32),
                        pltpu.VMEM((PEER_PICKS, ROW_BLOCK), f32),
                        pltpu.VMEM((PEER_PICKS, ROW_BLOCK), i32)],
        compiler_params=pltpu.CompilerParams(vmem_limit_bytes=VMEM_LIMIT),
        name="peer_query",
    )(h, mod, g, wq_bf16, keys_bf16)


def pack_rows(rows):
    bits = lax.bitcast_convert_type(rows.astype(jnp.bfloat16), jnp.uint16).astype(jnp.uint32)
    bits = bits.reshape(-1, EXPERT_ROWS, 2, 128)
    words = bits[:, :, 0, :] | (bits[:, :, 1, :] << 16)
    return lax.bitcast_convert_type(words, jnp.int32)


def _expert_chunks(idx_ref, tab_ref, t):
    row = idx_ref.at[0, 0, pl.ds(t * PEER_PICKS, PEER_PICKS)]
    for c in range(PEER_PICKS // PEER_CHUNK):
        tiles = [tab_ref[pl.ds(pl.multiple_of(row[c * PEER_CHUNK + j], EXPERT_ROWS),
                               EXPERT_ROWS), :] for j in range(PEER_CHUNK)]
        yield c, jnp.concatenate(tiles, axis=0)


def _pick_rows_mask():
    tile_rows = 2 * EXPERT_ROWS
    shape = (PEER_PICKS, PEER_PICKS * tile_rows)
    return (lax.broadcasted_iota(jnp.int32, shape, 1) // tile_rows
            == lax.broadcasted_iota(jnp.int32, shape, 0))


def _peer_score_kernel(idx_ref, x_ref, gate_ref, tab_ref, w_ref):
    f32, bf16, i32 = jnp.float32, jnp.bfloat16, jnp.int32
    sel = _pick_rows_mask().astype(bf16)
    lane = lax.broadcasted_iota(i32, (PEER_PICKS, PEER_TOK), 1)
    cols = PEER_CHUNK * 2 * EXPERT_ROWS

    def group_body(g, acc):
        for a in range(PEER_STAGE):
            t = g * PEER_STAGE + a
            xw = x_ref[t]
            xt = pltpu.bitcast(jnp.concatenate([xw] * PEER_CHUNK, axis=0), bf16)
            part = jnp.zeros((PEER_PICKS, 128), f32)
            for c, words in _expert_chunks(idx_ref, tab_ref, t):
                prod = pltpu.bitcast(words, bf16) * xt
                part = part + jnp.dot(sel[:, c * cols:(c + 1) * cols], prod,
                                      preferred_element_type=f32)
            acc = jnp.where(lane == t, jnp.sum(part, axis=1, keepdims=True), acc)
        return acc

    scores = lax.fori_loop(0, PEER_TOK // PEER_STAGE, group_body,
                           jnp.zeros((PEER_PICKS, PEER_TOK), f32))
    w = gate_ref[0] * jax.nn.gelu(scores)
    w_ref[...] = jnp.dot(w.T.astype(bf16), sel, preferred_element_type=f32)


def _peer_mix_kernel(idx_ref, wexp_ref, h_ref, gate_ref, tab_ref, o_ref):
    f32, bf16, i32 = jnp.float32, jnp.bfloat16, jnp.int32
    tile_rows = 2 * EXPERT_ROWS
    shape = (tile_rows, PEER_PICKS * tile_rows)
    diag = (lax.broadcasted_iota(i32, shape, 1) % tile_rows
            == lax.broadcasted_iota(i32, shape, 0))
    cols = PEER_CHUNK * tile_rows

    def group_body(g, carry):
        r0 = pl.multiple_of(g * PEER_STAGE, PEER_STAGE)
        tiles = []
        for a in range(PEER_STAGE):
            t = r0 + a
            wmat = jnp.where(diag, wexp_ref[pl.ds(t, 1), :], 0.0).astype(bf16)
            out = jnp.zeros((tile_rows, 128), f32)
            for c, words in _expert_chunks(idx_ref, tab_ref, t):
                out = out + jnp.dot(wmat[:, c * cols:(c + 1) * cols],
                                    pltpu.bitcast(words, bf16), preferred_element_type=f32)
            tiles.append(out)
        rows = jnp.concatenate(
            [jnp.concatenate([tile[r:r + 1, :] for tile in tiles], axis=0)
             for r in range(tile_rows)], axis=1)
        o_ref[pl.ds(r0, PEER_STAGE), :] = (h_ref[pl.ds(r0, PEER_STAGE), :]
                                           + gate_ref[0, 0] * rows)
        return carry

    lax.fori_loop(0, PEER_TOK // PEER_STAGE, group_body, 0)


def _table_spec(tab):
    return pl.BlockSpec(tab.shape, lambda i: (0,) * tab.ndim, pipeline_mode=pl.Buffered(1))


def peer_experts(h, xn, idx, gates, mod, u_packed, v_packed, n_ctx):
    bsz, n_tok, _ = h.shape
    n_pblk = idx.shape[0]
    blk_per_batch = n_tok // PEER_TOK
    ctx_pblk = n_ctx // PEER_TOK
    flat = PEER_PICKS * PEER_TOK
    smem_spec = pl.BlockSpec((1, 1, flat), lambda i: (i, 0, 0), memory_space=pltpu.SMEM)
    pick_spec = pl.BlockSpec((1, PEER_PICKS, PEER_TOK), lambda i: (i, 0, 0))
    word_spec = pl.BlockSpec((PEER_TOK, EXPERT_ROWS, 128), lambda i: (i, 0, 0))
    row_spec = pl.BlockSpec((PEER_TOK, D_MODEL), lambda i: (i, 0))
    idx_flat = idx.reshape(n_pblk, 1, flat)
    wexp = pl.pallas_call(
        _peer_score_kernel,
        grid=(n_pblk,),
        in_specs=[smem_spec, word_spec, pick_spec, _table_spec(u_packed)],
        out_specs=row_spec,
        out_shape=jax.ShapeDtypeStruct((bsz * n_tok, D_MODEL), jnp.float32),
        compiler_params=pltpu.CompilerParams(vmem_limit_bytes=VMEM_LIMIT),
        name="peer_scores",
    )(idx_flat, pack_rows(xn.reshape(-1, D_MODEL)), gates, u_packed)

    def gate_map(i):
        bi, ji = i // blk_per_batch, i % blk_per_batch
        return (jnp.where(ji < ctx_pblk, 0, bi + 1), N_MOD - 1, 0, 0)

    out = pl.pallas_call(
        _peer_mix_kernel,
        grid=(n_pblk,),
        in_specs=[smem_spec, row_spec, row_spec,
                  pl.BlockSpec((1, 1, 1, D_MODEL), gate_map), _table_spec(v_packed)],
        out_specs=row_spec,
        out_shape=jax.ShapeDtypeStruct((bsz * n_tok, D_MODEL), jnp.float32),
        compiler_params=pltpu.CompilerParams(vmem_limit_bytes=VMEM_LIMIT),
        name="peer_mix",
    )(idx_flat, wexp, h.reshape(-1, D_MODEL), mod.reshape(MOD_ROWS, N_MOD, 1, D_MODEL), v_packed)
    return out.reshape(h.shape)


def kernel(x, c, ctx, c_ctx, w_mod, b_mod, norm_mix_g, w_in, q_norm_g, k_norm_g, ssm_a_re, ssm_a_im, ssm_log_dt, ssm_b_re, ssm_b_im, ssm_c_re, ssm_c_im, ssm_d, w_glu, b_glu, w_branch_attn, w_branch_ssm, w_out, norm_ffn_g, peer_w_q, peer_keys, peer_u, peer_v):
    bf16 = jnp.bfloat16
    bsz, n_lat, _ = x.shape
    n_ctx = ctx.shape[1]
    depth = w_mod.shape[0]
    assert bsz + 1 <= MOD_ROWS and n_ctx % ROW_BLOCK == 0 and n_lat % ROW_BLOCK == 0

    cc = jnp.zeros((MOD_ROWS, D_MODEL), jnp.float32).at[0].set(c_ctx).at[1:bsz + 1].set(c)
    mods = modulation(cc, w_mod, b_mod)
    rope = rope_tables(n_ctx, n_lat)
    h = jnp.concatenate([ctx, x], axis=1)
    row = lambda a: a.reshape(1, -1)

    for i in range(depth):
        mod = mods[i]
        q, k, v, u, ga, gb = in_projection(h, mod, row(norm_mix_g[i]), w_in[i].astype(bf16), rope,
                                           row(q_norm_g[i]), row(k_norm_g[i]), n_ctx)
        attn = attention(q, k, v, n_ctx)
        y = s5_scan(u, ssm_a_re[i], ssm_a_im[i], ssm_log_dt[i], ssm_b_re[i], ssm_b_im[i],
                    ssm_c_re[i], ssm_c_im[i], n_ctx)
        h = branch_merge(h, mod, attn, y, u, ga, gb, row(ssm_d[i]), w_glu[i].astype(bf16),
                         row(b_glu[i]), w_branch_attn[i].astype(bf16),
                         w_branch_ssm[i].astype(bf16), w_out[i].astype(bf16), n_ctx)
        xn, idx, gates = peer_query(h, mod, row(norm_ffn_g[i]), peer_w_q[i].astype(bf16),
                                    peer_keys[i].astype(bf16), n_ctx)
        h = peer_experts(h, xn, idx, gates, mod, pack_rows(peer_u[i]).reshape(-1, 128),
                         pack_rows(peer_v[i]).reshape(-1, 128), n_ctx)
    return h[:, n_ctx:, :]
```

```python
import functools
import math

import jax
import jax.numpy as jnp
from jax import lax
from jax.experimental import pallas as pl
from jax.experimental.pallas import tpu as pltpu

D_MODEL = 1024
GRID_W = 64
HEAD_DIM = 128
N_Q_HEADS = 8
N_KV_HEADS = 2
GQA_GROUP = N_Q_HEADS // N_KV_HEADS
ATTN_WIDTH = N_Q_HEADS * HEAD_DIM
KV_WIDTH = N_KV_HEADS * HEAD_DIM
ROPE_THETA = 10000.0
ROPE_FREQS = HEAD_DIM // 4
ATTN_SCALE = HEAD_DIM ** -0.5
SSM_WIDTH = D_MODEL // 2
SSM_GROUP = 16
SSM_GROUPS = SSM_WIDTH // SSM_GROUP
SSM_STATE = 64
IN_WIDTH = ATTN_WIDTH + 2 * KV_WIDTH + SSM_WIDTH + 2 * D_MODEL
PEER_HEADS = 8
PEER_N_KEYS = 128
PEER_N_EXPERTS = PEER_N_KEYS ** 2
PEER_QUERY_DIM = 256
PEER_HALF = PEER_QUERY_DIM // 2
PEER_TOPK = 16
PEER_PICKS = PEER_HEADS * PEER_TOPK
N_MOD = 6
EPS = 1e-6

ROW_BLOCK = 256
ATTN_KEYS = 4096
MOD_ROWS = 16
S5_CHUNK = 16
PEER_TOK = 128
EXPERT_ROWS = 4
PEER_STAGE = 8
PEER_CHUNK = 32
VMEM_LIMIT = 56 * 1024 * 1024

_NT = (((1,), (1,)), ((), ()))


def _mod_row(b, j, ctx_blocks):
    return jnp.where(j < ctx_blocks, 0, b + 1)


def _mod_kernel(c_ref, w_ref, b_ref, o_ref):
    c = c_ref[...]
    s = c * jax.nn.sigmoid(c)
    o_ref[0] = jnp.dot(s.astype(jnp.bfloat16), w_ref[0].astype(jnp.bfloat16),
                       preferred_element_type=jnp.float32) + b_ref[0]


def modulation(cc, w_mod, b_mod):
    depth = w_mod.shape[0]
    nblk = 1536
    width = N_MOD * D_MODEL
    return pl.pallas_call(
        _mod_kernel,
        grid=(depth, width // nblk),
        in_specs=[pl.BlockSpec((MOD_ROWS, D_MODEL), lambda l, n: (0, 0)),
                  pl.BlockSpec((1, D_MODEL, nblk), lambda l, n: (l, 0, n)),
                  pl.BlockSpec((1, 1, nblk), lambda l, n: (l, 0, n))],
        out_specs=pl.BlockSpec((1, MOD_ROWS, nblk), lambda l, n: (l, 0, n)),
        out_shape=jax.ShapeDtypeStruct((depth, MOD_ROWS, width), jnp.float32),
        name="modulation",
    )(cc, w_mod, b_mod.reshape(depth, 1, width))


def _inproj_kernel(ctx_blocks, h_ref, mod_ref, g_ref, w_ref, cos_ref, sa_ref, sb_ref,
                   qg_ref, kg_ref, q_ref, k_ref, v_ref, u_ref, ga_ref, gb_ref):
    b, j = pl.program_id(0), pl.program_id(1)
    row = _mod_row(b, j, ctx_blocks)
    shift = mod_ref[pl.ds(row, 1), 0:D_MODEL]
    scale = mod_ref[pl.ds(row, 1), D_MODEL:2 * D_MODEL]
    x = h_ref[0]
    xn = x * lax.rsqrt(jnp.mean(x * x, axis=-1, keepdims=True) + EPS) * g_ref[...]
    xm = (xn * (1.0 + scale) + shift).astype(jnp.bfloat16)

    cos, sa, sb = cos_ref[...], sa_ref[...], sb_ref[...]

    def norm_rope(z, g, out_scale):
        zn = z * lax.rsqrt(jnp.mean(z * z, axis=-1, keepdims=True) + EPS) * g
        r = zn * cos + pltpu.roll(zn, 96, 1) * sa + pltpu.roll(zn, 32, 1) * sb
        return r * out_scale

    for hd in range(N_Q_HEADS):
        lo = hd * HEAD_DIM
        z = jnp.dot(xm, w_ref[:, lo:lo + HEAD_DIM], preferred_element_type=jnp.float32)
        q_ref[0, :, lo:lo + HEAD_DIM] = norm_rope(z, qg_ref[...], ATTN_SCALE * math.log2(math.e)).astype(q_ref.dtype)
    for hd in range(N_KV_HEADS):
        lo = hd * HEAD_DIM
        z = jnp.dot(xm, w_ref[:, ATTN_WIDTH + lo:ATTN_WIDTH + lo + HEAD_DIM],
                    preferred_element_type=jnp.float32)
        k_ref[0, :, lo:lo + HEAD_DIM] = norm_rope(z, kg_ref[...], 1.0).astype(k_ref.dtype)
    off = ATTN_WIDTH + KV_WIDTH
    v_ref[0] = jnp.dot(xm, w_ref[:, off:off + KV_WIDTH],
                       preferred_element_type=jnp.float32).astype(v_ref.dtype)
    off += KV_WIDTH
    u_ref[0] = jnp.dot(xm, w_ref[:, off:off + SSM_WIDTH], preferred_element_type=jnp.float32)
    off += SSM_WIDTH
    ga_ref[0] = jnp.dot(xm, w_ref[:, off:off + D_MODEL], preferred_element_type=jnp.float32)
    off += D_MODEL
    gb_ref[0] = jnp.dot(xm, w_ref[:, off:off + D_MODEL], preferred_element_type=jnp.float32)


def in_projection(h, mod, g, w_in_bf16, rope, qg, kg, n_ctx):
    bsz, n_tok, _ = h.shape
    nblk = n_tok // ROW_BLOCK
    cos, sa, sb = rope
    row_spec = lambda w: pl.BlockSpec((1, ROW_BLOCK, w), lambda b, j: (b, j, 0))
    full = lambda a: pl.BlockSpec(a.shape, lambda b, j: (0,) * a.ndim)
    rope_spec = pl.BlockSpec((ROW_BLOCK, HEAD_DIM), lambda b, j: (j, 0))
    f32, bf16 = jnp.float32, jnp.bfloat16
    out_shape = [jax.ShapeDtypeStruct((bsz, n_tok, ATTN_WIDTH), bf16),
                 jax.ShapeDtypeStruct((bsz, n_tok, KV_WIDTH), bf16),
                 jax.ShapeDtypeStruct((bsz, n_tok, KV_WIDTH), bf16),
                 jax.ShapeDtypeStruct((bsz, n_tok, SSM_WIDTH), f32),
                 jax.ShapeDtypeStruct((bsz, n_tok, D_MODEL), f32),
                 jax.ShapeDtypeStruct((bsz, n_tok, D_MODEL), f32)]
    return pl.pallas_call(
        functools.partial(_inproj_kernel, n_ctx // ROW_BLOCK),
        grid=(bsz, nblk),
        in_specs=[row_spec(D_MODEL), full(mod), full(g), full(w_in_bf16),
                  rope_spec, rope_spec, rope_spec, full(qg), full(kg)],
        out_specs=[row_spec(ATTN_WIDTH), row_spec(KV_WIDTH), row_spec(KV_WIDTH),
                   row_spec(SSM_WIDTH), row_spec(D_MODEL), row_spec(D_MODEL)],
        out_shape=out_shape,
        compiler_params=pltpu.CompilerParams(vmem_limit_bytes=VMEM_LIMIT),
        name="in_projection",
    )(h, mod, g, w_in_bf16, cos, sa, sb, qg, kg)


def rope_tables(n_ctx, n_lat):
    pos = jnp.arange(n_lat)
    inv_freq = ROPE_THETA ** (-jnp.arange(ROPE_FREQS, dtype=jnp.float32) / ROPE_FREQS)
    ang_row = (pos // GRID_W).astype(jnp.float32)[:, None] * inv_freq
    ang_col = (pos % GRID_W).astype(jnp.float32)[:, None] * inv_freq
    ang = jnp.concatenate([ang_row, ang_row, ang_col, ang_col], axis=-1)
    cos, sin = jnp.cos(ang), jnp.sin(ang)
    even = ((jnp.arange(HEAD_DIM) // ROPE_FREQS) % 2 == 0)[None, :]
    sa = jnp.where(even, -sin, 0.0)
    sb = jnp.where(even, 0.0, sin)
    pad = lambda t, v: jnp.concatenate([jnp.full((n_ctx, HEAD_DIM), v, jnp.float32), t], axis=0)
    return pad(cos, 1.0), pad(sa, 0.0), pad(sb, 0.0)


def _attn_kernel(n_ctx, n_lat, key_chunk, ctx_blocks, q_ref, k_ref, v_ref, o_ref, m_scr, l_scr, acc_scr):
    f32 = jnp.float32
    j = pl.program_id(2)
    q = jnp.concatenate([q_ref[0, :, hd * HEAD_DIM:(hd + 1) * HEAD_DIM]
                         for hd in range(GQA_GROUP)], axis=0)

    kc, vc = k_ref[0, 0:n_ctx, :], v_ref[0, 0:n_ctx, :]
    s = lax.dot_general(q, kc, _NT, preferred_element_type=f32)
    m = jnp.max(s, axis=-1, keepdims=True)
    p = jnp.exp2(s - m)
    m_scr[...] = m
    l_scr[...] = jnp.sum(p, axis=-1, keepdims=True)
    acc_scr[...] = jnp.dot(p.astype(vc.dtype), vc, preferred_element_type=f32)

    @pl.when(j >= ctx_blocks)
    def _():
        def body(c, carry):
            r0 = pl.multiple_of(n_ctx + c * key_chunk, ROW_BLOCK)
            kc = k_ref[0, pl.ds(r0, key_chunk), :]
            vc = v_ref[0, pl.ds(r0, key_chunk), :]
            s = lax.dot_general(q, kc, _NT, preferred_element_type=f32)
            m_old = m_scr[...]
            m_new = jnp.maximum(m_old, jnp.max(s, axis=-1, keepdims=True))
            alpha = jnp.exp2(m_old - m_new)
            p = jnp.exp2(s - m_new)
            m_scr[...] = m_new
            l_scr[...] = alpha * l_scr[...] + jnp.sum(p, axis=-1, keepdims=True)
            acc_scr[...] = alpha * acc_scr[...] + jnp.dot(p.astype(vc.dtype), vc,
                                                          preferred_element_type=f32)
            return carry

        lax.fori_loop(0, n_lat // key_chunk, body, 0)

    out = acc_scr[...] / l_scr[...]
    for hd in range(GQA_GROUP):
        o_ref[0, :, hd * HEAD_DIM:(hd + 1) * HEAD_DIM] = (
            out[hd * ROW_BLOCK:(hd + 1) * ROW_BLOCK]).astype(o_ref.dtype)


def attention(q, k, v, n_ctx):
    bsz, n_tok, _ = q.shape
    nblk = n_tok // ROW_BLOCK
    gw = GQA_GROUP * HEAD_DIM
    rows = GQA_GROUP * ROW_BLOCK
    key_chunk = math.gcd(n_tok - n_ctx, ATTN_KEYS)
    return pl.pallas_call(
        functools.partial(_attn_kernel, n_ctx, n_tok - n_ctx, key_chunk, n_ctx // ROW_BLOCK),
        grid=(bsz, N_KV_HEADS, nblk),
        in_specs=[pl.BlockSpec((1, ROW_BLOCK, gw), lambda b, g, j: (b, j, g)),
                  pl.BlockSpec((1, n_tok, HEAD_DIM), lambda b, g, j: (b, 0, g)),
                  pl.BlockSpec((1, n_tok, HEAD_DIM), lambda b, g, j: (b, 0, g))],
        out_specs=pl.BlockSpec((1, ROW_BLOCK, gw), lambda b, g, j: (b, j, g)),
        out_shape=jax.ShapeDtypeStruct((bsz, n_tok, ATTN_WIDTH), jnp.bfloat16),
        scratch_shapes=[pltpu.VMEM((rows, 1), jnp.float32), pltpu.VMEM((rows, 1), jnp.float32),
                        pltpu.VMEM((rows, HEAD_DIM), jnp.float32)],
        compiler_params=pltpu.CompilerParams(vmem_limit_bytes=VMEM_LIMIT),
        name="attention",
    )(q, k, v)


def _cpow(e, lam_re_dt, lam_im_dt):
    mag = jnp.exp(e * lam_re_dt)
    ang = e * lam_im_dt
    return mag * jnp.cos(ang), mag * jnp.sin(ang)


def _s5_kernel(n_batch, ctx_chunks, n_chunks,
               u_ref, arow_ref, acol_ref, b_ref, c_ref, ct_ref, y_ref,
               sfr, sfi, sbr, sbi, hfr, hfi, hbr, hbi):
    f32, bf16 = jnp.float32, jnp.bfloat16
    width = S5_CHUNK * SSM_GROUP
    lane = lax.broadcasted_iota(jnp.int32, (1, width), 1)
    lane_blk = lane // SSM_GROUP
    expand = (lax.broadcasted_iota(jnp.int32, (SSM_GROUP, width), 1) % SSM_GROUP
              == lax.broadcasted_iota(jnp.int32, (SSM_GROUP, width), 0)).astype(f32)
    rowi = lax.broadcasted_iota(jnp.int32, (width, 1), 0) // SSM_GROUP
    u = u_ref[0].astype(bf16)

    m_tot = jnp.zeros((width, width), f32)
    proj, read, step = [], [], []
    for d in range(2):
        fwd = d == 0
        are_c, aim_c, dt_c = acol_ref[d, 0, 0], acol_ref[d, 0, 1], jnp.exp(acol_ref[d, 0, 2])
        lr_c, li_c = are_c * dt_c, aim_c * dt_c
        abr, abi = _cpow(1.0, lr_c, li_c)
        den = are_c * are_c + aim_c * aim_c
        nr, ni = abr - 1.0, abi
        k_r = (nr * are_c + ni * aim_c) / den
        k_i = (ni * are_c - nr * aim_c) / den
        b_re, b_im = b_ref[d, 0, 0], b_ref[d, 0, 1]
        bb_r = k_r * b_re - k_i * b_im
        bb_i = k_r * b_im + k_i * b_re
        bt_r = jnp.dot(bb_r, expand, preferred_element_type=f32)
        bt_i = jnp.dot(bb_i, expand, preferred_element_type=f32)
        e_in = (S5_CHUNK - 1 - lane_blk if fwd else lane_blk).astype(f32)
        ap_r, ap_i = _cpow(e_in, lr_c, li_c)
        proj.append(((ap_r * bt_r - ap_i * bt_i).astype(bf16),
                     (ap_r * bt_i + ap_i * bt_r).astype(bf16)))
        ct_r = jnp.dot(ct_ref[d, 0, 0], expand, preferred_element_type=f32)
        ct_i = jnp.dot(ct_ref[d, 0, 1], expand, preferred_element_type=f32)
        e_out = (lane_blk + 1 if fwd else S5_CHUNK - lane_blk).astype(f32)
        aq_r, aq_i = _cpow(e_out, lr_c, li_c)
        read.append(((ct_r * aq_r - ct_i * aq_i).astype(bf16),
                     (-(ct_r * aq_i + ct_i * aq_r)).astype(bf16)))
        are_r, aim_r, dt_r = arow_ref[d, 0, 0], arow_ref[d, 0, 1], jnp.exp(arow_ref[d, 0, 2])
        lr_r, li_r = are_r * dt_r, aim_r * dt_r
        step.append(_cpow(float(S5_CHUNK), lr_r, li_r))
        lag = (rowi if fwd else S5_CHUNK - 1 - rowi).astype(f32)
        al_r, al_i = _cpow(lag, lr_r, li_r)
        c_re = jnp.concatenate([c_ref[d, 0, 0]] * S5_CHUNK, axis=0)
        c_im = jnp.concatenate([c_ref[d, 0, 1]] * S5_CHUNK, axis=0)
        ca_r = c_re * al_r - c_im * al_i
        ca_i = c_re * al_i + c_im * al_r
        kmat = (jnp.dot(ca_r, bb_r, preferred_element_type=f32)
                - jnp.dot(ca_i, bb_i, preferred_element_type=f32))
        kt = jnp.dot(kmat, expand, preferred_element_type=f32)
        for s in range(S5_CHUNK):
            sh = (s if fwd else S5_CHUNK - 1 - s) * SSM_GROUP
            if sh == 0:
                shifted = kt
            elif fwd:
                shifted = jnp.concatenate([jnp.zeros((sh, width), f32), kt[:width - sh]], axis=0)
            else:
                shifted = jnp.concatenate([kt[sh:], jnp.zeros((sh, width), f32)], axis=0)
            m_tot = m_tot + jnp.where(lane_blk == s, shifted, 0.0)

    y_ref[0] = lax.dot_general(u, m_tot.astype(bf16), _NT, preferred_element_type=f32)
    sfr[...] = lax.dot_general(u, proj[0][0], _NT, preferred_element_type=f32)
    sfi[...] = lax.dot_general(u, proj[0][1], _NT, preferred_element_type=f32)
    sbr[...] = lax.dot_general(u, proj[1][0], _NT, preferred_element_type=f32)
    sbi[...] = lax.dot_general(u, proj[1][1], _NT, preferred_element_type=f32)

    (afr, afi), (abr_, abi_) = step

    def scan_body(i, carry):
        fr, fi, br, bi = carry
        rf = pl.multiple_of(i * n_batch, n_batch)
        hfr[pl.ds(rf, n_batch), :] = fr
        hfi[pl.ds(rf, n_batch), :] = fi
        nfr = afr * fr - afi * fi + sfr[pl.ds(rf, n_batch), :]
        nfi = afr * fi + afi * fr + sfi[pl.ds(rf, n_batch), :]
        cb = jnp.where(i < ctx_chunks, ctx_chunks - 1 - i, n_chunks - 1 + ctx_chunks - i)
        rb = pl.multiple_of(cb * n_batch, n_batch)
        hbr[pl.ds(rb, n_batch), :] = br
        hbi[pl.ds(rb, n_batch), :] = bi
        nbr = abr_ * br - abi_ * bi + sbr[pl.ds(rb, n_batch), :]
        nbi = abr_ * bi + abi_ * br + sbi[pl.ds(rb, n_batch), :]
        return nfr, nfi, nbr, nbi

    z = jnp.zeros((n_batch, SSM_STATE), f32)
    lax.fori_loop(0, n_chunks, scan_body, (z, z, z, z))

    y_ref[0] += (jnp.dot(hfr[...].astype(bf16), read[0][0], preferred_element_type=f32)
                 + jnp.dot(hfi[...].astype(bf16), read[0][1], preferred_element_type=f32)
                 + jnp.dot(hbr[...].astype(bf16), read[1][0], preferred_element_type=f32)
                 + jnp.dot(hbi[...].astype(bf16), read[1][1], preferred_element_type=f32))


def s5_scan(u, a_re, a_im, log_dt, b_re, b_im, c_re, c_im, n_ctx):
    bsz, n_tok, _ = u.shape
    n_chunks = n_tok // S5_CHUNK
    rows = n_chunks * bsz
    width = S5_CHUNK * SSM_GROUP
    ug = u.reshape(bsz, n_chunks, S5_CHUNK, SSM_GROUPS, SSM_GROUP)
    ug = ug.transpose(3, 1, 0, 2, 4).reshape(SSM_GROUPS, rows, width)
    ldt = jnp.broadcast_to(log_dt[..., None], a_re.shape)
    arow = jnp.stack([a_re, a_im, ldt], axis=2)[:, :, :, None, :]
    acol = jnp.stack([a_re, a_im, ldt], axis=2)[..., None]
    bmat = jnp.stack([b_re, b_im], axis=2)
    cmat = jnp.stack([c_re, c_im], axis=2)
    ctmat = jnp.swapaxes(cmat, -1, -2)
    grp = lambda a: pl.BlockSpec((2, 1) + a.shape[2:], lambda g: (0, g) + (0,) * (a.ndim - 2))
    state = pltpu.VMEM((rows, SSM_STATE), jnp.float32)
    y = pl.pallas_call(
        functools.partial(_s5_kernel, bsz, n_ctx // S5_CHUNK, n_chunks),
        grid=(SSM_GROUPS,),
        in_specs=[pl.BlockSpec((1, rows, width), lambda g: (g, 0, 0)),
                  grp(arow), grp(acol), grp(bmat), grp(cmat), grp(ctmat)],
        out_specs=pl.BlockSpec((1, rows, width), lambda g: (g, 0, 0)),
        out_shape=jax.ShapeDtypeStruct((SSM_GROUPS, rows, width), jnp.float32),
        scratch_shapes=[state] * 8,
        compiler_params=pltpu.CompilerParams(vmem_limit_bytes=VMEM_LIMIT),
        name="s5_scan",
    )(ug, arow, acol, bmat, cmat, ctmat)
    y = y.reshape(SSM_GROUPS, n_chunks, bsz, S5_CHUNK, SSM_GROUP)
    return y.transpose(2, 1, 3, 0, 4).reshape(bsz, n_tok, SSM_WIDTH)


def _merge_kernel(ctx_blocks, h_ref, mod_ref, attn_ref, y_ref, u_ref, ga_ref, gb_ref,
                  dskip_ref, wglu_ref, bglu_ref, wba_ref, wbs_ref, wo_ref, o_ref):
    f32, bf16 = jnp.float32, jnp.bfloat16
    b, j = pl.program_id(0), pl.program_id(1)
    row = _mod_row(b, j, ctx_blocks)
    gate = mod_ref[pl.ds(row, 1), 2 * D_MODEL:3 * D_MODEL]
    y = jax.nn.gelu(y_ref[0] + dskip_ref[...] * u_ref[0])
    glu = jnp.dot(y.astype(bf16), wglu_ref[...], preferred_element_type=f32) + bglu_ref[...]
    ssm = y * jax.nn.sigmoid(glu)
    ba = jnp.dot(attn_ref[0], wba_ref[...], preferred_element_type=f32)
    bs = jnp.dot(ssm.astype(bf16), wbs_ref[...], preferred_element_type=f32)
    mix = jax.nn.sigmoid(ga_ref[0]) * ba + jax.nn.sigmoid(gb_ref[0]) * bs
    out = jnp.dot(mix.astype(bf16), wo_ref[...], preferred_element_type=f32)
    o_ref[0] = h_ref[0] + gate * out


def branch_merge(h, mod, attn, y, u, ga, gb, d_skip, w_glu, b_glu, w_ba, w_bs, w_o, n_ctx):
    bsz, n_tok, _ = h.shape
    nblk = n_tok // ROW_BLOCK
    row_spec = lambda w: pl.BlockSpec((1, ROW_BLOCK, w), lambda b, j: (b, j, 0))
    full = lambda a: pl.BlockSpec(a.shape, lambda b, j: (0,) * a.ndim)
    return pl.pallas_call(
        functools.partial(_merge_kernel, n_ctx // ROW_BLOCK),
        grid=(bsz, nblk),
        in_specs=[row_spec(D_MODEL), full(mod), row_spec(ATTN_WIDTH), row_spec(SSM_WIDTH),
                  row_spec(SSM_WIDTH), row_spec(D_MODEL), row_spec(D_MODEL),
                  full(d_skip), full(w_glu), full(b_glu), full(w_ba), full(w_bs), full(w_o)],
        out_specs=row_spec(D_MODEL),
        out_shape=jax.ShapeDtypeStruct(h.shape, jnp.float32),
        compiler_params=pltpu.CompilerParams(vmem_limit_bytes=VMEM_LIMIT),
        name="branch_merge",
    )(h, mod, attn, y, u, ga, gb, d_skip, w_glu, b_glu, w_ba, w_bs, w_o)


def _top16(jobs, vals_ref, pay_ref):
    rows = [lax.broadcasted_iota(jnp.int32, s.shape, 0) for s, _, _ in jobs]

    def body(r, carry):
        nxt = []
        for s, (s0, payload, base), row in zip(carry, jobs, rows):
            m = jnp.max(s, axis=0, keepdims=True)
            pos = jnp.min(jnp.where(s == m, row, s0.shape[0]), axis=0, keepdims=True)
            hit = row == pos
            vals_ref[pl.ds(base + r, 1), :] = m
            if payload is None:
                pay_ref[pl.ds(base + r, 1), :] = pos
            else:
                pay_ref[pl.ds(base + r, 1), :] = jnp.max(jnp.where(hit, payload, -1), axis=0,
                                                         keepdims=True)
            nxt.append(jnp.where(hit, -jnp.inf, s))
        return tuple(nxt)

    lax.fori_loop(0, PEER_TOPK, body, tuple(s for s, _, _ in jobs))


def _peer_query_kernel(ctx_blocks, h_ref, mod_ref, g_ref, wq_ref, keys_ref,
                       xw_ref, idx_ref, gate_ref, q_scr, sv, si, bv, be):
    f32, bf16 = jnp.float32, jnp.bfloat16
    b, j = pl.program_id(0), pl.program_id(1)
    row = _mod_row(b, j, ctx_blocks)
    shift = mod_ref[pl.ds(row, 1), 3 * D_MODEL:4 * D_MODEL]
    scale = mod_ref[pl.ds(row, 1), 4 * D_MODEL:5 * D_MODEL]
    x = h_ref[0]
    xn = x * lax.rsqrt(jnp.mean(x * x, axis=-1, keepdims=True) + EPS) * g_ref[...]
    xm = xn * (1.0 + scale) + shift
    xb = xm.astype(bf16)
    bits = lax.bitcast_convert_type(xb.astype(f32), jnp.int32)
    for r in range(EXPERT_ROWS):
        lo = (bits[:, (2 * r) * 128:(2 * r + 1) * 128] >> 16) & 0xFFFF
        hi = bits[:, (2 * r + 1) * 128:(2 * r + 2) * 128] & jnp.int32(-65536)
        xw_ref[r] = lo | hi
    for hp in range(2 * PEER_HEADS):
        lo = hp * PEER_HALF
        q_scr[hp] = jnp.dot(xb, wq_ref[:, lo:lo + PEER_HALF],
                            preferred_element_type=f32).astype(bf16)

    half = PEER_TOPK // 2

    def head_body(hd, carry):
        _top16([(lax.dot_general(keys_ref[p], q_scr[2 * hd + p], _NT,
                                 preferred_element_type=f32),
                 None, p * PEER_TOPK) for p in range(2)], sv, si)
        s1, s2 = sv[0:PEER_TOPK, :], sv[PEER_TOPK:2 * PEER_TOPK, :]
        i1, i2 = si[0:PEER_TOPK, :], si[PEER_TOPK:2 * PEER_TOPK, :]
        cand, cexp = [], []
        for i in range(half):
            size = PEER_TOPK if i == 0 else half
            blk = s1[i:i + 1, :] + s2[0:size, :]
            keep = PEER_TOPK // (i + 1)
            if keep < size:
                jrow = lax.broadcasted_iota(jnp.int32, (size, 1), 0)
                blk = jnp.where(jrow < keep, blk, -jnp.inf)
            cand.append(blk)
            cexp.append(i1[i:i + 1, :] * PEER_N_KEYS + i2[0:size, :])
        cand.append(s1[half:PEER_TOPK, :] + s2[0:1, :])
        cexp.append(i1[half:PEER_TOPK, :] * PEER_N_KEYS + i2[0:1, :])
        cand = jnp.concatenate(cand, axis=0)
        cexp = jnp.concatenate(cexp, axis=0)
        base = pl.multiple_of(hd * PEER_TOPK, PEER_TOPK)
        _top16([(cand, cexp, base)], bv, be)
        return carry

    lax.fori_loop(0, PEER_HEADS, head_body, 0)

    for hd in range(PEER_HEADS):
        best = bv[hd * PEER_TOPK:(hd + 1) * PEER_TOPK, :]
        e = jnp.exp(best - jnp.max(best, axis=0, keepdims=True))
        bv[hd * PEER_TOPK:(hd + 1) * PEER_TOPK, :] = e / jnp.sum(e, axis=0, keepdims=True)
    for t in range(ROW_BLOCK // PEER_TOK):
        idx_ref[t] = be[:, t * PEER_TOK:(t + 1) * PEER_TOK].T * EXPERT_ROWS
        gate_ref[t] = bv[:, t * PEER_TOK:(t + 1) * PEER_TOK]


def peer_query(h, mod, g, wq_bf16, keys_bf16, n_ctx):
    bsz, n_tok, _ = h.shape
    nblk = n_tok // ROW_BLOCK
    sub = ROW_BLOCK // PEER_TOK
    n_pblk = bsz * n_tok // PEER_TOK
    row_spec = pl.BlockSpec((1, ROW_BLOCK, D_MODEL), lambda b, j: (b, j, 0))
    full = lambda a: pl.BlockSpec(a.shape, lambda b, j: (0,) * a.ndim)
    pick_spec = pl.BlockSpec((sub, PEER_PICKS, PEER_TOK), lambda b, j: (b * nblk + j, 0, 0))
    f32, i32 = jnp.float32, jnp.int32
    return pl.pallas_call(
        functools.partial(_peer_query_kernel, n_ctx // ROW_BLOCK),
        grid=(bsz, nblk),
        in_specs=[row_spec, full(mod), full(g), full(wq_bf16), full(keys_bf16)],
        out_specs=[pl.BlockSpec((EXPERT_ROWS, ROW_BLOCK, 128), lambda b, j: (0, b * nblk + j, 0)),
                   pick_spec, pick_spec],
        out_shape=[jax.ShapeDtypeStruct((EXPERT_ROWS, bsz * n_tok, 128), i32),
                   jax.ShapeDtypeStruct((n_pblk, PEER_PICKS, PEER_TOK), i32),
                   jax.ShapeDtypeStruct((n_pblk, PEER_PICKS, PEER_TOK), f32)],
        scratch_shapes=[pltpu.VMEM((2 * PEER_HEADS, ROW_BLOCK, PEER_HALF), jnp.bfloat16),
                        pltpu.VMEM((2 * PEER_TOPK, ROW_BLOCK), f32),
                        pltpu.VMEM((2 * PEER_TOPK, ROW_BLOCK), i32),
                        pltpu.VMEM((PEER_PICKS, ROW_BLOCK), f32),
                        pltpu.VMEM((PEER_PICKS, ROW_BLOCK), i32)],
        compiler_params=pltpu.CompilerParams(vmem_limit_bytes=VMEM_LIMIT),
        name="peer_query",
    )(h, mod, g, wq_bf16, keys_bf16)


def pack_rows(rows):
    bits = lax.bitcast_convert_type(rows.astype(jnp.bfloat16), jnp.uint16).astype(jnp.uint32)
    bits = bits.reshape(-1, EXPERT_ROWS, 2, 128)
    words = bits[:, :, 0, :] | (bits[:, :, 1, :] << 16)
    return lax.bitcast_convert_type(words, jnp.int32)


def _expert_chunks(idx_ref, tab_ref, t):
    row = idx_ref.at[0, 0, pl.ds(t * PEER_PICKS, PEER_PICKS)]
    for c in range(PEER_PICKS // PEER_CHUNK):
        tiles = [tab_ref[pl.ds(pl.multiple_of(row[c * PEER_CHUNK + j], EXPERT_ROWS),
                               EXPERT_ROWS), :] for j in range(PEER_CHUNK)]
        yield c, jnp.concatenate(tiles, axis=0)


def _pick_rows_mask():
    tile_rows = 2 * EXPERT_ROWS
    shape = (PEER_PICKS, PEER_PICKS * tile_rows)
    return (lax.broadcasted_iota(jnp.int32, shape, 1) // tile_rows
            == lax.broadcasted_iota(jnp.int32, shape, 0))


def _peer_score_kernel(idx_ref, x_ref, gate_ref, tab_ref, w_ref):
    f32, bf16, i32 = jnp.float32, jnp.bfloat16, jnp.int32
    sel = _pick_rows_mask().astype(bf16)
    lane = lax.broadcasted_iota(i32, (PEER_PICKS, PEER_TOK), 1)
    cols = PEER_CHUNK * 2 * EXPERT_ROWS

    def group_body(g, acc):
        r0 = pl.multiple_of(g * PEER_STAGE, PEER_STAGE)
        xg = x_ref[:, pl.ds(r0, PEER_STAGE), :]
        for a in range(PEER_STAGE):
            t = r0 + a
            xw = jnp.concatenate([xg[r, a:a + 1, :] for r in range(EXPERT_ROWS)], axis=0)
            xt = pltpu.bitcast(jnp.concatenate([xw] * PEER_CHUNK, axis=0), bf16)
            part = jnp.zeros((PEER_PICKS, 128), f32)
            for c, words in _expert_chunks(idx_ref, tab_ref, t):
                prod = pltpu.bitcast(words, bf16) * xt
                part = part + jnp.dot(sel[:, c * cols:(c + 1) * cols], prod,
                                      preferred_element_type=f32)
            acc = jnp.where(lane == t, jnp.sum(part, axis=1, keepdims=True), acc)
        return acc

    scores = lax.fori_loop(0, PEER_TOK // PEER_STAGE, group_body,
                           jnp.zeros((PEER_PICKS, PEER_TOK), f32))
    w = gate_ref[0] * jax.nn.gelu(scores)
    w_ref[...] = jnp.dot(w.T.astype(bf16), sel, preferred_element_type=f32)


def _peer_mix_kernel(idx_ref, wexp_ref, h_ref, gate_ref, tab_ref, o_ref):
    f32, bf16, i32 = jnp.float32, jnp.bfloat16, jnp.int32
    tile_rows = 2 * EXPERT_ROWS
    shape = (tile_rows, PEER_PICKS * tile_rows)
    diag = (lax.broadcasted_iota(i32, shape, 1) % tile_rows
            == lax.broadcasted_iota(i32, shape, 0))
    cols = PEER_CHUNK * tile_rows

    def group_body(g, carry):
        r0 = pl.multiple_of(g * PEER_STAGE, PEER_STAGE)
        tiles = []
        for a in range(PEER_STAGE):
            t = r0 + a
            wmat = jnp.where(diag, wexp_ref[pl.ds(t, 1), :], 0.0).astype(bf16)
            out = jnp.zeros((tile_rows, 128), f32)
            for c, words in _expert_chunks(idx_ref, tab_ref, t):
                out = out + jnp.dot(wmat[:, c * cols:(c + 1) * cols],
                                    pltpu.bitcast(words, bf16), preferred_element_type=f32)
            tiles.append(out)
        rows = jnp.concatenate(
            [jnp.concatenate([tile[r:r + 1, :] for tile in tiles], axis=0)
             for r in range(tile_rows)], axis=1)
        o_ref[pl.ds(r0, PEER_STAGE), :] = (h_ref[pl.ds(r0, PEER_STAGE), :]
                                           + gate_ref[0, 0] * rows)
        return carry

    lax.fori_loop(0, PEER_TOK // PEER_STAGE, group_body, 0)


def _table_spec(tab):
    return pl.BlockSpec(tab.shape, lambda i: (0,) * tab.ndim, pipeline_mode=pl.Buffered(1))


def peer_experts(h, xw, idx, gates, mod, u_packed, v_packed, n_ctx):
    bsz, n_tok, _ = h.shape
    n_pblk = idx.shape[0]
    blk_per_batch = n_tok // PEER_TOK
    ctx_pblk = n_ctx // PEER_TOK
    flat = PEER_PICKS * PEER_TOK
    smem_spec = pl.BlockSpec((1, 1, flat), lambda i: (i, 0, 0), memory_space=pltpu.SMEM)
    pick_spec = pl.BlockSpec((1, PEER_PICKS, PEER_TOK), lambda i: (i, 0, 0))
    word_spec = pl.BlockSpec((EXPERT_ROWS, PEER_TOK, 128), lambda i: (0, i, 0))
    row_spec = pl.BlockSpec((PEER_TOK, D_MODEL), lambda i: (i, 0))
    idx_flat = idx.reshape(n_pblk, 1, flat)
    wexp = pl.pallas_call(
        _peer_score_kernel,
        grid=(n_pblk,),
        in_specs=[smem_spec, word_spec, pick_spec, _table_spec(u_packed)],
        out_specs=row_spec,
        out_shape=jax.ShapeDtypeStruct((bsz * n_tok, D_MODEL), jnp.float32),
        compiler_params=pltpu.CompilerParams(vmem_limit_bytes=VMEM_LIMIT),
        name="peer_scores",
    )(idx_flat, xw, gates, u_packed)

    def gate_map(i):
        bi, ji = i // blk_per_batch, i % blk_per_batch
        return (jnp.where(ji < ctx_pblk, 0, bi + 1), N_MOD - 1, 0, 0)

    out = pl.pallas_call(
        _peer_mix_kernel,
        grid=(n_pblk,),
        in_specs=[smem_spec, row_spec, row_spec,
                  pl.BlockSpec((1, 1, 1, D_MODEL), gate_map), _table_spec(v_packed)],
        out_specs=row_spec,
        out_shape=jax.ShapeDtypeStruct((bsz * n_tok, D_MODEL), jnp.float32),
        compiler_params=pltpu.CompilerParams(vmem_limit_bytes=VMEM_LIMIT),
        name="peer_mix",
    )(idx_flat, wexp, h.reshape(-1, D_MODEL), mod.reshape(MOD_ROWS, N_MOD, 1, D_MODEL), v_packed)
    return out.reshape(h.shape)


def kernel(x, c, ctx, c_ctx, w_mod, b_mod, norm_mix_g, w_in, q_norm_g, k_norm_g, ssm_a_re, ssm_a_im, ssm_log_dt, ssm_b_re, ssm_b_im, ssm_c_re, ssm_c_im, ssm_d, w_glu, b_glu, w_branch_attn, w_branch_ssm, w_out, norm_ffn_g, peer_w_q, peer_keys, peer_u, peer_v):
    bf16 = jnp.bfloat16
    bsz, n_lat, _ = x.shape
    n_ctx = ctx.shape[1]
    depth = w_mod.shape[0]
    assert bsz + 1 <= MOD_ROWS and n_ctx % ROW_BLOCK == 0 and n_lat % ROW_BLOCK == 0

    cc = jnp.zeros((MOD_ROWS, D_MODEL), jnp.float32).at[0].set(c_ctx).at[1:bsz + 1].set(c)
    mods = modulation(cc, w_mod, b_mod)
    rope = rope_tables(n_ctx, n_lat)
    h = jnp.concatenate([ctx, x], axis=1)
    row = lambda a: a.reshape(1, -1)

    for i in range(depth):
        mod = mods[i]
        q, k, v, u, ga, gb = in_projection(h, mod, row(norm_mix_g[i]), w_in[i].astype(bf16), rope,
                                           row(q_norm_g[i]), row(k_norm_g[i]), n_ctx)
        attn = attention(q, k, v, n_ctx)
        y = s5_scan(u, ssm_a_re[i], ssm_a_im[i], ssm_log_dt[i], ssm_b_re[i], ssm_b_im[i],
                    ssm_c_re[i], ssm_c_im[i], n_ctx)
        h = branch_merge(h, mod, attn, y, u, ga, gb, row(ssm_d[i]), w_glu[i].astype(bf16),
                         row(b_glu[i]), w_branch_attn[i].astype(bf16),
                         w_branch_ssm[i].astype(bf16), w_out[i].astype(bf16), n_ctx)
        xw, idx, gates = peer_query(h, mod, row(norm_ffn_g[i]), peer_w_q[i].astype(bf16),
                                    peer_keys[i].astype(bf16), n_ctx)
        h = peer_experts(h, xw, idx, gates, mod, pack_rows(peer_u[i]).reshape(-1, 128),
                         pack_rows(peer_v[i]).reshape(-1, 128), n_ctx)
    return h[:, n_ctx:, :]
```

```python
import functools
import math

import jax
import jax.numpy as jnp
from jax import lax
from jax.experimental import pallas as pl
from jax.experimental.pallas import tpu as pltpu

D_MODEL = 1024
GRID_W = 64
HEAD_DIM = 128
N_Q_HEADS = 8
N_KV_HEADS = 2
GQA_GROUP = N_Q_HEADS // N_KV_HEADS
ATTN_WIDTH = N_Q_HEADS * HEAD_DIM
KV_WIDTH = N_KV_HEADS * HEAD_DIM
ROPE_THETA = 10000.0
ROPE_FREQS = HEAD_DIM // 4
ATTN_SCALE = HEAD_DIM ** -0.5
SSM_WIDTH = D_MODEL // 2
SSM_GROUP = 16
SSM_GROUPS = SSM_WIDTH // SSM_GROUP
SSM_STATE = 64
IN_WIDTH = ATTN_WIDTH + 2 * KV_WIDTH + SSM_WIDTH + 2 * D_MODEL
PEER_HEADS = 8
PEER_N_KEYS = 128
PEER_N_EXPERTS = PEER_N_KEYS ** 2
PEER_QUERY_DIM = 256
PEER_HALF = PEER_QUERY_DIM // 2
PEER_TOPK = 16
PEER_PICKS = PEER_HEADS * PEER_TOPK
N_MOD = 6
EPS = 1e-6

ROW_BLOCK = 256
ATTN_KEYS = 4096
MOD_ROWS = 16
S5_CHUNK = 16
PEER_TOK = 128
EXPERT_ROWS = 4
PEER_STAGE = 16
PEER_CHUNK = 32
VMEM_LIMIT = 56 * 1024 * 1024

_NT = (((1,), (1,)), ((), ()))


def _mod_row(b, j, ctx_blocks):
    return jnp.where(j < ctx_blocks, 0, b + 1)


def _mod_kernel(c_ref, w_ref, b_ref, o_ref):
    c = c_ref[...]
    s = c * jax.nn.sigmoid(c)
    o_ref[0] = jnp.dot(s.astype(jnp.bfloat16), w_ref[0].astype(jnp.bfloat16),
                       preferred_element_type=jnp.float32) + b_ref[0]


def modulation(cc, w_mod, b_mod):
    depth = w_mod.shape[0]
    nblk = 1536
    width = N_MOD * D_MODEL
    return pl.pallas_call(
        _mod_kernel,
        grid=(depth, width // nblk),
        in_specs=[pl.BlockSpec((MOD_ROWS, D_MODEL), lambda l, n: (0, 0)),
                  pl.BlockSpec((1, D_MODEL, nblk), lambda l, n: (l, 0, n)),
                  pl.BlockSpec((1, 1, nblk), lambda l, n: (l, 0, n))],
        out_specs=pl.BlockSpec((1, MOD_ROWS, nblk), lambda l, n: (l, 0, n)),
        out_shape=jax.ShapeDtypeStruct((depth, MOD_ROWS, width), jnp.float32),
        name="modulation",
    )(cc, w_mod, b_mod.reshape(depth, 1, width))


def _inproj_kernel(ctx_blocks, h_ref, mod_ref, g_ref, w_ref, cos_ref, sa_ref, sb_ref,
                   qg_ref, kg_ref, q_ref, k_ref, v_ref, u_ref, ga_ref, gb_ref):
    b, j = pl.program_id(0), pl.program_id(1)
    row = _mod_row(b, j, ctx_blocks)
    shift = mod_ref[pl.ds(row, 1), 0:D_MODEL]
    scale = mod_ref[pl.ds(row, 1), D_MODEL:2 * D_MODEL]
    x = h_ref[0]
    xn = x * lax.rsqrt(jnp.mean(x * x, axis=-1, keepdims=True) + EPS) * g_ref[...]
    xm = (xn * (1.0 + scale) + shift).astype(jnp.bfloat16)

    cos, sa, sb = cos_ref[...], sa_ref[...], sb_ref[...]

    def norm_rope(z, g, out_scale):
        zn = z * lax.rsqrt(jnp.mean(z * z, axis=-1, keepdims=True) + EPS) * g
        r = zn * cos + pltpu.roll(zn, 96, 1) * sa + pltpu.roll(zn, 32, 1) * sb
        return r * out_scale

    for hd in range(N_Q_HEADS):
        lo = hd * HEAD_DIM
        z = jnp.dot(xm, w_ref[:, lo:lo + HEAD_DIM], preferred_element_type=jnp.float32)
        q_ref[0, :, lo:lo + HEAD_DIM] = norm_rope(z, qg_ref[...], ATTN_SCALE * math.log2(math.e)).astype(q_ref.dtype)
    for hd in range(N_KV_HEADS):
        lo = hd * HEAD_DIM
        z = jnp.dot(xm, w_ref[:, ATTN_WIDTH + lo:ATTN_WIDTH + lo + HEAD_DIM],
                    preferred_element_type=jnp.float32)
        k_ref[0, :, lo:lo + HEAD_DIM] = norm_rope(z, kg_ref[...], 1.0).astype(k_ref.dtype)
    off = ATTN_WIDTH + KV_WIDTH
    v_ref[0] = jnp.dot(xm, w_ref[:, off:off + KV_WIDTH],
                       preferred_element_type=jnp.float32).astype(v_ref.dtype)
    off += KV_WIDTH
    u_ref[0] = jnp.dot(xm, w_ref[:, off:off + SSM_WIDTH], preferred_element_type=jnp.float32)
    off += SSM_WIDTH
    ga_ref[0] = jnp.dot(xm, w_ref[:, off:off + D_MODEL], preferred_element_type=jnp.float32)
    off += D_MODEL
    gb_ref[0] = jnp.dot(xm, w_ref[:, off:off + D_MODEL], preferred_element_type=jnp.float32)


def in_projection(h, mod, g, w_in_bf16, rope, qg, kg, n_ctx):
    bsz, n_tok, _ = h.shape
    nblk = n_tok // ROW_BLOCK
    cos, sa, sb = rope
    row_spec = lambda w: pl.BlockSpec((1, ROW_BLOCK, w), lambda b, j: (b, j, 0))
    full = lambda a: pl.BlockSpec(a.shape, lambda b, j: (0,) * a.ndim)
    rope_spec = pl.BlockSpec((ROW_BLOCK, HEAD_DIM), lambda b, j: (j, 0))
    f32, bf16 = jnp.float32, jnp.bfloat16
    out_shape = [jax.ShapeDtypeStruct((bsz, n_tok, ATTN_WIDTH), bf16),
                 jax.ShapeDtypeStruct((bsz, n_tok, KV_WIDTH), bf16),
                 jax.ShapeDtypeStruct((bsz, n_tok, KV_WIDTH), bf16),
                 jax.ShapeDtypeStruct((bsz, n_tok, SSM_WIDTH), f32),
                 jax.ShapeDtypeStruct((bsz, n_tok, D_MODEL), f32),
                 jax.ShapeDtypeStruct((bsz, n_tok, D_MODEL), f32)]
    return pl.pallas_call(
        functools.partial(_inproj_kernel, n_ctx // ROW_BLOCK),
        grid=(bsz, nblk),
        in_specs=[row_spec(D_MODEL), full(mod), full(g), full(w_in_bf16),
                  rope_spec, rope_spec, rope_spec, full(qg), full(kg)],
        out_specs=[row_spec(ATTN_WIDTH), row_spec(KV_WIDTH), row_spec(KV_WIDTH),
                   row_spec(SSM_WIDTH), row_spec(D_MODEL), row_spec(D_MODEL)],
        out_shape=out_shape,
        compiler_params=pltpu.CompilerParams(vmem_limit_bytes=VMEM_LIMIT),
        name="in_projection",
    )(h, mod, g, w_in_bf16, cos, sa, sb, qg, kg)


def rope_tables(n_ctx, n_lat):
    pos = jnp.arange(n_lat)
    inv_freq = ROPE_THETA ** (-jnp.arange(ROPE_FREQS, dtype=jnp.float32) / ROPE_FREQS)
    ang_row = (pos // GRID_W).astype(jnp.float32)[:, None] * inv_freq
    ang_col = (pos % GRID_W).astype(jnp.float32)[:, None] * inv_freq
    ang = jnp.concatenate([ang_row, ang_row, ang_col, ang_col], axis=-1)
    cos, sin = jnp.cos(ang), jnp.sin(ang)
    even = ((jnp.arange(HEAD_DIM) // ROPE_FREQS) % 2 == 0)[None, :]
    sa = jnp.where(even, -sin, 0.0)
    sb = jnp.where(even, 0.0, sin)
    pad = lambda t, v: jnp.concatenate([jnp.full((n_ctx, HEAD_DIM), v, jnp.float32), t], axis=0)
    return pad(cos, 1.0), pad(sa, 0.0), pad(sb, 0.0)


def _attn_kernel(n_ctx, n_lat, key_chunk, ctx_blocks, q_ref, k_ref, v_ref, o_ref, m_scr, l_scr, acc_scr):
    f32 = jnp.float32
    j = pl.program_id(2)
    q = jnp.concatenate([q_ref[0, :, hd * HEAD_DIM:(hd + 1) * HEAD_DIM]
                         for hd in range(GQA_GROUP)], axis=0)

    kc, vc = k_ref[0, 0:n_ctx, :], v_ref[0, 0:n_ctx, :]
    s = lax.dot_general(q, kc, _NT, preferred_element_type=f32)
    m = jnp.max(s, axis=-1, keepdims=True)
    p = jnp.exp2(s - m)
    m_scr[...] = m
    l_scr[...] = jnp.sum(p, axis=-1, keepdims=True)
    acc_scr[...] = jnp.dot(p.astype(vc.dtype), vc, preferred_element_type=f32)

    @pl.when(j >= ctx_blocks)
    def _():
        def body(c, carry):
            r0 = pl.multiple_of(n_ctx + c * key_chunk, ROW_BLOCK)
            kc = k_ref[0, pl.ds(r0, key_chunk), :]
            vc = v_ref[0, pl.ds(r0, key_chunk), :]
            s = lax.dot_general(q, kc, _NT, preferred_element_type=f32)
            m_old = m_scr[...]
            m_new = jnp.maximum(m_old, jnp.max(s, axis=-1, keepdims=True))
            alpha = jnp.exp2(m_old - m_new)
            p = jnp.exp2(s - m_new)
            m_scr[...] = m_new
            l_scr[...] = alpha * l_scr[...] + jnp.sum(p, axis=-1, keepdims=True)
            acc_scr[...] = alpha * acc_scr[...] + jnp.dot(p.astype(vc.dtype), vc,
                                                          preferred_element_type=f32)
            return carry

        lax.fori_loop(0, n_lat // key_chunk, body, 0)

    out = acc_scr[...] / l_scr[...]
    for hd in range(GQA_GROUP):
        o_ref[0, :, hd * HEAD_DIM:(hd + 1) * HEAD_DIM] = (
            out[hd * ROW_BLOCK:(hd + 1) * ROW_BLOCK]).astype(o_ref.dtype)


def attention(q, k, v, n_ctx):
    bsz, n_tok, _ = q.shape
    nblk = n_tok // ROW_BLOCK
    gw = GQA_GROUP * HEAD_DIM
    rows = GQA_GROUP * ROW_BLOCK
    key_chunk = math.gcd(n_tok - n_ctx, ATTN_KEYS)
    return pl.pallas_call(
        functools.partial(_attn_kernel, n_ctx, n_tok - n_ctx, key_chunk, n_ctx // ROW_BLOCK),
        grid=(bsz, N_KV_HEADS, nblk),
        in_specs=[pl.BlockSpec((1, ROW_BLOCK, gw), lambda b, g, j: (b, j, g)),
                  pl.BlockSpec((1, n_tok, HEAD_DIM), lambda b, g, j: (b, 0, g)),
                  pl.BlockSpec((1, n_tok, HEAD_DIM), lambda b, g, j: (b, 0, g))],
        out_specs=pl.BlockSpec((1, ROW_BLOCK, gw), lambda b, g, j: (b, j, g)),
        out_shape=jax.ShapeDtypeStruct((bsz, n_tok, ATTN_WIDTH), jnp.bfloat16),
        scratch_shapes=[pltpu.VMEM((rows, 1), jnp.float32), pltpu.VMEM((rows, 1), jnp.float32),
                        pltpu.VMEM((rows, HEAD_DIM), jnp.float32)],
        compiler_params=pltpu.CompilerParams(vmem_limit_bytes=VMEM_LIMIT),
        name="attention",
    )(q, k, v)


def _cpow(e, lam_re_dt, lam_im_dt):
    mag = jnp.exp(e * lam_re_dt)
    ang = e * lam_im_dt
    return mag * jnp.cos(ang), mag * jnp.sin(ang)


def _s5_kernel(n_batch, ctx_chunks, n_chunks,
               u_ref, arow_ref, acol_ref, b_ref, c_ref, ct_ref, y_ref,
               sfr, sfi, sbr, sbi, hfr, hfi, hbr, hbi):
    f32, bf16 = jnp.float32, jnp.bfloat16
    width = S5_CHUNK * SSM_GROUP
    lane = lax.broadcasted_iota(jnp.int32, (1, width), 1)
    lane_blk = lane // SSM_GROUP
    expand = (lax.broadcasted_iota(jnp.int32, (SSM_GROUP, width), 1) % SSM_GROUP
              == lax.broadcasted_iota(jnp.int32, (SSM_GROUP, width), 0)).astype(f32)
    rowi = lax.broadcasted_iota(jnp.int32, (width, 1), 0) // SSM_GROUP
    u = u_ref[0].astype(bf16)

    m_tot = jnp.zeros((width, width), f32)
    proj, read, step = [], [], []
    for d in range(2):
        fwd = d == 0
        are_c, aim_c, dt_c = acol_ref[d, 0, 0], acol_ref[d, 0, 1], jnp.exp(acol_ref[d, 0, 2])
        lr_c, li_c = are_c * dt_c, aim_c * dt_c
        abr, abi = _cpow(1.0, lr_c, li_c)
        den = are_c * are_c + aim_c * aim_c
        nr, ni = abr - 1.0, abi
        k_r = (nr * are_c + ni * aim_c) / den
        k_i = (ni * are_c - nr * aim_c) / den
        b_re, b_im = b_ref[d, 0, 0], b_ref[d, 0, 1]
        bb_r = k_r * b_re - k_i * b_im
        bb_i = k_r * b_im + k_i * b_re
        bt_r = jnp.dot(bb_r, expand, preferred_element_type=f32)
        bt_i = jnp.dot(bb_i, expand, preferred_element_type=f32)
        e_in = (S5_CHUNK - 1 - lane_blk if fwd else lane_blk).astype(f32)
        ap_r, ap_i = _cpow(e_in, lr_c, li_c)
        proj.append(((ap_r * bt_r - ap_i * bt_i).astype(bf16),
                     (ap_r * bt_i + ap_i * bt_r).astype(bf16)))
        ct_r = jnp.dot(ct_ref[d, 0, 0], expand, preferred_element_type=f32)
        ct_i = jnp.dot(ct_ref[d, 0, 1], expand, preferred_element_type=f32)
        e_out = (lane_blk + 1 if fwd else S5_CHUNK - lane_blk).astype(f32)
        aq_r, aq_i = _cpow(e_out, lr_c, li_c)
        read.append(((ct_r * aq_r - ct_i * aq_i).astype(bf16),
                     (-(ct_r * aq_i + ct_i * aq_r)).astype(bf16)))
        are_r, aim_r, dt_r = arow_ref[d, 0, 0], arow_ref[d, 0, 1], jnp.exp(arow_ref[d, 0, 2])
        lr_r, li_r = are_r * dt_r, aim_r * dt_r
        step.append(_cpow(float(S5_CHUNK), lr_r, li_r))
        lag = (rowi if fwd else S5_CHUNK - 1 - rowi).astype(f32)
        al_r, al_i = _cpow(lag, lr_r, li_r)
        c_re = jnp.concatenate([c_ref[d, 0, 0]] * S5_CHUNK, axis=0)
        c_im = jnp.concatenate([c_ref[d, 0, 1]] * S5_CHUNK, axis=0)
        ca_r = c_re * al_r - c_im * al_i
        ca_i = c_re * al_i + c_im * al_r
        kmat = (jnp.dot(ca_r, bb_r, preferred_element_type=f32)
                - jnp.dot(ca_i, bb_i, preferred_element_type=f32))
        kt = jnp.dot(kmat, expand, preferred_element_type=f32)
        for s in range(S5_CHUNK):
            sh = (s if fwd else S5_CHUNK - 1 - s) * SSM_GROUP
            if sh == 0:
                shifted = kt
            elif fwd:
                shifted = jnp.concatenate([jnp.zeros((sh, width), f32), kt[:width - sh]], axis=0)
            else:
                shifted = jnp.concatenate([kt[sh:], jnp.zeros((sh, width), f32)], axis=0)
            m_tot = m_tot + jnp.where(lane_blk == s, shifted, 0.0)

    y_ref[0] = lax.dot_general(u, m_tot.astype(bf16), _NT, preferred_element_type=f32)
    sfr[...] = lax.dot_general(u, proj[0][0], _NT, preferred_element_type=f32)
    sfi[...] = lax.dot_general(u, proj[0][1], _NT, preferred_element_type=f32)
    sbr[...] = lax.dot_general(u, proj[1][0], _NT, preferred_element_type=f32)
    sbi[...] = lax.dot_general(u, proj[1][1], _NT, preferred_element_type=f32)

    (afr, afi), (abr_, abi_) = step

    def scan_body(i, carry):
        fr, fi, br, bi = carry
        rf = pl.multiple_of(i * n_batch, n_batch)
        hfr[pl.ds(rf, n_batch), :] = fr
        hfi[pl.ds(rf, n_batch), :] = fi
        nfr = afr * fr - afi * fi + sfr[pl.ds(rf, n_batch), :]
        nfi = afr * fi + afi * fr + sfi[pl.ds(rf, n_batch), :]
        cb = jnp.where(i < ctx_chunks, ctx_chunks - 1 - i, n_chunks - 1 + ctx_chunks - i)
        rb = pl.multiple_of(cb * n_batch, n_batch)
        hbr[pl.ds(rb, n_batch), :] = br
        hbi[pl.ds(rb, n_batch), :] = bi
        nbr = abr_ * br - abi_ * bi + sbr[pl.ds(rb, n_batch), :]
        nbi = abr_ * bi + abi_ * br + sbi[pl.ds(rb, n_batch), :]
        return nfr, nfi, nbr, nbi

    z = jnp.zeros((n_batch, SSM_STATE), f32)
    lax.fori_loop(0, n_chunks, scan_body, (z, z, z, z))

    y_ref[0] += (jnp.dot(hfr[...].astype(bf16), read[0][0], preferred_element_type=f32)
                 + jnp.dot(hfi[...].astype(bf16), read[0][1], preferred_element_type=f32)
                 + jnp.dot(hbr[...].astype(bf16), read[1][0], preferred_element_type=f32)
                 + jnp.dot(hbi[...].astype(bf16), read[1][1], preferred_element_type=f32))


def s5_scan(u, a_re, a_im, log_dt, b_re, b_im, c_re, c_im, n_ctx):
    bsz, n_tok, _ = u.shape
    n_chunks = n_tok // S5_CHUNK
    rows = n_chunks * bsz
    width = S5_CHUNK * SSM_GROUP
    ug = u.reshape(bsz, n_chunks, S5_CHUNK, SSM_GROUPS, SSM_GROUP)
    ug = ug.transpose(3, 1, 0, 2, 4).reshape(SSM_GROUPS, rows, width)
    ldt = jnp.broadcast_to(log_dt[..., None], a_re.shape)
    arow = jnp.stack([a_re, a_im, ldt], axis=2)[:, :, :, None, :]
    acol = jnp.stack([a_re, a_im, ldt], axis=2)[..., None]
    bmat = jnp.stack([b_re, b_im], axis=2)
    cmat = jnp.stack([c_re, c_im], axis=2)
    ctmat = jnp.swapaxes(cmat, -1, -2)
    grp = lambda a: pl.BlockSpec((2, 1) + a.shape[2:], lambda g: (0, g) + (0,) * (a.ndim - 2))
    state = pltpu.VMEM((rows, SSM_STATE), jnp.float32)
    y = pl.pallas_call(
        functools.partial(_s5_kernel, bsz, n_ctx // S5_CHUNK, n_chunks),
        grid=(SSM_GROUPS,),
        in_specs=[pl.BlockSpec((1, rows, width), lambda g: (g, 0, 0)),
                  grp(arow), grp(acol), grp(bmat), grp(cmat), grp(ctmat)],
        out_specs=pl.BlockSpec((1, rows, width), lambda g: (g, 0, 0)),
        out_shape=jax.ShapeDtypeStruct((SSM_GROUPS, rows, width), jnp.float32),
        scratch_shapes=[state] * 8,
        compiler_params=pltpu.CompilerParams(vmem_limit_bytes=VMEM_LIMIT),
        name="s5_scan",
    )(ug, arow, acol, bmat, cmat, ctmat)
    y = y.reshape(SSM_GROUPS, n_chunks, bsz, S5_CHUNK, SSM_GROUP)
    return y.transpose(2, 1, 3, 0, 4).reshape(bsz, n_tok, SSM_WIDTH)


def _merge_kernel(ctx_blocks, h_ref, mod_ref, attn_ref, y_ref, u_ref, ga_ref, gb_ref,
                  dskip_ref, wglu_ref, bglu_ref, wba_ref, wbs_ref, wo_ref, o_ref):
    f32, bf16 = jnp.float32, jnp.bfloat16
    b, j = pl.program_id(0), pl.program_id(1)
    row = _mod_row(b, j, ctx_blocks)
    gate = mod_ref[pl.ds(row, 1), 2 * D_MODEL:3 * D_MODEL]
    y = jax.nn.gelu(y_ref[0] + dskip_ref[...] * u_ref[0])
    glu = jnp.dot(y.astype(bf16), wglu_ref[...], preferred_element_type=f32) + bglu_ref[...]
    ssm = y * jax.nn.sigmoid(glu)
    ba = jnp.dot(attn_ref[0], wba_ref[...], preferred_element_type=f32)
    bs = jnp.dot(ssm.astype(bf16), wbs_ref[...], preferred_element_type=f32)
    mix = jax.nn.sigmoid(ga_ref[0]) * ba + jax.nn.sigmoid(gb_ref[0]) * bs
    out = jnp.dot(mix.astype(bf16), wo_ref[...], preferred_element_type=f32)
    o_ref[0] = h_ref[0] + gate * out


def branch_merge(h, mod, attn, y, u, ga, gb, d_skip, w_glu, b_glu, w_ba, w_bs, w_o, n_ctx):
    bsz, n_tok, _ = h.shape
    nblk = n_tok // ROW_BLOCK
    row_spec = lambda w: pl.BlockSpec((1, ROW_BLOCK, w), lambda b, j: (b, j, 0))
    full = lambda a: pl.BlockSpec(a.shape, lambda b, j: (0,) * a.ndim)
    return pl.pallas_call(
        functools.partial(_merge_kernel, n_ctx // ROW_BLOCK),
        grid=(bsz, nblk),
        in_specs=[row_spec(D_MODEL), full(mod), row_spec(ATTN_WIDTH), row_spec(SSM_WIDTH),
                  row_spec(SSM_WIDTH), row_spec(D_MODEL), row_spec(D_MODEL),
                  full(d_skip), full(w_glu), full(b_glu), full(w_ba), full(w_bs), full(w_o)],
        out_specs=row_spec(D_MODEL),
        out_shape=jax.ShapeDtypeStruct(h.shape, jnp.float32),
        compiler_params=pltpu.CompilerParams(vmem_limit_bytes=VMEM_LIMIT),
        name="branch_merge",
    )(h, mod, attn, y, u, ga, gb, d_skip, w_glu, b_glu, w_ba, w_bs, w_o)


def _top16(jobs, vals_ref, pay_ref):
    rows = [lax.broadcasted_iota(jnp.int32, s.shape, 0) for s, _, _ in jobs]

    def body(r, carry):
        nxt = []
        for s, (s0, payload, base), row in zip(carry, jobs, rows):
            m = jnp.max(s, axis=0, keepdims=True)
            pos = jnp.min(jnp.where(s == m, row, s0.shape[0]), axis=0, keepdims=True)
            hit = row == pos
            vals_ref[pl.ds(base + r, 1), :] = m
            if payload is None:
                pay_ref[pl.ds(base + r, 1), :] = pos
            else:
                pay_ref[pl.ds(base + r, 1), :] = jnp.max(jnp.where(hit, payload, -1), axis=0,
                                                         keepdims=True)
            nxt.append(jnp.where(hit, -jnp.inf, s))
        return tuple(nxt)

    lax.fori_loop(0, PEER_TOPK, body, tuple(s for s, _, _ in jobs))


def _peer_query_kernel(ctx_blocks, h_ref, mod_ref, g_ref, wq_ref, keys_ref,
                       xw_ref, idx_ref, gate_ref, q_scr, sv, si, bv, be):
    f32, bf16 = jnp.float32, jnp.bfloat16
    b, j = pl.program_id(0), pl.program_id(1)
    row = _mod_row(b, j, ctx_blocks)
    shift = mod_ref[pl.ds(row, 1), 3 * D_MODEL:4 * D_MODEL]
    scale = mod_ref[pl.ds(row, 1), 4 * D_MODEL:5 * D_MODEL]
    x = h_ref[0]
    xn = x * lax.rsqrt(jnp.mean(x * x, axis=-1, keepdims=True) + EPS) * g_ref[...]
    xm = xn * (1.0 + scale) + shift
    xb = xm.astype(bf16)
    bits = lax.bitcast_convert_type(xb.astype(f32), jnp.int32)
    for r in range(EXPERT_ROWS):
        lo = (bits[:, (2 * r) * 128:(2 * r + 1) * 128] >> 16) & 0xFFFF
        hi = bits[:, (2 * r + 1) * 128:(2 * r + 2) * 128] & jnp.int32(-65536)
        xw_ref[r] = lo | hi
    for hp in range(2 * PEER_HEADS):
        lo = hp * PEER_HALF
        q_scr[hp] = jnp.dot(xb, wq_ref[:, lo:lo + PEER_HALF],
                            preferred_element_type=f32).astype(bf16)

    half = PEER_TOPK // 2

    def head_body(hd, carry):
        _top16([(lax.dot_general(keys_ref[p], q_scr[2 * hd + p], _NT,
                                 preferred_element_type=f32),
                 None, p * PEER_TOPK) for p in range(2)], sv, si)
        s1, s2 = sv[0:PEER_TOPK, :], sv[PEER_TOPK:2 * PEER_TOPK, :]
        i1, i2 = si[0:PEER_TOPK, :], si[PEER_TOPK:2 * PEER_TOPK, :]
        cand, cexp = [], []
        for i in range(half):
            size = PEER_TOPK if i == 0 else half
            blk = s1[i:i + 1, :] + s2[0:size, :]
            keep = PEER_TOPK // (i + 1)
            if keep < size:
                jrow = lax.broadcasted_iota(jnp.int32, (size, 1), 0)
                blk = jnp.where(jrow < keep, blk, -jnp.inf)
            cand.append(blk)
            cexp.append(i1[i:i + 1, :] * PEER_N_KEYS + i2[0:size, :])
        cand.append(s1[half:PEER_TOPK, :] + s2[0:1, :])
        cexp.append(i1[half:PEER_TOPK, :] * PEER_N_KEYS + i2[0:1, :])
        cand = jnp.concatenate(cand, axis=0)
        cexp = jnp.concatenate(cexp, axis=0)
        base = pl.multiple_of(hd * PEER_TOPK, PEER_TOPK)
        _top16([(cand, cexp, base)], bv, be)
        return carry

    lax.fori_loop(0, PEER_HEADS, head_body, 0)

    for hd in range(PEER_HEADS):
        best = bv[hd * PEER_TOPK:(hd + 1) * PEER_TOPK, :]
        e = jnp.exp(best - jnp.max(best, axis=0, keepdims=True))
        bv[hd * PEER_TOPK:(hd + 1) * PEER_TOPK, :] = e / jnp.sum(e, axis=0, keepdims=True)
    for t in range(ROW_BLOCK // PEER_TOK):
        idx_ref[t] = be[:, t * PEER_TOK:(t + 1) * PEER_TOK].T * EXPERT_ROWS
        gate_ref[t] = bv[:, t * PEER_TOK:(t + 1) * PEER_TOK]


def peer_query(h, mod, g, wq_bf16, keys_bf16, n_ctx):
    bsz, n_tok, _ = h.shape
    nblk = n_tok // ROW_BLOCK
    sub = ROW_BLOCK // PEER_TOK
    n_pblk = bsz * n_tok // PEER_TOK
    row_spec = pl.BlockSpec((1, ROW_BLOCK, D_MODEL), lambda b, j: (b, j, 0))
    full = lambda a: pl.BlockSpec(a.shape, lambda b, j: (0,) * a.ndim)
    pick_spec = pl.BlockSpec((sub, PEER_PICKS, PEER_TOK), lambda b, j: (b * nblk + j, 0, 0))
    f32, i32 = jnp.float32, jnp.int32
    return pl.pallas_call(
        functools.partial(_peer_query_kernel, n_ctx // ROW_BLOCK),
        grid=(bsz, nblk),
        in_specs=[row_spec, full(mod), full(g), full(wq_bf16), full(keys_bf16)],
        out_specs=[pl.BlockSpec((EXPERT_ROWS, ROW_BLOCK, 128), lambda b, j: (0, b * nblk + j, 0)),
                   pick_spec, pick_spec],
        out_shape=[jax.ShapeDtypeStruct((EXPERT_ROWS, bsz * n_tok, 128), i32),
                   jax.ShapeDtypeStruct((n_pblk, PEER_PICKS, PEER_TOK), i32),
                   jax.ShapeDtypeStruct((n_pblk, PEER_PICKS, PEER_TOK), f32)],
        scratch_shapes=[pltpu.VMEM((2 * PEER_HEADS, ROW_BLOCK, PEER_HALF), jnp.bfloat16),
                        pltpu.VMEM((2 * PEER_TOPK, ROW_BLOCK), f32),
                        pltpu.VMEM((2 * PEER_TOPK, ROW_BLOCK), i32),
                        pltpu.VMEM((PEER_PICKS, ROW_BLOCK), f32),
                        pltpu.VMEM((PEER_PICKS, ROW_BLOCK), i32)],
        compiler_params=pltpu.CompilerParams(vmem_limit_bytes=VMEM_LIMIT),
        name="peer_query",
    )(h, mod, g, wq_bf16, keys_bf16)


def pack_rows(rows):
    bits = lax.bitcast_convert_type(rows.astype(jnp.bfloat16), jnp.uint16).astype(jnp.uint32)
    bits = bits.reshape(-1, EXPERT_ROWS, 2, 128)
    words = bits[:, :, 0, :] | (bits[:, :, 1, :] << 16)
    return lax.bitcast_convert_type(words, jnp.int32)


def _expert_chunks(idx_ref, tab_ref, t):
    row = idx_ref.at[0, 0, pl.ds(t * PEER_PICKS, PEER_PICKS)]
    for c in range(PEER_PICKS // PEER_CHUNK):
        tiles = [tab_ref[pl.ds(pl.multiple_of(row[c * PEER_CHUNK + j], EXPERT_ROWS),
                               EXPERT_ROWS), :] for j in range(PEER_CHUNK)]
        yield c, jnp.concatenate(tiles, axis=0)


def _pick_rows_mask():
    tile_rows = 2 * EXPERT_ROWS
    shape = (PEER_PICKS, PEER_PICKS * tile_rows)
    return (lax.broadcasted_iota(jnp.int32, shape, 1) // tile_rows
            == lax.broadcasted_iota(jnp.int32, shape, 0))


def _peer_score_kernel(idx_ref, x_ref, gate_ref, tab_ref, w_ref):
    f32, bf16, i32 = jnp.float32, jnp.bfloat16, jnp.int32
    sel = _pick_rows_mask().astype(bf16)
    lane = lax.broadcasted_iota(i32, (PEER_PICKS, PEER_TOK), 1)
    cols = PEER_CHUNK * 2 * EXPERT_ROWS

    def group_body(g, acc):
        r0 = pl.multiple_of(g * PEER_STAGE, PEER_STAGE)
        xg = x_ref[:, pl.ds(r0, PEER_STAGE), :]
        for a in range(PEER_STAGE):
            t = r0 + a
            xw = jnp.concatenate([xg[r, a:a + 1, :] for r in range(EXPERT_ROWS)], axis=0)
            xt = pltpu.bitcast(jnp.concatenate([xw] * PEER_CHUNK, axis=0), bf16)
            parts = []
            for c, words in _expert_chunks(idx_ref, tab_ref, t):
                prod = pltpu.bitcast(words, bf16) * xt
                parts.append(jnp.dot(sel[0:PEER_CHUNK, 0:cols], prod,
                                     preferred_element_type=f32))
            part = jnp.concatenate(parts, axis=0)
            acc = jnp.where(lane == t, jnp.sum(part, axis=1, keepdims=True), acc)
        return acc

    scores = lax.fori_loop(0, PEER_TOK // PEER_STAGE, group_body,
                           jnp.zeros((PEER_PICKS, PEER_TOK), f32))
    w = gate_ref[0] * jax.nn.gelu(scores)
    w_ref[...] = jnp.dot(w.T.astype(bf16), sel, preferred_element_type=f32)


def _peer_mix_kernel(idx_ref, wexp_ref, h_ref, gate_ref, tab_ref, o_ref):
    f32, bf16, i32 = jnp.float32, jnp.bfloat16, jnp.int32
    tile_rows = 2 * EXPERT_ROWS
    shape = (tile_rows, PEER_PICKS * tile_rows)
    diag = (lax.broadcasted_iota(i32, shape, 1) % tile_rows
            == lax.broadcasted_iota(i32, shape, 0))
    cols = PEER_CHUNK * tile_rows

    def group_body(g, carry):
        r0 = pl.multiple_of(g * PEER_STAGE, PEER_STAGE)
        tiles = []
        for a in range(PEER_STAGE):
            t = r0 + a
            wmat = jnp.where(diag, wexp_ref[pl.ds(t, 1), :], 0.0).astype(bf16)
            out = jnp.zeros((tile_rows, 128), f32)
            for c, words in _expert_chunks(idx_ref, tab_ref, t):
                out = out + jnp.dot(wmat[:, c * cols:(c + 1) * cols],
                                    pltpu.bitcast(words, bf16), preferred_element_type=f32)
            tiles.append(out)
        rows = jnp.concatenate(
            [jnp.concatenate([tile[r:r + 1, :] for tile in tiles], axis=0)
             for r in range(tile_rows)], axis=1)
        o_ref[pl.ds(r0, PEER_STAGE), :] = (h_ref[pl.ds(r0, PEER_STAGE), :]
                                           + gate_ref[0, 0] * rows)
        return carry

    lax.fori_loop(0, PEER_TOK // PEER_STAGE, group_body, 0)


def _table_spec(tab):
    return pl.BlockSpec(tab.shape, lambda i: (0,) * tab.ndim, pipeline_mode=pl.Buffered(1))


def peer_experts(h, xw, idx, gates, mod, u_packed, v_packed, n_ctx):
    bsz, n_tok, _ = h.shape
    n_pblk = idx.shape[0]
    blk_per_batch = n_tok // PEER_TOK
    ctx_pblk = n_ctx // PEER_TOK
    flat = PEER_PICKS * PEER_TOK
    smem_spec = pl.BlockSpec((1, 1, flat), lambda i: (i, 0, 0), memory_space=pltpu.SMEM)
    pick_spec = pl.BlockSpec((1, PEER_PICKS, PEER_TOK), lambda i: (i, 0, 0))
    word_spec = pl.BlockSpec((EXPERT_ROWS, PEER_TOK, 128), lambda i: (0, i, 0))
    row_spec = pl.BlockSpec((PEER_TOK, D_MODEL), lambda i: (i, 0))
    idx_flat = idx.reshape(n_pblk, 1, flat)
    wexp = pl.pallas_call(
        _peer_score_kernel,
        grid=(n_pblk,),
        in_specs=[smem_spec, word_spec, pick_spec, _table_spec(u_packed)],
        out_specs=row_spec,
        out_shape=jax.ShapeDtypeStruct((bsz * n_tok, D_MODEL), jnp.float32),
        compiler_params=pltpu.CompilerParams(vmem_limit_bytes=VMEM_LIMIT),
        name="peer_scores",
    )(idx_flat, xw, gates, u_packed)

    def gate_map(i):
        bi, ji = i // blk_per_batch, i % blk_per_batch
        return (jnp.where(ji < ctx_pblk, 0, bi + 1), N_MOD - 1, 0, 0)

    out = pl.pallas_call(
        _peer_mix_kernel,
        grid=(n_pblk,),
        in_specs=[smem_spec, row_spec, row_spec,
                  pl.BlockSpec((1, 1, 1, D_MODEL), gate_map), _table_spec(v_packed)],
        out_specs=row_spec,
        out_shape=jax.ShapeDtypeStruct((bsz * n_tok, D_MODEL), jnp.float32),
        compiler_params=pltpu.CompilerParams(vmem_limit_bytes=VMEM_LIMIT),
        name="peer_mix",
    )(idx_flat, wexp, h.reshape(-1, D_MODEL), mod.reshape(MOD_ROWS, N_MOD, 1, D_MODEL), v_packed)
    return out.reshape(h.shape)


def kernel(x, c, ctx, c_ctx, w_mod, b_mod, norm_mix_g, w_in, q_norm_g, k_norm_g, ssm_a_re, ssm_a_im, ssm_log_dt, ssm_b_re, ssm_b_im, ssm_c_re, ssm_c_im, ssm_d, w_glu, b_glu, w_branch_attn, w_branch_ssm, w_out, norm_ffn_g, peer_w_q, peer_keys, peer_u, peer_v):
    bf16 = jnp.bfloat16
    bsz, n_lat, _ = x.shape
    n_ctx = ctx.shape[1]
    depth = w_mod.shape[0]
    assert bsz + 1 <= MOD_ROWS and n_ctx % ROW_BLOCK == 0 and n_lat % ROW_BLOCK == 0

    cc = jnp.zeros((MOD_ROWS, D_MODEL), jnp.float32).at[0].set(c_ctx).at[1:bsz + 1].set(c)
    mods = modulation(cc, w_mod, b_mod)
    rope = rope_tables(n_ctx, n_lat)
    h = jnp.concatenate([ctx, x], axis=1)
    row = lambda a: a.reshape(1, -1)

    for i in range(depth):
        mod = mods[i]
        q, k, v, u, ga, gb = in_projection(h, mod, row(norm_mix_g[i]), w_in[i].astype(bf16), rope,
                                           row(q_norm_g[i]), row(k_norm_g[i]), n_ctx)
        attn = attention(q, k, v, n_ctx)
        y = s5_scan(u.astype(bf16), ssm_a_re[i], ssm_a_im[i], ssm_log_dt[i],
                    ssm_b_re[i], ssm_b_im[i], ssm_c_re[i], ssm_c_im[i], n_ctx)
        h = branch_merge(h, mod, attn, y, u, ga, gb, row(ssm_d[i]), w_glu[i].astype(bf16),
                         row(b_glu[i]), w_branch_attn[i].astype(bf16),
                         w_branch_ssm[i].astype(bf16), w_out[i].astype(bf16), n_ctx)
        xw, idx, gates = peer_query(h, mod, row(norm_ffn_g[i]), peer_w_q[i].astype(bf16),
                                    peer_keys[i].astype(bf16), n_ctx)
        h = peer_experts(h, xw, idx, gates, mod, pack_rows(peer_u[i]).reshape(-1, 128),
                         pack_rows(peer_v[i]).reshape(-1, 128), n_ctx)
    return h[:, n_ctx:, :]
```

```python
import functools
import math

import jax
import jax.numpy as jnp
from jax import lax
from jax.experimental import pallas as pl
from jax.experimental.pallas import tpu as pltpu

D_MODEL = 1024
GRID_W = 64
HEAD_DIM = 128
N_Q_HEADS = 8
N_KV_HEADS = 2
GQA_GROUP = N_Q_HEADS // N_KV_HEADS
ATTN_WIDTH = N_Q_HEADS * HEAD_DIM
KV_WIDTH = N_KV_HEADS * HEAD_DIM
ROPE_THETA = 10000.0
ROPE_FREQS = HEAD_DIM // 4
ATTN_SCALE = HEAD_DIM ** -0.5
SSM_WIDTH = D_MODEL // 2
SSM_GROUP = 16
SSM_GROUPS = SSM_WIDTH // SSM_GROUP
SSM_STATE = 64
IN_WIDTH = ATTN_WIDTH + 2 * KV_WIDTH + SSM_WIDTH + 2 * D_MODEL
PEER_HEADS = 8
PEER_N_KEYS = 128
PEER_N_EXPERTS = PEER_N_KEYS ** 2
PEER_QUERY_DIM = 256
PEER_HALF = PEER_QUERY_DIM // 2
PEER_TOPK = 16
PEER_PICKS = PEER_HEADS * PEER_TOPK
N_MOD = 6
EPS = 1e-6

ROW_BLOCK = 256
ATTN_KEYS = 4096
MOD_ROWS = 16
S5_CHUNK = 16
PEER_TOK = 128
EXPERT_ROWS = 4
PEER_STAGE = 16
PEER_CHUNK = 32
VMEM_LIMIT = 56 * 1024 * 1024

_NT = (((1,), (1,)), ((), ()))


def _mod_row(b, j, ctx_blocks):
    return jnp.where(j < ctx_blocks, 0, b + 1)


def _mod_kernel(c_ref, w_ref, b_ref, o_ref):
    c = c_ref[...]
    s = c * jax.nn.sigmoid(c)
    o_ref[0] = jnp.dot(s.astype(jnp.bfloat16), w_ref[0].astype(jnp.bfloat16),
                       preferred_element_type=jnp.float32) + b_ref[0]


def modulation(cc, w_mod, b_mod):
    depth = w_mod.shape[0]
    nblk = 1536
    width = N_MOD * D_MODEL
    return pl.pallas_call(
        _mod_kernel,
        grid=(depth, width // nblk),
        in_specs=[pl.BlockSpec((MOD_ROWS, D_MODEL), lambda l, n: (0, 0)),
                  pl.BlockSpec((1, D_MODEL, nblk), lambda l, n: (l, 0, n)),
                  pl.BlockSpec((1, 1, nblk), lambda l, n: (l, 0, n))],
        out_specs=pl.BlockSpec((1, MOD_ROWS, nblk), lambda l, n: (l, 0, n)),
        out_shape=jax.ShapeDtypeStruct((depth, MOD_ROWS, width), jnp.float32),
        name="modulation",
    )(cc, w_mod, b_mod.reshape(depth, 1, width))


def _inproj_kernel(ctx_blocks, h_ref, mod_ref, g_ref, w_ref, cos_ref, sa_ref, sb_ref,
                   qg_ref, kg_ref, q_ref, k_ref, v_ref, u_ref, ga_ref, gb_ref):
    b, j = pl.program_id(0), pl.program_id(1)
    row = _mod_row(b, j, ctx_blocks)
    shift = mod_ref[pl.ds(row, 1), 0:D_MODEL]
    scale = mod_ref[pl.ds(row, 1), D_MODEL:2 * D_MODEL]
    x = h_ref[0]
    xn = x * lax.rsqrt(jnp.mean(x * x, axis=-1, keepdims=True) + EPS) * g_ref[...]
    xm = (xn * (1.0 + scale) + shift).astype(jnp.bfloat16)

    cos, sa, sb = cos_ref[...], sa_ref[...], sb_ref[...]

    def norm_rope(z, g, out_scale):
        zn = z * lax.rsqrt(jnp.mean(z * z, axis=-1, keepdims=True) + EPS) * g
        r = zn * cos + pltpu.roll(zn, 96, 1) * sa + pltpu.roll(zn, 32, 1) * sb
        return r * out_scale

    for hd in range(N_Q_HEADS):
        lo = hd * HEAD_DIM
        z = jnp.dot(xm, w_ref[:, lo:lo + HEAD_DIM], preferred_element_type=jnp.float32)
        q_ref[0, :, lo:lo + HEAD_DIM] = norm_rope(z, qg_ref[...], ATTN_SCALE * math.log2(math.e)).astype(q_ref.dtype)
    for hd in range(N_KV_HEADS):
        lo = hd * HEAD_DIM
        z = jnp.dot(xm, w_ref[:, ATTN_WIDTH + lo:ATTN_WIDTH + lo + HEAD_DIM],
                    preferred_element_type=jnp.float32)
        k_ref[0, :, lo:lo + HEAD_DIM] = norm_rope(z, kg_ref[...], 1.0).astype(k_ref.dtype)
    off = ATTN_WIDTH + KV_WIDTH
    v_ref[0] = jnp.dot(xm, w_ref[:, off:off + KV_WIDTH],
                       preferred_element_type=jnp.float32).astype(v_ref.dtype)
    off += KV_WIDTH
    u_ref[0] = jnp.dot(xm, w_ref[:, off:off + SSM_WIDTH], preferred_element_type=jnp.float32)
    off += SSM_WIDTH
    ga_ref[0] = jnp.dot(xm, w_ref[:, off:off + D_MODEL], preferred_element_type=jnp.float32)
    off += D_MODEL
    gb_ref[0] = jnp.dot(xm, w_ref[:, off:off + D_MODEL], preferred_element_type=jnp.float32)


def in_projection(h, mod, g, w_in_bf16, rope, qg, kg, n_ctx):
    bsz, n_tok, _ = h.shape
    nblk = n_tok // ROW_BLOCK
    cos, sa, sb = rope
    row_spec = lambda w: pl.BlockSpec((1, ROW_BLOCK, w), lambda b, j: (b, j, 0))
    full = lambda a: pl.BlockSpec(a.shape, lambda b, j: (0,) * a.ndim)
    rope_spec = pl.BlockSpec((ROW_BLOCK, HEAD_DIM), lambda b, j: (j, 0))
    f32, bf16 = jnp.float32, jnp.bfloat16
    out_shape = [jax.ShapeDtypeStruct((bsz, n_tok, ATTN_WIDTH), bf16),
                 jax.ShapeDtypeStruct((bsz, n_tok, KV_WIDTH), bf16),
                 jax.ShapeDtypeStruct((bsz, n_tok, KV_WIDTH), bf16),
                 jax.ShapeDtypeStruct((bsz, n_tok, SSM_WIDTH), f32),
                 jax.ShapeDtypeStruct((bsz, n_tok, D_MODEL), f32),
                 jax.ShapeDtypeStruct((bsz, n_tok, D_MODEL), f32)]
    return pl.pallas_call(
        functools.partial(_inproj_kernel, n_ctx // ROW_BLOCK),
        grid=(bsz, nblk),
        in_specs=[row_spec(D_MODEL), full(mod), full(g), full(w_in_bf16),
                  rope_spec, rope_spec, rope_spec, full(qg), full(kg)],
        out_specs=[row_spec(ATTN_WIDTH), row_spec(KV_WIDTH), row_spec(KV_WIDTH),
                   row_spec(SSM_WIDTH), row_spec(D_MODEL), row_spec(D_MODEL)],
        out_shape=out_shape,
        compiler_params=pltpu.CompilerParams(vmem_limit_bytes=VMEM_LIMIT),
        name="in_projection",
    )(h, mod, g, w_in_bf16, cos, sa, sb, qg, kg)


def rope_tables(n_ctx, n_lat):
    pos = jnp.arange(n_lat)
    inv_freq = ROPE_THETA ** (-jnp.arange(ROPE_FREQS, dtype=jnp.float32) / ROPE_FREQS)
    ang_row = (pos // GRID_W).astype(jnp.float32)[:, None] * inv_freq
    ang_col = (pos % GRID_W).astype(jnp.float32)[:, None] * inv_freq
    ang = jnp.concatenate([ang_row, ang_row, ang_col, ang_col], axis=-1)
    cos, sin = jnp.cos(ang), jnp.sin(ang)
    even = ((jnp.arange(HEAD_DIM) // ROPE_FREQS) % 2 == 0)[None, :]
    sa = jnp.where(even, -sin, 0.0)
    sb = jnp.where(even, 0.0, sin)
    pad = lambda t, v: jnp.concatenate([jnp.full((n_ctx, HEAD_DIM), v, jnp.float32), t], axis=0)
    return pad(cos, 1.0), pad(sa, 0.0), pad(sb, 0.0)


def _attn_kernel(n_ctx, n_lat, key_chunk, ctx_blocks, q_ref, k_ref, v_ref, o_ref, m_scr, l_scr, acc_scr):
    f32 = jnp.float32
    j = pl.program_id(2)
    q = jnp.concatenate([q_ref[0, :, hd * HEAD_DIM:(hd + 1) * HEAD_DIM]
                         for hd in range(GQA_GROUP)], axis=0)

    kc, vc = k_ref[0, 0:n_ctx, :], v_ref[0, 0:n_ctx, :]
    s = lax.dot_general(q, kc, _NT, preferred_element_type=f32)
    m = jnp.max(s, axis=-1, keepdims=True)
    p = jnp.exp2(s - m)
    m_scr[...] = m
    l_scr[...] = jnp.sum(p, axis=-1, keepdims=True)
    acc_scr[...] = jnp.dot(p.astype(vc.dtype), vc, preferred_element_type=f32)

    @pl.when(j >= ctx_blocks)
    def _():
        def body(c, carry):
            r0 = pl.multiple_of(n_ctx + c * key_chunk, ROW_BLOCK)
            kc = k_ref[0, pl.ds(r0, key_chunk), :]
            vc = v_ref[0, pl.ds(r0, key_chunk), :]
            s = lax.dot_general(q, kc, _NT, preferred_element_type=f32)
            m_old = m_scr[...]
            m_new = jnp.maximum(m_old, jnp.max(s, axis=-1, keepdims=True))
            alpha = jnp.exp2(m_old - m_new)
            p = jnp.exp2(s - m_new)
            m_scr[...] = m_new
            l_scr[...] = alpha * l_scr[...] + jnp.sum(p, axis=-1, keepdims=True)
            acc_scr[...] = alpha * acc_scr[...] + jnp.dot(p.astype(vc.dtype), vc,
                                                          preferred_element_type=f32)
            return carry

        lax.fori_loop(0, n_lat // key_chunk, body, 0)

    out = acc_scr[...] / l_scr[...]
    for hd in range(GQA_GROUP):
        o_ref[0, :, hd * HEAD_DIM:(hd + 1) * HEAD_DIM] = (
            out[hd * ROW_BLOCK:(hd + 1) * ROW_BLOCK]).astype(o_ref.dtype)


def attention(q, k, v, n_ctx):
    bsz, n_tok, _ = q.shape
    nblk = n_tok // ROW_BLOCK
    gw = GQA_GROUP * HEAD_DIM
    rows = GQA_GROUP * ROW_BLOCK
    key_chunk = math.gcd(n_tok - n_ctx, ATTN_KEYS)
    return pl.pallas_call(
        functools.partial(_attn_kernel, n_ctx, n_tok - n_ctx, key_chunk, n_ctx // ROW_BLOCK),
        grid=(bsz, N_KV_HEADS, nblk),
        in_specs=[pl.BlockSpec((1, ROW_BLOCK, gw), lambda b, g, j: (b, j, g)),
                  pl.BlockSpec((1, n_tok, HEAD_DIM), lambda b, g, j: (b, 0, g)),
                  pl.BlockSpec((1, n_tok, HEAD_DIM), lambda b, g, j: (b, 0, g))],
        out_specs=pl.BlockSpec((1, ROW_BLOCK, gw), lambda b, g, j: (b, j, g)),
        out_shape=jax.ShapeDtypeStruct((bsz, n_tok, ATTN_WIDTH), jnp.bfloat16),
        scratch_shapes=[pltpu.VMEM((rows, 1), jnp.float32), pltpu.VMEM((rows, 1), jnp.float32),
                        pltpu.VMEM((rows, HEAD_DIM), jnp.float32)],
        compiler_params=pltpu.CompilerParams(vmem_limit_bytes=VMEM_LIMIT),
        name="attention",
    )(q, k, v)


def _cpow(e, lam_re_dt, lam_im_dt):
    mag = jnp.exp(e * lam_re_dt)
    ang = e * lam_im_dt
    return mag * jnp.cos(ang), mag * jnp.sin(ang)


def _s5_kernel(n_batch, ctx_chunks, n_chunks,
               u_ref, arow_ref, acol_ref, b_ref, c_ref, ct_ref, y_ref,
               sfr, sfi, sbr, sbi, hfr, hfi, hbr, hbi):
    f32, bf16 = jnp.float32, jnp.bfloat16
    width = S5_CHUNK * SSM_GROUP
    lane = lax.broadcasted_iota(jnp.int32, (1, width), 1)
    lane_blk = lane // SSM_GROUP
    expand = (lax.broadcasted_iota(jnp.int32, (SSM_GROUP, width), 1) % SSM_GROUP
              == lax.broadcasted_iota(jnp.int32, (SSM_GROUP, width), 0)).astype(f32)
    rowi = lax.broadcasted_iota(jnp.int32, (width, 1), 0) // SSM_GROUP
    u = u_ref[0].astype(bf16)

    m_tot = jnp.zeros((width, width), f32)
    proj, read, step = [], [], []
    for d in range(2):
        fwd = d == 0
        are_c, aim_c, dt_c = acol_ref[d, 0, 0], acol_ref[d, 0, 1], jnp.exp(acol_ref[d, 0, 2])
        lr_c, li_c = are_c * dt_c, aim_c * dt_c
        abr, abi = _cpow(1.0, lr_c, li_c)
        den = are_c * are_c + aim_c * aim_c
        nr, ni = abr - 1.0, abi
        k_r = (nr * are_c + ni * aim_c) / den
        k_i = (ni * are_c - nr * aim_c) / den
        b_re, b_im = b_ref[d, 0, 0], b_ref[d, 0, 1]
        bb_r = k_r * b_re - k_i * b_im
        bb_i = k_r * b_im + k_i * b_re
        bt_r = jnp.dot(bb_r, expand, preferred_element_type=f32)
        bt_i = jnp.dot(bb_i, expand, preferred_element_type=f32)
        e_in = (S5_CHUNK - 1 - lane_blk if fwd else lane_blk).astype(f32)
        ap_r, ap_i = _cpow(e_in, lr_c, li_c)
        proj.append(((ap_r * bt_r - ap_i * bt_i).astype(bf16),
                     (ap_r * bt_i + ap_i * bt_r).astype(bf16)))
        ct_r = jnp.dot(ct_ref[d, 0, 0], expand, preferred_element_type=f32)
        ct_i = jnp.dot(ct_ref[d, 0, 1], expand, preferred_element_type=f32)
        e_out = (lane_blk + 1 if fwd else S5_CHUNK - lane_blk).astype(f32)
        aq_r, aq_i = _cpow(e_out, lr_c, li_c)
        read.append(((ct_r * aq_r - ct_i * aq_i).astype(bf16),
                     (-(ct_r * aq_i + ct_i * aq_r)).astype(bf16)))
        are_r, aim_r, dt_r = arow_ref[d, 0, 0], arow_ref[d, 0, 1], jnp.exp(arow_ref[d, 0, 2])
        lr_r, li_r = are_r * dt_r, aim_r * dt_r
        step.append(_cpow(float(S5_CHUNK), lr_r, li_r))
        lag = (rowi if fwd else S5_CHUNK - 1 - rowi).astype(f32)
        al_r, al_i = _cpow(lag, lr_r, li_r)
        c_re = jnp.concatenate([c_ref[d, 0, 0]] * S5_CHUNK, axis=0)
        c_im = jnp.concatenate([c_ref[d, 0, 1]] * S5_CHUNK, axis=0)
        ca_r = c_re * al_r - c_im * al_i
        ca_i = c_re * al_i + c_im * al_r
        kmat = (jnp.dot(ca_r, bb_r, preferred_element_type=f32)
                - jnp.dot(ca_i, bb_i, preferred_element_type=f32))
        kt = jnp.dot(kmat, expand, preferred_element_type=f32)
        for s in range(S5_CHUNK):
            sh = (s if fwd else S5_CHUNK - 1 - s) * SSM_GROUP
            if sh == 0:
                shifted = kt
            elif fwd:
                shifted = jnp.concatenate([jnp.zeros((sh, width), f32), kt[:width - sh]], axis=0)
            else:
                shifted = jnp.concatenate([kt[sh:], jnp.zeros((sh, width), f32)], axis=0)
            m_tot = m_tot + jnp.where(lane_blk == s, shifted, 0.0)

    y_ref[0] = lax.dot_general(u, m_tot.astype(bf16), _NT, preferred_element_type=f32)
    sfr[...] = lax.dot_general(u, proj[0][0], _NT, preferred_element_type=f32)
    sfi[...] = lax.dot_general(u, proj[0][1], _NT, preferred_element_type=f32)
    sbr[...] = lax.dot_general(u, proj[1][0], _NT, preferred_element_type=f32)
    sbi[...] = lax.dot_general(u, proj[1][1], _NT, preferred_element_type=f32)

    (afr, afi), (abr_, abi_) = step

    def scan_body(i, carry):
        fr, fi, br, bi = carry
        rf = pl.multiple_of(i * n_batch, n_batch)
        hfr[pl.ds(rf, n_batch), :] = fr
        hfi[pl.ds(rf, n_batch), :] = fi
        nfr = afr * fr - afi * fi + sfr[pl.ds(rf, n_batch), :]
        nfi = afr * fi + afi * fr + sfi[pl.ds(rf, n_batch), :]
        cb = jnp.where(i < ctx_chunks, ctx_chunks - 1 - i, n_chunks - 1 + ctx_chunks - i)
        rb = pl.multiple_of(cb * n_batch, n_batch)
        hbr[pl.ds(rb, n_batch), :] = br
        hbi[pl.ds(rb, n_batch), :] = bi
        nbr = abr_ * br - abi_ * bi + sbr[pl.ds(rb, n_batch), :]
        nbi = abr_ * bi + abi_ * br + sbi[pl.ds(rb, n_batch), :]
        return nfr, nfi, nbr, nbi

    z = jnp.zeros((n_batch, SSM_STATE), f32)
    lax.fori_loop(0, n_chunks, scan_body, (z, z, z, z))

    y_ref[0] += (jnp.dot(hfr[...].astype(bf16), read[0][0], preferred_element_type=f32)
                 + jnp.dot(hfi[...].astype(bf16), read[0][1], preferred_element_type=f32)
                 + jnp.dot(hbr[...].astype(bf16), read[1][0], preferred_element_type=f32)
                 + jnp.dot(hbi[...].astype(bf16), read[1][1], preferred_element_type=f32))


def s5_scan(u, a_re, a_im, log_dt, b_re, b_im, c_re, c_im, n_ctx):
    bsz, n_tok, _ = u.shape
    n_chunks = n_tok // S5_CHUNK
    rows = n_chunks * bsz
    width = S5_CHUNK * SSM_GROUP
    ug = u.reshape(bsz, n_chunks, S5_CHUNK, SSM_GROUPS, SSM_GROUP)
    ug = ug.transpose(3, 1, 0, 2, 4).reshape(SSM_GROUPS, rows, width)
    ldt = jnp.broadcast_to(log_dt[..., None], a_re.shape)
    arow = jnp.stack([a_re, a_im, ldt], axis=2)[:, :, :, None, :]
    acol = jnp.stack([a_re, a_im, ldt], axis=2)[..., None]
    bmat = jnp.stack([b_re, b_im], axis=2)
    cmat = jnp.stack([c_re, c_im], axis=2)
    ctmat = jnp.swapaxes(cmat, -1, -2)
    grp = lambda a: pl.BlockSpec((2, 1) + a.shape[2:], lambda g: (0, g) + (0,) * (a.ndim - 2))
    state = pltpu.VMEM((rows, SSM_STATE), jnp.float32)
    y = pl.pallas_call(
        functools.partial(_s5_kernel, bsz, n_ctx // S5_CHUNK, n_chunks),
        grid=(SSM_GROUPS,),
        in_specs=[pl.BlockSpec((1, rows, width), lambda g: (g, 0, 0)),
                  grp(arow), grp(acol), grp(bmat), grp(cmat), grp(ctmat)],
        out_specs=pl.BlockSpec((1, rows, width), lambda g: (g, 0, 0)),
        out_shape=jax.ShapeDtypeStruct((SSM_GROUPS, rows, width), jnp.float32),
        scratch_shapes=[state] * 8,
        compiler_params=pltpu.CompilerParams(vmem_limit_bytes=VMEM_LIMIT),
        name="s5_scan",
    )(ug, arow, acol, bmat, cmat, ctmat)
    y = y.reshape(SSM_GROUPS, n_chunks, bsz, S5_CHUNK, SSM_GROUP)
    return y.transpose(2, 1, 3, 0, 4).reshape(bsz, n_tok, SSM_WIDTH)


def _merge_kernel(ctx_blocks, h_ref, mod_ref, attn_ref, y_ref, u_ref, ga_ref, gb_ref,
                  dskip_ref, wglu_ref, bglu_ref, wba_ref, wbs_ref, wo_ref, o_ref):
    f32, bf16 = jnp.float32, jnp.bfloat16
    b, j = pl.program_id(0), pl.program_id(1)
    row = _mod_row(b, j, ctx_blocks)
    gate = mod_ref[pl.ds(row, 1), 2 * D_MODEL:3 * D_MODEL]
    y = jax.nn.gelu(y_ref[0] + dskip_ref[...] * u_ref[0])
    glu = jnp.dot(y.astype(bf16), wglu_ref[...], preferred_element_type=f32) + bglu_ref[...]
    ssm = y * jax.nn.sigmoid(glu)
    ba = jnp.dot(attn_ref[0], wba_ref[...], preferred_element_type=f32)
    bs = jnp.dot(ssm.astype(bf16), wbs_ref[...], preferred_element_type=f32)
    mix = jax.nn.sigmoid(ga_ref[0]) * ba + jax.nn.sigmoid(gb_ref[0]) * bs
    out = jnp.dot(mix.astype(bf16), wo_ref[...], preferred_element_type=f32)
    o_ref[0] = h_ref[0] + gate * out


def branch_merge(h, mod, attn, y, u, ga, gb, d_skip, w_glu, b_glu, w_ba, w_bs, w_o, n_ctx):
    bsz, n_tok, _ = h.shape
    nblk = n_tok // ROW_BLOCK
    row_spec = lambda w: pl.BlockSpec((1, ROW_BLOCK, w), lambda b, j: (b, j, 0))
    full = lambda a: pl.BlockSpec(a.shape, lambda b, j: (0,) * a.ndim)
    return pl.pallas_call(
        functools.partial(_merge_kernel, n_ctx // ROW_BLOCK),
        grid=(bsz, nblk),
        in_specs=[row_spec(D_MODEL), full(mod), row_spec(ATTN_WIDTH), row_spec(SSM_WIDTH),
                  row_spec(SSM_WIDTH), row_spec(D_MODEL), row_spec(D_MODEL),
                  full(d_skip), full(w_glu), full(b_glu), full(w_ba), full(w_bs), full(w_o)],
        out_specs=row_spec(D_MODEL),
        out_shape=jax.ShapeDtypeStruct(h.shape, jnp.float32),
        compiler_params=pltpu.CompilerParams(vmem_limit_bytes=VMEM_LIMIT),
        name="branch_merge",
    )(h, mod, attn, y, u, ga, gb, d_skip, w_glu, b_glu, w_ba, w_bs, w_o)


def _top16(jobs, vals_ref, pay_ref):
    rows = [lax.broadcasted_iota(jnp.int32, s.shape, 0) for s, _, _ in jobs]

    def body(r, carry):
        nxt = []
        for s, (s0, payload, base), row in zip(carry, jobs, rows):
            m = jnp.max(s, axis=0, keepdims=True)
            pos = jnp.min(jnp.where(s == m, row, s0.shape[0]), axis=0, keepdims=True)
            hit = row == pos
            vals_ref[pl.ds(base + r, 1), :] = m
            if payload is None:
                pay_ref[pl.ds(base + r, 1), :] = pos
            else:
                pay_ref[pl.ds(base + r, 1), :] = jnp.max(jnp.where(hit, payload, -1), axis=0,
                                                         keepdims=True)
            nxt.append(jnp.where(hit, -jnp.inf, s))
        return tuple(nxt)

    lax.fori_loop(0, PEER_TOPK, body, tuple(s for s, _, _ in jobs))


def _peer_query_kernel(ctx_blocks, h_ref, mod_ref, g_ref, wq_ref, keys_ref,
                       xw_ref, idx_ref, gate_ref, q_scr, sv, si, bv, be):
    f32, bf16 = jnp.float32, jnp.bfloat16
    b, j = pl.program_id(0), pl.program_id(1)
    row = _mod_row(b, j, ctx_blocks)
    shift = mod_ref[pl.ds(row, 1), 3 * D_MODEL:4 * D_MODEL]
    scale = mod_ref[pl.ds(row, 1), 4 * D_MODEL:5 * D_MODEL]
    x = h_ref[0]
    xn = x * lax.rsqrt(jnp.mean(x * x, axis=-1, keepdims=True) + EPS) * g_ref[...]
    xm = xn * (1.0 + scale) + shift
    xb = xm.astype(bf16)
    bits = lax.bitcast_convert_type(xb.astype(f32), jnp.int32)
    for r in range(EXPERT_ROWS):
        lo = (bits[:, (2 * r) * 128:(2 * r + 1) * 128] >> 16) & 0xFFFF
        hi = bits[:, (2 * r + 1) * 128:(2 * r + 2) * 128] & jnp.int32(-65536)
        xw_ref[r] = lo | hi
    for hp in range(2 * PEER_HEADS):
        lo = hp * PEER_HALF
        q_scr[hp] = jnp.dot(xb, wq_ref[:, lo:lo + PEER_HALF],
                            preferred_element_type=f32).astype(bf16)

    half = PEER_TOPK // 2

    def head_body(hd, carry):
        _top16([(lax.dot_general(keys_ref[p], q_scr[2 * hd + p], _NT,
                                 preferred_element_type=f32),
                 None, p * PEER_TOPK) for p in range(2)], sv, si)
        s1, s2 = sv[0:PEER_TOPK, :], sv[PEER_TOPK:2 * PEER_TOPK, :]
        i1, i2 = si[0:PEER_TOPK, :], si[PEER_TOPK:2 * PEER_TOPK, :]
        cand, cexp = [], []
        for i in range(half):
            size = PEER_TOPK if i == 0 else half
            blk = s1[i:i + 1, :] + s2[0:size, :]
            keep = PEER_TOPK // (i + 1)
            if keep < size:
                jrow = lax.broadcasted_iota(jnp.int32, (size, 1), 0)
                blk = jnp.where(jrow < keep, blk, -jnp.inf)
            cand.append(blk)
            cexp.append(i1[i:i + 1, :] * PEER_N_KEYS + i2[0:size, :])
        cand.append(s1[half:PEER_TOPK, :] + s2[0:1, :])
        cexp.append(i1[half:PEER_TOPK, :] * PEER_N_KEYS + i2[0:1, :])
        cand = jnp.concatenate(cand, axis=0)
        cexp = jnp.concatenate(cexp, axis=0)
        base = pl.multiple_of(hd * PEER_TOPK, PEER_TOPK)
        _top16([(cand, cexp, base)], bv, be)
        return carry

    lax.fori_loop(0, PEER_HEADS, head_body, 0)

    for hd in range(PEER_HEADS):
        best = bv[hd * PEER_TOPK:(hd + 1) * PEER_TOPK, :]
        e = jnp.exp(best - jnp.max(best, axis=0, keepdims=True))
        bv[hd * PEER_TOPK:(hd + 1) * PEER_TOPK, :] = e / jnp.sum(e, axis=0, keepdims=True)
    for t in range(ROW_BLOCK // PEER_TOK):
        idx_ref[t] = be[:, t * PEER_TOK:(t + 1) * PEER_TOK].T * EXPERT_ROWS
        gate_ref[t] = bv[:, t * PEER_TOK:(t + 1) * PEER_TOK]


def peer_query(h, mod, g, wq_bf16, keys_bf16, n_ctx):
    bsz, n_tok, _ = h.shape
    nblk = n_tok // ROW_BLOCK
    sub = ROW_BLOCK // PEER_TOK
    n_pblk = bsz * n_tok // PEER_TOK
    row_spec = pl.BlockSpec((1, ROW_BLOCK, D_MODEL), lambda b, j: (b, j, 0))
    full = lambda a: pl.BlockSpec(a.shape, lambda b, j: (0,) * a.ndim)
    pick_spec = pl.BlockSpec((sub, PEER_PICKS, PEER_TOK), lambda b, j: (b * nblk + j, 0, 0))
    f32, i32 = jnp.float32, jnp.int32
    return pl.pallas_call(
        functools.partial(_peer_query_kernel, n_ctx // ROW_BLOCK),
        grid=(bsz, nblk),
        in_specs=[row_spec, full(mod), full(g), full(wq_bf16), full(keys_bf16)],
        out_specs=[pl.BlockSpec((EXPERT_ROWS, ROW_BLOCK, 128), lambda b, j: (0, b * nblk + j, 0)),
                   pick_spec, pick_spec],
        out_shape=[jax.ShapeDtypeStruct((EXPERT_ROWS, bsz * n_tok, 128), i32),
                   jax.ShapeDtypeStruct((n_pblk, PEER_PICKS, PEER_TOK), i32),
                   jax.ShapeDtypeStruct((n_pblk, PEER_PICKS, PEER_TOK), f32)],
        scratch_shapes=[pltpu.VMEM((2 * PEER_HEADS, ROW_BLOCK, PEER_HALF), jnp.bfloat16),
                        pltpu.VMEM((2 * PEER_TOPK, ROW_BLOCK), f32),
                        pltpu.VMEM((2 * PEER_TOPK, ROW_BLOCK), i32),
                        pltpu.VMEM((PEER_PICKS, ROW_BLOCK), f32),
                        pltpu.VMEM((PEER_PICKS, ROW_BLOCK), i32)],
        compiler_params=pltpu.CompilerParams(vmem_limit_bytes=VMEM_LIMIT),
        name="peer_query",
    )(h, mod, g, wq_bf16, keys_bf16)


def pack_rows(rows):
    bits = lax.bitcast_convert_type(rows.astype(jnp.bfloat16), jnp.uint16).astype(jnp.uint32)
    bits = bits.reshape(-1, EXPERT_ROWS, 2, 128)
    words = bits[:, :, 0, :] | (bits[:, :, 1, :] << 16)
    return lax.bitcast_convert_type(words, jnp.int32)


def _expert_chunks(idx_ref, tab_ref, t):
    row = idx_ref.at[0, 0, pl.ds(t * PEER_PICKS, PEER_PICKS)]
    for c in range(PEER_PICKS // PEER_CHUNK):
        tiles = [tab_ref[pl.ds(pl.multiple_of(row[c * PEER_CHUNK + j], EXPERT_ROWS),
                               EXPERT_ROWS), :] for j in range(PEER_CHUNK)]
        yield c, jnp.concatenate(tiles, axis=0)


def _group_chunks(idx_ref, tab_ref, r0):
    rows = [idx_ref.at[0, 0, pl.ds((r0 + a) * PEER_PICKS, PEER_PICKS)] for a in range(PEER_STAGE)]
    for c in range(PEER_PICKS // PEER_CHUNK):
        tiles = [[None] * PEER_CHUNK for _ in range(PEER_STAGE)]
        for j in range(PEER_CHUNK):
            for a in range(PEER_STAGE):
                off = pl.multiple_of(rows[a][c * PEER_CHUNK + j], EXPERT_ROWS)
                tiles[a][j] = tab_ref[pl.ds(off, EXPERT_ROWS), :]
        yield c, [jnp.concatenate(tiles[a], axis=0) for a in range(PEER_STAGE)]


def _pick_rows_mask():
    tile_rows = 2 * EXPERT_ROWS
    shape = (PEER_PICKS, PEER_PICKS * tile_rows)
    return (lax.broadcasted_iota(jnp.int32, shape, 1) // tile_rows
            == lax.broadcasted_iota(jnp.int32, shape, 0))


def _peer_score_kernel(idx_ref, x_ref, gate_ref, tab_ref, w_ref):
    f32, bf16, i32 = jnp.float32, jnp.bfloat16, jnp.int32
    sel = _pick_rows_mask().astype(bf16)
    lane = lax.broadcasted_iota(i32, (PEER_PICKS, PEER_TOK), 1)
    cols = PEER_CHUNK * 2 * EXPERT_ROWS

    def group_body(g, acc):
        r0 = pl.multiple_of(g * PEER_STAGE, PEER_STAGE)
        xg = x_ref[:, pl.ds(r0, PEER_STAGE), :]
        xts = []
        for a in range(PEER_STAGE):
            xw = jnp.concatenate([xg[r, a:a + 1, :] for r in range(EXPERT_ROWS)], axis=0)
            xts.append(pltpu.bitcast(jnp.concatenate([xw] * PEER_CHUNK, axis=0), bf16))
        parts = [[] for _ in range(PEER_STAGE)]
        for c, words in _group_chunks(idx_ref, tab_ref, r0):
            for a in range(PEER_STAGE):
                prod = pltpu.bitcast(words[a], bf16) * xts[a]
                parts[a].append(jnp.dot(sel[0:PEER_CHUNK, 0:cols], prod,
                                        preferred_element_type=f32))
        for a in range(PEER_STAGE):
            part = jnp.concatenate(parts[a], axis=0)
            acc = jnp.where(lane == r0 + a, jnp.sum(part, axis=1, keepdims=True), acc)
        return acc

    scores = lax.fori_loop(0, PEER_TOK // PEER_STAGE, group_body,
                           jnp.zeros((PEER_PICKS, PEER_TOK), f32))
    w = gate_ref[0] * jax.nn.gelu(scores)
    w_ref[...] = jnp.dot(w.T.astype(bf16), sel, preferred_element_type=f32)


def _peer_mix_kernel(idx_ref, wexp_ref, h_ref, gate_ref, tab_ref, o_ref):
    f32, bf16, i32 = jnp.float32, jnp.bfloat16, jnp.int32
    tile_rows = 2 * EXPERT_ROWS
    shape = (tile_rows, PEER_PICKS * tile_rows)
    diag = (lax.broadcasted_iota(i32, shape, 1) % tile_rows
            == lax.broadcasted_iota(i32, shape, 0))
    cols = PEER_CHUNK * tile_rows

    def group_body(g, carry):
        r0 = pl.multiple_of(g * PEER_STAGE, PEER_STAGE)
        tiles = []
        wmats = [jnp.where(diag, wexp_ref[pl.ds(r0 + a, 1), :], 0.0).astype(bf16)
                 for a in range(PEER_STAGE)]
        tiles = [jnp.zeros((tile_rows, 128), f32)] * PEER_STAGE
        for c, words in _group_chunks(idx_ref, tab_ref, r0):
            tiles = [tiles[a] + jnp.dot(wmats[a][:, c * cols:(c + 1) * cols],
                                        pltpu.bitcast(words[a], bf16),
                                        preferred_element_type=f32) for a in range(PEER_STAGE)]
        rows = jnp.concatenate(
            [jnp.concatenate([tile[r:r + 1, :] for tile in tiles], axis=0)
             for r in range(tile_rows)], axis=1)
        o_ref[pl.ds(r0, PEER_STAGE), :] = (h_ref[pl.ds(r0, PEER_STAGE), :]
                                           + gate_ref[0, 0] * rows)
        return carry

    lax.fori_loop(0, PEER_TOK // PEER_STAGE, group_body, 0)


def _table_spec(tab):
    return pl.BlockSpec(tab.shape, lambda i: (0,) * tab.ndim, pipeline_mode=pl.Buffered(1))


def peer_experts(h, xw, idx, gates, mod, u_packed, v_packed, n_ctx):
    bsz, n_tok, _ = h.shape
    n_pblk = idx.shape[0]
    blk_per_batch = n_tok // PEER_TOK
    ctx_pblk = n_ctx // PEER_TOK
    flat = PEER_PICKS * PEER_TOK
    smem_spec = pl.BlockSpec((1, 1, flat), lambda i: (i, 0, 0), memory_space=pltpu.SMEM)
    pick_spec = pl.BlockSpec((1, PEER_PICKS, PEER_TOK), lambda i: (i, 0, 0))
    word_spec = pl.BlockSpec((EXPERT_ROWS, PEER_TOK, 128), lambda i: (0, i, 0))
    row_spec = pl.BlockSpec((PEER_TOK, D_MODEL), lambda i: (i, 0))
    idx_flat = idx.reshape(n_pblk, 1, flat)
    wexp = pl.pallas_call(
        _peer_score_kernel,
        grid=(n_pblk,),
        in_specs=[smem_spec, word_spec, pick_spec, _table_spec(u_packed)],
        out_specs=row_spec,
        out_shape=jax.ShapeDtypeStruct((bsz * n_tok, D_MODEL), jnp.float32),
        compiler_params=pltpu.CompilerParams(vmem_limit_bytes=VMEM_LIMIT),
        name="peer_scores",
    )(idx_flat, xw, gates, u_packed)

    def gate_map(i):
        bi, ji = i // blk_per_batch, i % blk_per_batch
        return (jnp.where(ji < ctx_pblk, 0, bi + 1), N_MOD - 1, 0, 0)

    out = pl.pallas_call(
        _peer_mix_kernel,
        grid=(n_pblk,),
        in_specs=[smem_spec, row_spec, row_spec,
                  pl.BlockSpec((1, 1, 1, D_MODEL), gate_map), _table_spec(v_packed)],
        out_specs=row_spec,
        out_shape=jax.ShapeDtypeStruct((bsz * n_tok, D_MODEL), jnp.float32),
        compiler_params=pltpu.CompilerParams(vmem_limit_bytes=VMEM_LIMIT),
        name="peer_mix",
    )(idx_flat, wexp, h.reshape(-1, D_MODEL), mod.reshape(MOD_ROWS, N_MOD, 1, D_MODEL), v_packed)
    return out.reshape(h.shape)


def kernel(x, c, ctx, c_ctx, w_mod, b_mod, norm_mix_g, w_in, q_norm_g, k_norm_g, ssm_a_re, ssm_a_im, ssm_log_dt, ssm_b_re, ssm_b_im, ssm_c_re, ssm_c_im, ssm_d, w_glu, b_glu, w_branch_attn, w_branch_ssm, w_out, norm_ffn_g, peer_w_q, peer_keys, peer_u, peer_v):
    bf16 = jnp.bfloat16
    bsz, n_lat, _ = x.shape
    n_ctx = ctx.shape[1]
    depth = w_mod.shape[0]
    assert bsz + 1 <= MOD_ROWS and n_ctx % ROW_BLOCK == 0 and n_lat % ROW_BLOCK == 0

    cc = jnp.zeros((MOD_ROWS, D_MODEL), jnp.float32).at[0].set(c_ctx).at[1:bsz + 1].set(c)
    mods = modulation(cc, w_mod, b_mod)
    rope = rope_tables(n_ctx, n_lat)
    h = jnp.concatenate([ctx, x], axis=1)
    row = lambda a: a.reshape(1, -1)

    for i in range(depth):
        mod = mods[i]
        q, k, v, u, ga, gb = in_projection(h, mod, row(norm_mix_g[i]), w_in[i].astype(bf16), rope,
                                           row(q_norm_g[i]), row(k_norm_g[i]), n_ctx)
        attn = attention(q, k, v, n_ctx)
        y = s5_scan(u.astype(bf16), ssm_a_re[i], ssm_a_im[i], ssm_log_dt[i],
                    ssm_b_re[i], ssm_b_im[i], ssm_c_re[i], ssm_c_im[i], n_ctx)
        h = branch_merge(h, mod, attn, y, u, ga, gb, row(ssm_d[i]), w_glu[i].astype(bf16),
                         row(b_glu[i]), w_branch_attn[i].astype(bf16),
                         w_branch_ssm[i].astype(bf16), w_out[i].astype(bf16), n_ctx)
        xw, idx, gates = peer_query(h, mod, row(norm_ffn_g[i]), peer_w_q[i].astype(bf16),
                                    peer_keys[i].astype(bf16), n_ctx)
        h = peer_experts(h, xw, idx, gates, mod, pack_rows(peer_u[i]).reshape(-1, 128),
                         pack_rows(peer_v[i]).reshape(-1, 128), n_ctx)
    return h[:, n_ctx:, :]
```

```python
import functools
import math

import jax
import jax.numpy as jnp
from jax import lax
from jax.experimental import pallas as pl
from jax.experimental.pallas import tpu as pltpu

D_MODEL = 1024
GRID_W = 64
HEAD_DIM = 128
N_Q_HEADS = 8
N_KV_HEADS = 2
GQA_GROUP = N_Q_HEADS // N_KV_HEADS
ATTN_WIDTH = N_Q_HEADS * HEAD_DIM
KV_WIDTH = N_KV_HEADS * HEAD_DIM
ROPE_THETA = 10000.0
ROPE_FREQS = HEAD_DIM // 4
ATTN_SCALE = HEAD_DIM ** -0.5
SSM_WIDTH = D_MODEL // 2
SSM_GROUP = 16
SSM_GROUPS = SSM_WIDTH // SSM_GROUP
SSM_STATE = 64
IN_WIDTH = ATTN_WIDTH + 2 * KV_WIDTH + SSM_WIDTH + 2 * D_MODEL
PEER_HEADS = 8
PEER_N_KEYS = 128
PEER_N_EXPERTS = PEER_N_KEYS ** 2
PEER_QUERY_DIM = 256
PEER_HALF = PEER_QUERY_DIM // 2
PEER_TOPK = 16
PEER_PICKS = PEER_HEADS * PEER_TOPK
N_MOD = 6
EPS = 1e-6

ROW_BLOCK = 256
ATTN_KEYS = 4096
MOD_ROWS = 16
S5_CHUNK = 16
PEER_TOK = 128
EXPERT_ROWS = 4
PEER_STAGE = 32
PEER_CHUNK = 32
VMEM_LIMIT = 56 * 1024 * 1024

_NT = (((1,), (1,)), ((), ()))


def _mod_row(b, j, ctx_blocks):
    return jnp.where(j < ctx_blocks, 0, b + 1)


def _mod_kernel(c_ref, w_ref, b_ref, o_ref):
    c = c_ref[...]
    s = c * jax.nn.sigmoid(c)
    o_ref[0] = jnp.dot(s.astype(jnp.bfloat16), w_ref[0].astype(jnp.bfloat16),
                       preferred_element_type=jnp.float32) + b_ref[0]


def modulation(cc, w_mod, b_mod):
    depth = w_mod.shape[0]
    nblk = 1536
    width = N_MOD * D_MODEL
    return pl.pallas_call(
        _mod_kernel,
        grid=(depth, width // nblk),
        in_specs=[pl.BlockSpec((MOD_ROWS, D_MODEL), lambda l, n: (0, 0)),
                  pl.BlockSpec((1, D_MODEL, nblk), lambda l, n: (l, 0, n)),
                  pl.BlockSpec((1, 1, nblk), lambda l, n: (l, 0, n))],
        out_specs=pl.BlockSpec((1, MOD_ROWS, nblk), lambda l, n: (l, 0, n)),
        out_shape=jax.ShapeDtypeStruct((depth, MOD_ROWS, width), jnp.float32),
        name="modulation",
    )(cc, w_mod, b_mod.reshape(depth, 1, width))


def _inproj_kernel(ctx_blocks, h_ref, mod_ref, g_ref, w_ref, cos_ref, sa_ref, sb_ref,
                   qg_ref, kg_ref, q_ref, k_ref, v_ref, u_ref, ga_ref, gb_ref):
    b, j = pl.program_id(0), pl.program_id(1)
    row = _mod_row(b, j, ctx_blocks)
    shift = mod_ref[pl.ds(row, 1), 0:D_MODEL]
    scale = mod_ref[pl.ds(row, 1), D_MODEL:2 * D_MODEL]
    x = h_ref[0]
    xn = x * lax.rsqrt(jnp.mean(x * x, axis=-1, keepdims=True) + EPS) * g_ref[...]
    xm = (xn * (1.0 + scale) + shift).astype(jnp.bfloat16)

    cos, sa, sb = cos_ref[...], sa_ref[...], sb_ref[...]

    def norm_rope(z, g, out_scale):
        zn = z * lax.rsqrt(jnp.mean(z * z, axis=-1, keepdims=True) + EPS) * g
        r = zn * cos + pltpu.roll(zn, 96, 1) * sa + pltpu.roll(zn, 32, 1) * sb
        return r * out_scale

    for hd in range(N_Q_HEADS):
        lo = hd * HEAD_DIM
        z = jnp.dot(xm, w_ref[:, lo:lo + HEAD_DIM], preferred_element_type=jnp.float32)
        q_ref[0, :, lo:lo + HEAD_DIM] = norm_rope(z, qg_ref[...], ATTN_SCALE * math.log2(math.e)).astype(q_ref.dtype)
    for hd in range(N_KV_HEADS):
        lo = hd * HEAD_DIM
        z = jnp.dot(xm, w_ref[:, ATTN_WIDTH + lo:ATTN_WIDTH + lo + HEAD_DIM],
                    preferred_element_type=jnp.float32)
        k_ref[0, :, lo:lo + HEAD_DIM] = norm_rope(z, kg_ref[...], 1.0).astype(k_ref.dtype)
    off = ATTN_WIDTH + KV_WIDTH
    v_ref[0] = jnp.dot(xm, w_ref[:, off:off + KV_WIDTH],
                       preferred_element_type=jnp.float32).astype(v_ref.dtype)
    off += KV_WIDTH
    u_ref[0] = jnp.dot(xm, w_ref[:, off:off + SSM_WIDTH], preferred_element_type=jnp.float32)
    off += SSM_WIDTH
    ga_ref[0] = jnp.dot(xm, w_ref[:, off:off + D_MODEL], preferred_element_type=jnp.float32)
    off += D_MODEL
    gb_ref[0] = jnp.dot(xm, w_ref[:, off:off + D_MODEL], preferred_element_type=jnp.float32)


def in_projection(h, mod, g, w_in_bf16, rope, qg, kg, n_ctx):
    bsz, n_tok, _ = h.shape
    nblk = n_tok // ROW_BLOCK
    cos, sa, sb = rope
    row_spec = lambda w: pl.BlockSpec((1, ROW_BLOCK, w), lambda b, j: (b, j, 0))
    full = lambda a: pl.BlockSpec(a.shape, lambda b, j: (0,) * a.ndim)
    rope_spec = pl.BlockSpec((ROW_BLOCK, HEAD_DIM), lambda b, j: (j, 0))
    f32, bf16 = jnp.float32, jnp.bfloat16
    out_shape = [jax.ShapeDtypeStruct((bsz, n_tok, ATTN_WIDTH), bf16),
                 jax.ShapeDtypeStruct((bsz, n_tok, KV_WIDTH), bf16),
                 jax.ShapeDtypeStruct((bsz, n_tok, KV_WIDTH), bf16),
                 jax.ShapeDtypeStruct((bsz, n_tok, SSM_WIDTH), f32),
                 jax.ShapeDtypeStruct((bsz, n_tok, D_MODEL), f32),
                 jax.ShapeDtypeStruct((bsz, n_tok, D_MODEL), f32)]
    return pl.pallas_call(
        functools.partial(_inproj_kernel, n_ctx // ROW_BLOCK),
        grid=(bsz, nblk),
        in_specs=[row_spec(D_MODEL), full(mod), full(g), full(w_in_bf16),
                  rope_spec, rope_spec, rope_spec, full(qg), full(kg)],
        out_specs=[row_spec(ATTN_WIDTH), row_spec(KV_WIDTH), row_spec(KV_WIDTH),
                   row_spec(SSM_WIDTH), row_spec(D_MODEL), row_spec(D_MODEL)],
        out_shape=out_shape,
        compiler_params=pltpu.CompilerParams(vmem_limit_bytes=VMEM_LIMIT),
        name="in_projection",
    )(h, mod, g, w_in_bf16, cos, sa, sb, qg, kg)


def rope_tables(n_ctx, n_lat):
    pos = jnp.arange(n_lat)
    inv_freq = ROPE_THETA ** (-jnp.arange(ROPE_FREQS, dtype=jnp.float32) / ROPE_FREQS)
    ang_row = (pos // GRID_W).astype(jnp.float32)[:, None] * inv_freq
    ang_col = (pos % GRID_W).astype(jnp.float32)[:, None] * inv_freq
    ang = jnp.concatenate([ang_row, ang_row, ang_col, ang_col], axis=-1)
    cos, sin = jnp.cos(ang), jnp.sin(ang)
    even = ((jnp.arange(HEAD_DIM) // ROPE_FREQS) % 2 == 0)[None, :]
    sa = jnp.where(even, -sin, 0.0)
    sb = jnp.where(even, 0.0, sin)
    pad = lambda t, v: jnp.concatenate([jnp.full((n_ctx, HEAD_DIM), v, jnp.float32), t], axis=0)
    return pad(cos, 1.0), pad(sa, 0.0), pad(sb, 0.0)


def _attn_kernel(n_ctx, n_lat, key_chunk, ctx_blocks, q_ref, k_ref, v_ref, o_ref, m_scr, l_scr, acc_scr):
    f32 = jnp.float32
    j = pl.program_id(2)
    q = jnp.concatenate([q_ref[0, :, hd * HEAD_DIM:(hd + 1) * HEAD_DIM]
                         for hd in range(GQA_GROUP)], axis=0)

    kc, vc = k_ref[0, 0:n_ctx, :], v_ref[0, 0:n_ctx, :]
    s = lax.dot_general(q, kc, _NT, preferred_element_type=f32)
    m = jnp.max(s, axis=-1, keepdims=True)
    p = jnp.exp2(s - m)
    m_scr[...] = m
    l_scr[...] = jnp.sum(p, axis=-1, keepdims=True)
    acc_scr[...] = jnp.dot(p.astype(vc.dtype), vc, preferred_element_type=f32)

    @pl.when(j >= ctx_blocks)
    def _():
        def body(c, carry):
            r0 = pl.multiple_of(n_ctx + c * key_chunk, ROW_BLOCK)
            kc = k_ref[0, pl.ds(r0, key_chunk), :]
            vc = v_ref[0, pl.ds(r0, key_chunk), :]
            s = lax.dot_general(q, kc, _NT, preferred_element_type=f32)
            m_old = m_scr[...]
            m_new = jnp.maximum(m_old, jnp.max(s, axis=-1, keepdims=True))
            alpha = jnp.exp2(m_old - m_new)
            p = jnp.exp2(s - m_new)
            m_scr[...] = m_new
            l_scr[...] = alpha * l_scr[...] + jnp.sum(p, axis=-1, keepdims=True)
            acc_scr[...] = alpha * acc_scr[...] + jnp.dot(p.astype(vc.dtype), vc,
                                                          preferred_element_type=f32)
            return carry

        lax.fori_loop(0, n_lat // key_chunk, body, 0)

    out = acc_scr[...] / l_scr[...]
    for hd in range(GQA_GROUP):
        o_ref[0, :, hd * HEAD_DIM:(hd + 1) * HEAD_DIM] = (
            out[hd * ROW_BLOCK:(hd + 1) * ROW_BLOCK]).astype(o_ref.dtype)


def attention(q, k, v, n_ctx):
    bsz, n_tok, _ = q.shape
    nblk = n_tok // ROW_BLOCK
    gw = GQA_GROUP * HEAD_DIM
    rows = GQA_GROUP * ROW_BLOCK
    key_chunk = math.gcd(n_tok - n_ctx, ATTN_KEYS)
    return pl.pallas_call(
        functools.partial(_attn_kernel, n_ctx, n_tok - n_ctx, key_chunk, n_ctx // ROW_BLOCK),
        grid=(bsz, N_KV_HEADS, nblk),
        in_specs=[pl.BlockSpec((1, ROW_BLOCK, gw), lambda b, g, j: (b, j, g)),
                  pl.BlockSpec((1, n_tok, HEAD_DIM), lambda b, g, j: (b, 0, g)),
                  pl.BlockSpec((1, n_tok, HEAD_DIM), lambda b, g, j: (b, 0, g))],
        out_specs=pl.BlockSpec((1, ROW_BLOCK, gw), lambda b, g, j: (b, j, g)),
        out_shape=jax.ShapeDtypeStruct((bsz, n_tok, ATTN_WIDTH), jnp.bfloat16),
        scratch_shapes=[pltpu.VMEM((rows, 1), jnp.float32), pltpu.VMEM((rows, 1), jnp.float32),
                        pltpu.VMEM((rows, HEAD_DIM), jnp.float32)],
        compiler_params=pltpu.CompilerParams(vmem_limit_bytes=VMEM_LIMIT),
        name="attention",
    )(q, k, v)


def _cpow(e, lam_re_dt, lam_im_dt):
    mag = jnp.exp(e * lam_re_dt)
    ang = e * lam_im_dt
    return mag * jnp.cos(ang), mag * jnp.sin(ang)


def _s5_kernel(n_batch, ctx_chunks, n_chunks,
               u_ref, arow_ref, acol_ref, b_ref, c_ref, ct_ref, y_ref,
               sfr, sfi, sbr, sbi, hfr, hfi, hbr, hbi):
    f32, bf16 = jnp.float32, jnp.bfloat16
    width = S5_CHUNK * SSM_GROUP
    lane = lax.broadcasted_iota(jnp.int32, (1, width), 1)
    lane_blk = lane // SSM_GROUP
    expand = (lax.broadcasted_iota(jnp.int32, (SSM_GROUP, width), 1) % SSM_GROUP
              == lax.broadcasted_iota(jnp.int32, (SSM_GROUP, width), 0)).astype(f32)
    rowi = lax.broadcasted_iota(jnp.int32, (width, 1), 0) // SSM_GROUP
    u = u_ref[0].astype(bf16)

    m_tot = jnp.zeros((width, width), f32)
    proj, read, step = [], [], []
    for d in range(2):
        fwd = d == 0
        are_c, aim_c, dt_c = acol_ref[d, 0, 0], acol_ref[d, 0, 1], jnp.exp(acol_ref[d, 0, 2])
        lr_c, li_c = are_c * dt_c, aim_c * dt_c
        abr, abi = _cpow(1.0, lr_c, li_c)
        den = are_c * are_c + aim_c * aim_c
        nr, ni = abr - 1.0, abi
        k_r = (nr * are_c + ni * aim_c) / den
        k_i = (ni * are_c - nr * aim_c) / den
        b_re, b_im = b_ref[d, 0, 0], b_ref[d, 0, 1]
        bb_r = k_r * b_re - k_i * b_im
        bb_i = k_r * b_im + k_i * b_re
        bt_r = jnp.dot(bb_r, expand, preferred_element_type=f32)
        bt_i = jnp.dot(bb_i, expand, preferred_element_type=f32)
        e_in = (S5_CHUNK - 1 - lane_blk if fwd else lane_blk).astype(f32)
        ap_r, ap_i = _cpow(e_in, lr_c, li_c)
        proj.append(((ap_r * bt_r - ap_i * bt_i).astype(bf16),
                     (ap_r * bt_i + ap_i * bt_r).astype(bf16)))
        ct_r = jnp.dot(ct_ref[d, 0, 0], expand, preferred_element_type=f32)
        ct_i = jnp.dot(ct_ref[d, 0, 1], expand, preferred_element_type=f32)
        e_out = (lane_blk + 1 if fwd else S5_CHUNK - lane_blk).astype(f32)
        aq_r, aq_i = _cpow(e_out, lr_c, li_c)
        read.append(((ct_r * aq_r - ct_i * aq_i).astype(bf16),
                     (-(ct_r * aq_i + ct_i * aq_r)).astype(bf16)))
        are_r, aim_r, dt_r = arow_ref[d, 0, 0], arow_ref[d, 0, 1], jnp.exp(arow_ref[d, 0, 2])
        lr_r, li_r = are_r * dt_r, aim_r * dt_r
        step.append(_cpow(float(S5_CHUNK), lr_r, li_r))
        lag = (rowi if fwd else S5_CHUNK - 1 - rowi).astype(f32)
        al_r, al_i = _cpow(lag, lr_r, li_r)
        c_re = jnp.concatenate([c_ref[d, 0, 0]] * S5_CHUNK, axis=0)
        c_im = jnp.concatenate([c_ref[d, 0, 1]] * S5_CHUNK, axis=0)
        ca_r = c_re * al_r - c_im * al_i
        ca_i = c_re * al_i + c_im * al_r
        kmat = (jnp.dot(ca_r, bb_r, preferred_element_type=f32)
                - jnp.dot(ca_i, bb_i, preferred_element_type=f32))
        kt = jnp.dot(kmat, expand, preferred_element_type=f32)
        for s in range(S5_CHUNK):
            sh = (s if fwd else S5_CHUNK - 1 - s) * SSM_GROUP
            if sh == 0:
                shifted = kt
            elif fwd:
                shifted = jnp.concatenate([jnp.zeros((sh, width), f32), kt[:width - sh]], axis=0)
            else:
                shifted = jnp.concatenate([kt[sh:], jnp.zeros((sh, width), f32)], axis=0)
            m_tot = m_tot + jnp.where(lane_blk == s, shifted, 0.0)

    y_ref[0] = lax.dot_general(u, m_tot.astype(bf16), _NT, preferred_element_type=f32)
    sfr[...] = lax.dot_general(u, proj[0][0], _NT, preferred_element_type=f32)
    sfi[...] = lax.dot_general(u, proj[0][1], _NT, preferred_element_type=f32)
    sbr[...] = lax.dot_general(u, proj[1][0], _NT, preferred_element_type=f32)
    sbi[...] = lax.dot_general(u, proj[1][1], _NT, preferred_element_type=f32)

    (afr, afi), (abr_, abi_) = step

    def scan_body(i, carry):
        fr, fi, br, bi = carry
        rf = pl.multiple_of(i * n_batch, n_batch)
        hfr[pl.ds(rf, n_batch), :] = fr
        hfi[pl.ds(rf, n_batch), :] = fi
        nfr = afr * fr - afi * fi + sfr[pl.ds(rf, n_batch), :]
        nfi = afr * fi + afi * fr + sfi[pl.ds(rf, n_batch), :]
        cb = jnp.where(i < ctx_chunks, ctx_chunks - 1 - i, n_chunks - 1 + ctx_chunks - i)
        rb = pl.multiple_of(cb * n_batch, n_batch)
        hbr[pl.ds(rb, n_batch), :] = br
        hbi[pl.ds(rb, n_batch), :] = bi
        nbr = abr_ * br - abi_ * bi + sbr[pl.ds(rb, n_batch), :]
        nbi = abr_ * bi + abi_ * br + sbi[pl.ds(rb, n_batch), :]
        return nfr, nfi, nbr, nbi

    z = jnp.zeros((n_batch, SSM_STATE), f32)
    lax.fori_loop(0, n_chunks, scan_body, (z, z, z, z))

    y_ref[0] += (jnp.dot(hfr[...].astype(bf16), read[0][0], preferred_element_type=f32)
                 + jnp.dot(hfi[...].astype(bf16), read[0][1], preferred_element_type=f32)
                 + jnp.dot(hbr[...].astype(bf16), read[1][0], preferred_element_type=f32)
                 + jnp.dot(hbi[...].astype(bf16), read[1][1], preferred_element_type=f32))


def s5_scan(u, a_re, a_im, log_dt, b_re, b_im, c_re, c_im, n_ctx):
    bsz, n_tok, _ = u.shape
    n_chunks = n_tok // S5_CHUNK
    rows = n_chunks * bsz
    width = S5_CHUNK * SSM_GROUP
    ug = u.reshape(bsz, n_chunks, S5_CHUNK, SSM_GROUPS, SSM_GROUP)
    ug = ug.transpose(3, 1, 0, 2, 4).reshape(SSM_GROUPS, rows, width)
    ldt = jnp.broadcast_to(log_dt[..., None], a_re.shape)
    arow = jnp.stack([a_re, a_im, ldt], axis=2)[:, :, :, None, :]
    acol = jnp.stack([a_re, a_im, ldt], axis=2)[..., None]
    bmat = jnp.stack([b_re, b_im], axis=2)
    cmat = jnp.stack([c_re, c_im], axis=2)
    ctmat = jnp.swapaxes(cmat, -1, -2)
    grp = lambda a: pl.BlockSpec((2, 1) + a.shape[2:], lambda g: (0, g) + (0,) * (a.ndim - 2))
    state = pltpu.VMEM((rows, SSM_STATE), jnp.float32)
    y = pl.pallas_call(
        functools.partial(_s5_kernel, bsz, n_ctx // S5_CHUNK, n_chunks),
        grid=(SSM_GROUPS,),
        in_specs=[pl.BlockSpec((1, rows, width), lambda g: (g, 0, 0)),
                  grp(arow), grp(acol), grp(bmat), grp(cmat), grp(ctmat)],
        out_specs=pl.BlockSpec((1, rows, width), lambda g: (g, 0, 0)),
        out_shape=jax.ShapeDtypeStruct((SSM_GROUPS, rows, width), jnp.float32),
        scratch_shapes=[state] * 8,
        compiler_params=pltpu.CompilerParams(vmem_limit_bytes=VMEM_LIMIT),
        name="s5_scan",
    )(ug, arow, acol, bmat, cmat, ctmat)
    y = y.reshape(SSM_GROUPS, n_chunks, bsz, S5_CHUNK, SSM_GROUP)
    return y.transpose(2, 1, 3, 0, 4).reshape(bsz, n_tok, SSM_WIDTH)


def _merge_kernel(ctx_blocks, h_ref, mod_ref, attn_ref, y_ref, u_ref, ga_ref, gb_ref,
                  dskip_ref, wglu_ref, bglu_ref, wba_ref, wbs_ref, wo_ref, o_ref):
    f32, bf16 = jnp.float32, jnp.bfloat16
    b, j = pl.program_id(0), pl.program_id(1)
    row = _mod_row(b, j, ctx_blocks)
    gate = mod_ref[pl.ds(row, 1), 2 * D_MODEL:3 * D_MODEL]
    y = jax.nn.gelu(y_ref[0] + dskip_ref[...] * u_ref[0])
    glu = jnp.dot(y.astype(bf16), wglu_ref[...], preferred_element_type=f32) + bglu_ref[...]
    ssm = y * jax.nn.sigmoid(glu)
    ba = jnp.dot(attn_ref[0], wba_ref[...], preferred_element_type=f32)
    bs = jnp.dot(ssm.astype(bf16), wbs_ref[...], preferred_element_type=f32)
    mix = jax.nn.sigmoid(ga_ref[0]) * ba + jax.nn.sigmoid(gb_ref[0]) * bs
    out = jnp.dot(mix.astype(bf16), wo_ref[...], preferred_element_type=f32)
    o_ref[0] = h_ref[0] + gate * out


def branch_merge(h, mod, attn, y, u, ga, gb, d_skip, w_glu, b_glu, w_ba, w_bs, w_o, n_ctx):
    bsz, n_tok, _ = h.shape
    nblk = n_tok // ROW_BLOCK
    row_spec = lambda w: pl.BlockSpec((1, ROW_BLOCK, w), lambda b, j: (b, j, 0))
    full = lambda a: pl.BlockSpec(a.shape, lambda b, j: (0,) * a.ndim)
    return pl.pallas_call(
        functools.partial(_merge_kernel, n_ctx // ROW_BLOCK),
        grid=(bsz, nblk),
        in_specs=[row_spec(D_MODEL), full(mod), row_spec(ATTN_WIDTH), row_spec(SSM_WIDTH),
                  row_spec(SSM_WIDTH), row_spec(D_MODEL), row_spec(D_MODEL),
                  full(d_skip), full(w_glu), full(b_glu), full(w_ba), full(w_bs), full(w_o)],
        out_specs=row_spec(D_MODEL),
        out_shape=jax.ShapeDtypeStruct(h.shape, jnp.float32),
        compiler_params=pltpu.CompilerParams(vmem_limit_bytes=VMEM_LIMIT),
        name="branch_merge",
    )(h, mod, attn, y, u, ga, gb, d_skip, w_glu, b_glu, w_ba, w_bs, w_o)


def _top16(jobs, vals_ref, pay_ref):
    rows = [lax.broadcasted_iota(jnp.int32, s.shape, 0).astype(jnp.float32) for s, _, _ in jobs]

    def body(r, carry):
        nxt = []
        for s, (s0, payload, base), row in zip(carry, jobs, rows):
            m = jnp.max(s, axis=0, keepdims=True)
            pos = jnp.min(jnp.where(s == m, row, float(s0.shape[0])), axis=0, keepdims=True)
            hit = row == pos
            vals_ref[pl.ds(base + r, 1), :] = m
            if payload is None:
                pay_ref[pl.ds(base + r, 1), :] = pos
            else:
                pay_ref[pl.ds(base + r, 1), :] = jnp.max(jnp.where(hit, payload, -1.0), axis=0,
                                                         keepdims=True)
            nxt.append(jnp.where(hit, -jnp.inf, s))
        return tuple(nxt)

    lax.fori_loop(0, PEER_TOPK, body, tuple(s for s, _, _ in jobs))


def _peer_query_kernel(ctx_blocks, h_ref, mod_ref, g_ref, wq_ref, keys_ref,
                       xw_ref, idx_ref, gate_ref, q_scr, sv, si, bv, be):
    f32, bf16 = jnp.float32, jnp.bfloat16
    b, j = pl.program_id(0), pl.program_id(1)
    row = _mod_row(b, j, ctx_blocks)
    shift = mod_ref[pl.ds(row, 1), 3 * D_MODEL:4 * D_MODEL]
    scale = mod_ref[pl.ds(row, 1), 4 * D_MODEL:5 * D_MODEL]
    x = h_ref[0]
    xn = x * lax.rsqrt(jnp.mean(x * x, axis=-1, keepdims=True) + EPS) * g_ref[...]
    xm = xn * (1.0 + scale) + shift
    xb = xm.astype(bf16)
    bits = lax.bitcast_convert_type(xb.astype(f32), jnp.int32)
    for r in range(EXPERT_ROWS):
        lo = (bits[:, (2 * r) * 128:(2 * r + 1) * 128] >> 16) & 0xFFFF
        hi = bits[:, (2 * r + 1) * 128:(2 * r + 2) * 128] & jnp.int32(-65536)
        xw_ref[r] = lo | hi
    for hp in range(2 * PEER_HEADS):
        lo = hp * PEER_HALF
        q_scr[hp] = jnp.dot(xb, wq_ref[:, lo:lo + PEER_HALF],
                            preferred_element_type=f32).astype(bf16)

    half = PEER_TOPK // 2

    def candidates(s1, s2, i1, i2):
        cand, cexp = [], []
        for i in range(half):
            size = PEER_TOPK if i == 0 else half
            blk = s1[i:i + 1, :] + s2[0:size, :]
            keep = PEER_TOPK // (i + 1)
            if keep < size:
                jrow = lax.broadcasted_iota(jnp.int32, (size, 1), 0)
                blk = jnp.where(jrow < keep, blk, -jnp.inf)
            cand.append(blk)
            cexp.append(i1[i:i + 1, :] * PEER_N_KEYS + i2[0:size, :])
        cand.append(s1[half:PEER_TOPK, :] + s2[0:1, :])
        cexp.append(i1[half:PEER_TOPK, :] * PEER_N_KEYS + i2[0:1, :])
        return jnp.concatenate(cand, axis=0), jnp.concatenate(cexp, axis=0)

    def head_pair_body(hp, carry):
        for d in range(2):
            _top16([(lax.dot_general(keys_ref[p], q_scr[4 * hp + 2 * d + p], _NT,
                                     preferred_element_type=f32),
                     None, (2 * d + p) * PEER_TOPK) for p in range(2)], sv, si)
        jobs = []
        for d in range(2):
            lo = 2 * d * PEER_TOPK
            s1, s2 = sv[lo:lo + PEER_TOPK, :], sv[lo + PEER_TOPK:lo + 2 * PEER_TOPK, :]
            i1, i2 = si[lo:lo + PEER_TOPK, :], si[lo + PEER_TOPK:lo + 2 * PEER_TOPK, :]
            cand, cexp = candidates(s1, s2, i1, i2)
            jobs.append((cand, cexp, pl.multiple_of((2 * hp + d) * PEER_TOPK, PEER_TOPK)))
        _top16(jobs, bv, be)
        return carry

    lax.fori_loop(0, PEER_HEADS // 2, head_pair_body, 0)

    for hd in range(PEER_HEADS):
        best = bv[hd * PEER_TOPK:(hd + 1) * PEER_TOPK, :]
        e = jnp.exp(best - jnp.max(best, axis=0, keepdims=True))
        bv[hd * PEER_TOPK:(hd + 1) * PEER_TOPK, :] = e / jnp.sum(e, axis=0, keepdims=True)
    for t in range(ROW_BLOCK // PEER_TOK):
        idx_ref[t] = (be[:, t * PEER_TOK:(t + 1) * PEER_TOK].T * EXPERT_ROWS).astype(jnp.int32)
        gate_ref[t] = bv[:, t * PEER_TOK:(t + 1) * PEER_TOK]


def peer_query(h, mod, g, wq_bf16, keys_bf16, n_ctx):
    bsz, n_tok, _ = h.shape
    nblk = n_tok // ROW_BLOCK
    sub = ROW_BLOCK // PEER_TOK
    n_pblk = bsz * n_tok // PEER_TOK
    row_spec = pl.BlockSpec((1, ROW_BLOCK, D_MODEL), lambda b, j: (b, j, 0))
    full = lambda a: pl.BlockSpec(a.shape, lambda b, j: (0,) * a.ndim)
    pick_spec = pl.BlockSpec((sub, PEER_PICKS, PEER_TOK), lambda b, j: (b * nblk + j, 0, 0))
    f32, i32 = jnp.float32, jnp.int32
    return pl.pallas_call(
        functools.partial(_peer_query_kernel, n_ctx // ROW_BLOCK),
        grid=(bsz, nblk),
        in_specs=[row_spec, full(mod), full(g), full(wq_bf16), full(keys_bf16)],
        out_specs=[pl.BlockSpec((EXPERT_ROWS, ROW_BLOCK, 128), lambda b, j: (0, b * nblk + j, 0)),
                   pick_spec, pick_spec],
        out_shape=[jax.ShapeDtypeStruct((EXPERT_ROWS, bsz * n_tok, 128), i32),
                   jax.ShapeDtypeStruct((n_pblk, PEER_PICKS, PEER_TOK), i32),
                   jax.ShapeDtypeStruct((n_pblk, PEER_PICKS, PEER_TOK), f32)],
        scratch_shapes=[pltpu.VMEM((2 * PEER_HEADS, ROW_BLOCK, PEER_HALF), jnp.bfloat16),
                        pltpu.VMEM((4 * PEER_TOPK, ROW_BLOCK), f32),
                        pltpu.VMEM((4 * PEER_TOPK, ROW_BLOCK), f32),
                        pltpu.VMEM((PEER_PICKS, ROW_BLOCK), f32),
                        pltpu.VMEM((PEER_PICKS, ROW_BLOCK), f32)],
        compiler_params=pltpu.CompilerParams(vmem_limit_bytes=VMEM_LIMIT),
        name="peer_query",
    )(h, mod, g, wq_bf16, keys_bf16)


def pack_rows(rows):
    bits = lax.bitcast_convert_type(rows.astype(jnp.bfloat16), jnp.uint16).astype(jnp.uint32)
    bits = bits.reshape(-1, EXPERT_ROWS, 2, 128)
    words = bits[:, :, 0, :] | (bits[:, :, 1, :] << 16)
    return lax.bitcast_convert_type(words, jnp.int32)


def _expert_chunks(idx_ref, tab_ref, t):
    row = idx_ref.at[0, 0, pl.ds(t * PEER_PICKS, PEER_PICKS)]
    for c in range(PEER_PICKS // PEER_CHUNK):
        tiles = [tab_ref[pl.ds(pl.multiple_of(row[c * PEER_CHUNK + j], EXPERT_ROWS),
                               EXPERT_ROWS), :] for j in range(PEER_CHUNK)]
        yield c, jnp.concatenate(tiles, axis=0)


def _group_chunks(idx_ref, tab_ref, r0):
    rows = [idx_ref.at[0, 0, pl.ds((r0 + a) * PEER_PICKS, PEER_PICKS)] for a in range(PEER_STAGE)]
    for c in range(PEER_PICKS // PEER_CHUNK):
        tiles = [[None] * PEER_CHUNK for _ in range(PEER_STAGE)]
        for j in range(PEER_CHUNK):
            for a in range(PEER_STAGE):
                off = pl.multiple_of(rows[a][c * PEER_CHUNK + j], EXPERT_ROWS)
                tiles[a][j] = tab_ref[pl.ds(off, EXPERT_ROWS), :]
        yield c, [jnp.concatenate(tiles[a], axis=0) for a in range(PEER_STAGE)]


def _pick_rows_mask():
    tile_rows = 2 * EXPERT_ROWS
    shape = (PEER_PICKS, PEER_PICKS * tile_rows)
    return (lax.broadcasted_iota(jnp.int32, shape, 1) // tile_rows
            == lax.broadcasted_iota(jnp.int32, shape, 0))


def _peer_score_kernel(idx_ref, x_ref, gate_ref, tab_ref, w_ref):
    f32, bf16, i32 = jnp.float32, jnp.bfloat16, jnp.int32
    sel = _pick_rows_mask().astype(bf16)
    lane = lax.broadcasted_iota(i32, (PEER_PICKS, PEER_TOK), 1)
    cols = PEER_CHUNK * 2 * EXPERT_ROWS

    def group_body(g, acc):
        r0 = pl.multiple_of(g * PEER_STAGE, PEER_STAGE)
        xg = x_ref[:, pl.ds(r0, PEER_STAGE), :]
        xts = []
        for a in range(PEER_STAGE):
            xw = jnp.concatenate([xg[r, a:a + 1, :] for r in range(EXPERT_ROWS)], axis=0)
            xts.append(pltpu.bitcast(jnp.concatenate([xw] * PEER_CHUNK, axis=0), bf16))
        parts = [[] for _ in range(PEER_STAGE)]
        for c, words in _group_chunks(idx_ref, tab_ref, r0):
            for a in range(PEER_STAGE):
                prod = pltpu.bitcast(words[a], bf16) * xts[a]
                parts[a].append(jnp.dot(sel[0:PEER_CHUNK, 0:cols], prod,
                                        preferred_element_type=f32))
        for a in range(PEER_STAGE):
            part = jnp.concatenate(parts[a], axis=0)
            acc = jnp.where(lane == r0 + a, jnp.sum(part, axis=1, keepdims=True), acc)
        return acc

    scores = lax.fori_loop(0, PEER_TOK // PEER_STAGE, group_body,
                           jnp.zeros((PEER_PICKS, PEER_TOK), f32))
    w = gate_ref[0] * jax.nn.gelu(scores)
    w_ref[...] = jnp.dot(w.T.astype(bf16), sel, preferred_element_type=f32)


def _peer_mix_kernel(idx_ref, wexp_ref, h_ref, gate_ref, tab_ref, o_ref):
    f32, bf16, i32 = jnp.float32, jnp.bfloat16, jnp.int32
    tile_rows = 2 * EXPERT_ROWS
    shape = (tile_rows, PEER_PICKS * tile_rows)
    diag = (lax.broadcasted_iota(i32, shape, 1) % tile_rows
            == lax.broadcasted_iota(i32, shape, 0))
    cols = PEER_CHUNK * tile_rows

    def group_body(g, carry):
        r0 = pl.multiple_of(g * PEER_STAGE, PEER_STAGE)
        tiles = []
        wmats = [jnp.where(diag, wexp_ref[pl.ds(r0 + a, 1), :], 0.0).astype(bf16)
                 for a in range(PEER_STAGE)]
        tiles = [jnp.zeros((tile_rows, 128), f32)] * PEER_STAGE
        for c, words in _group_chunks(idx_ref, tab_ref, r0):
            tiles = [tiles[a] + jnp.dot(wmats[a][:, c * cols:(c + 1) * cols],
                                        pltpu.bitcast(words[a], bf16),
                                        preferred_element_type=f32) for a in range(PEER_STAGE)]
        rows = jnp.concatenate(
            [jnp.concatenate([tile[r:r + 1, :] for tile in tiles], axis=0)
             for r in range(tile_rows)], axis=1)
        o_ref[pl.ds(r0, PEER_STAGE), :] = (h_ref[pl.ds(r0, PEER_STAGE), :]
                                           + gate_ref[0, 0] * rows)
        return carry

    lax.fori_loop(0, PEER_TOK // PEER_STAGE, group_body, 0)


def _table_spec(tab):
    return pl.BlockSpec(tab.shape, lambda i: (0,) * tab.ndim, pipeline_mode=pl.Buffered(1))


def peer_experts(h, xw, idx, gates, mod, u_packed, v_packed, n_ctx):
    bsz, n_tok, _ = h.shape
    n_pblk = idx.shape[0]
    blk_per_batch = n_tok // PEER_TOK
    ctx_pblk = n_ctx // PEER_TOK
    flat = PEER_PICKS * PEER_TOK
    smem_spec = pl.BlockSpec((1, 1, flat), lambda i: (i, 0, 0), memory_space=pltpu.SMEM)
    pick_spec = pl.BlockSpec((1, PEER_PICKS, PEER_TOK), lambda i: (i, 0, 0))
    word_spec = pl.BlockSpec((EXPERT_ROWS, PEER_TOK, 128), lambda i: (0, i, 0))
    row_spec = pl.BlockSpec((PEER_TOK, D_MODEL), lambda i: (i, 0))
    idx_flat = idx.reshape(n_pblk, 1, flat)
    wexp = pl.pallas_call(
        _peer_score_kernel,
        grid=(n_pblk,),
        in_specs=[smem_spec, word_spec, pick_spec, _table_spec(u_packed)],
        out_specs=row_spec,
        out_shape=jax.ShapeDtypeStruct((bsz * n_tok, D_MODEL), jnp.float32),
        compiler_params=pltpu.CompilerParams(vmem_limit_bytes=VMEM_LIMIT),
        name="peer_scores",
    )(idx_flat, xw, gates, u_packed)

    def gate_map(i):
        bi, ji = i // blk_per_batch, i % blk_per_batch
        return (jnp.where(ji < ctx_pblk, 0, bi + 1), N_MOD - 1, 0, 0)

    out = pl.pallas_call(
        _peer_mix_kernel,
        grid=(n_pblk,),
        in_specs=[smem_spec, row_spec, row_spec,
                  pl.BlockSpec((1, 1, 1, D_MODEL), gate_map), _table_spec(v_packed)],
        out_specs=row_spec,
        out_shape=jax.ShapeDtypeStruct((bsz * n_tok, D_MODEL), jnp.float32),
        compiler_params=pltpu.CompilerParams(vmem_limit_bytes=VMEM_LIMIT),
        name="peer_mix",
    )(idx_flat, wexp, h.reshape(-1, D_MODEL), mod.reshape(MOD_ROWS, N_MOD, 1, D_MODEL), v_packed)
    return out.reshape(h.shape)


def kernel(x, c, ctx, c_ctx, w_mod, b_mod, norm_mix_g, w_in, q_norm_g, k_norm_g, ssm_a_re, ssm_a_im, ssm_log_dt, ssm_b_re, ssm_b_im, ssm_c_re, ssm_c_im, ssm_d, w_glu, b_glu, w_branch_attn, w_branch_ssm, w_out, norm_ffn_g, peer_w_q, peer_keys, peer_u, peer_v):
    bf16 = jnp.bfloat16
    bsz, n_lat, _ = x.shape
    n_ctx = ctx.shape[1]
    depth = w_mod.shape[0]
    assert bsz + 1 <= MOD_ROWS and n_ctx % ROW_BLOCK == 0 and n_lat % ROW_BLOCK == 0

    cc = jnp.zeros((MOD_ROWS, D_MODEL), jnp.float32).at[0].set(c_ctx).at[1:bsz + 1].set(c)
    mods = modulation(cc, w_mod, b_mod)
    rope = rope_tables(n_ctx, n_lat)
    h = jnp.concatenate([ctx, x], axis=1)
    row = lambda a: a.reshape(1, -1)

    for i in range(depth):
        mod = mods[i]
        q, k, v, u, ga, gb = in_projection(h, mod, row(norm_mix_g[i]), w_in[i].astype(bf16), rope,
                                           row(q_norm_g[i]), row(k_norm_g[i]), n_ctx)
        attn = attention(q, k, v, n_ctx)
        y = s5_scan(u.astype(bf16), ssm_a_re[i], ssm_a_im[i], ssm_log_dt[i],
                    ssm_b_re[i], ssm_b_im[i], ssm_c_re[i], ssm_c_im[i], n_ctx)
        h = branch_merge(h, mod, attn, y, u, ga, gb, row(ssm_d[i]), w_glu[i].astype(bf16),
                         row(b_glu[i]), w_branch_attn[i].astype(bf16),
                         w_branch_ssm[i].astype(bf16), w_out[i].astype(bf16), n_ctx)
        xw, idx, gates = peer_query(h, mod, row(norm_ffn_g[i]), peer_w_q[i].astype(bf16),
                                    peer_keys[i].astype(bf16), n_ctx)
        h = peer_experts(h, xw, idx, gates, mod, pack_rows(peer_u[i]).reshape(-1, 128),
                         pack_rows(peer_v[i]).reshape(-1, 128), n_ctx)
    return h[:, n_ctx:, :]
```

```python
import functools
import math

import jax
import jax.numpy as jnp
from jax import lax
from jax.experimental import pallas as pl
from jax.experimental.pallas import tpu as pltpu

D_MODEL = 1024
GRID_W = 64
HEAD_DIM = 128
N_Q_HEADS = 8
N_KV_HEADS = 2
GQA_GROUP = N_Q_HEADS // N_KV_HEADS
ATTN_WIDTH = N_Q_HEADS * HEAD_DIM
KV_WIDTH = N_KV_HEADS * HEAD_DIM
ROPE_THETA = 10000.0
ROPE_FREQS = HEAD_DIM // 4
ATTN_SCALE = HEAD_DIM ** -0.5
SSM_WIDTH = D_MODEL // 2
SSM_GROUP = 16
SSM_GROUPS = SSM_WIDTH // SSM_GROUP
SSM_STATE = 64
IN_WIDTH = ATTN_WIDTH + 2 * KV_WIDTH + SSM_WIDTH + 2 * D_MODEL
PEER_HEADS = 8
PEER_N_KEYS = 128
PEER_N_EXPERTS = PEER_N_KEYS ** 2
PEER_QUERY_DIM = 256
PEER_HALF = PEER_QUERY_DIM // 2
PEER_TOPK = 16
PEER_PICKS = PEER_HEADS * PEER_TOPK
N_MOD = 6
EPS = 1e-6

ROW_BLOCK = 256
ATTN_KEYS = 4096
MOD_ROWS = 16
S5_CHUNK = 16
PEER_TOK = 128
EXPERT_ROWS = 4
PEER_STAGE = 64
PEER_CHUNK = 32
VMEM_LIMIT = 56 * 1024 * 1024

_NT = (((1,), (1,)), ((), ()))


def _mod_row(b, j, ctx_blocks):
    return jnp.where(j < ctx_blocks, 0, b + 1)


def _mod_kernel(c_ref, w_ref, b_ref, o_ref):
    c = c_ref[...]
    s = c * jax.nn.sigmoid(c)
    o_ref[0] = jnp.dot(s.astype(jnp.bfloat16), w_ref[0].astype(jnp.bfloat16),
                       preferred_element_type=jnp.float32) + b_ref[0]


def modulation(cc, w_mod, b_mod):
    depth = w_mod.shape[0]
    nblk = 1536
    width = N_MOD * D_MODEL
    return pl.pallas_call(
        _mod_kernel,
        grid=(depth, width // nblk),
        in_specs=[pl.BlockSpec((MOD_ROWS, D_MODEL), lambda l, n: (0, 0)),
                  pl.BlockSpec((1, D_MODEL, nblk), lambda l, n: (l, 0, n)),
                  pl.BlockSpec((1, 1, nblk), lambda l, n: (l, 0, n))],
        out_specs=pl.BlockSpec((1, MOD_ROWS, nblk), lambda l, n: (l, 0, n)),
        out_shape=jax.ShapeDtypeStruct((depth, MOD_ROWS, width), jnp.float32),
        name="modulation",
    )(cc, w_mod, b_mod.reshape(depth, 1, width))


def _inproj_kernel(ctx_blocks, h_ref, mod_ref, g_ref, w_ref, cos_ref, sa_ref, sb_ref,
                   qg_ref, kg_ref, q_ref, k_ref, v_ref, u_ref, ga_ref, gb_ref):
    b, j = pl.program_id(0), pl.program_id(1)
    row = _mod_row(b, j, ctx_blocks)
    shift = mod_ref[pl.ds(row, 1), 0:D_MODEL]
    scale = mod_ref[pl.ds(row, 1), D_MODEL:2 * D_MODEL]
    x = h_ref[0]
    xn = x * lax.rsqrt(jnp.mean(x * x, axis=-1, keepdims=True) + EPS) * g_ref[...]
    xm = (xn * (1.0 + scale) + shift).astype(jnp.bfloat16)

    cos, sa, sb = cos_ref[...], sa_ref[...], sb_ref[...]

    def norm_rope(z, g, out_scale):
        zn = z * lax.rsqrt(jnp.mean(z * z, axis=-1, keepdims=True) + EPS) * g
        r = zn * cos + pltpu.roll(zn, 96, 1) * sa + pltpu.roll(zn, 32, 1) * sb
        return r * out_scale

    for hd in range(N_Q_HEADS):
        lo = hd * HEAD_DIM
        z = jnp.dot(xm, w_ref[:, lo:lo + HEAD_DIM], preferred_element_type=jnp.float32)
        q_ref[0, :, lo:lo + HEAD_DIM] = norm_rope(z, qg_ref[...], ATTN_SCALE * math.log2(math.e)).astype(q_ref.dtype)
    for hd in range(N_KV_HEADS):
        lo = hd * HEAD_DIM
        z = jnp.dot(xm, w_ref[:, ATTN_WIDTH + lo:ATTN_WIDTH + lo + HEAD_DIM],
                    preferred_element_type=jnp.float32)
        k_ref[0, :, lo:lo + HEAD_DIM] = norm_rope(z, kg_ref[...], 1.0).astype(k_ref.dtype)
    off = ATTN_WIDTH + KV_WIDTH
    v_ref[0] = jnp.dot(xm, w_ref[:, off:off + KV_WIDTH],
                       preferred_element_type=jnp.float32).astype(v_ref.dtype)
    off += KV_WIDTH
    u_ref[0] = jnp.dot(xm, w_ref[:, off:off + SSM_WIDTH], preferred_element_type=jnp.float32)
    off += SSM_WIDTH
    ga_ref[0] = jnp.dot(xm, w_ref[:, off:off + D_MODEL], preferred_element_type=jnp.float32)
    off += D_MODEL
    gb_ref[0] = jnp.dot(xm, w_ref[:, off:off + D_MODEL], preferred_element_type=jnp.float32)


def in_projection(h, mod, g, w_in_bf16, rope, qg, kg, n_ctx):
    bsz, n_tok, _ = h.shape
    nblk = n_tok // ROW_BLOCK
    cos, sa, sb = rope
    row_spec = lambda w: pl.BlockSpec((1, ROW_BLOCK, w), lambda b, j: (b, j, 0))
    full = lambda a: pl.BlockSpec(a.shape, lambda b, j: (0,) * a.ndim)
    rope_spec = pl.BlockSpec((ROW_BLOCK, HEAD_DIM), lambda b, j: (j, 0))
    f32, bf16 = jnp.float32, jnp.bfloat16
    out_shape = [jax.ShapeDtypeStruct((bsz, n_tok, ATTN_WIDTH), bf16),
                 jax.ShapeDtypeStruct((bsz, n_tok, KV_WIDTH), bf16),
                 jax.ShapeDtypeStruct((bsz, n_tok, KV_WIDTH), bf16),
                 jax.ShapeDtypeStruct((bsz, n_tok, SSM_WIDTH), f32),
                 jax.ShapeDtypeStruct((bsz, n_tok, D_MODEL), f32),
                 jax.ShapeDtypeStruct((bsz, n_tok, D_MODEL), f32)]
    return pl.pallas_call(
        functools.partial(_inproj_kernel, n_ctx // ROW_BLOCK),
        grid=(bsz, nblk),
        in_specs=[row_spec(D_MODEL), full(mod), full(g), full(w_in_bf16),
                  rope_spec, rope_spec, rope_spec, full(qg), full(kg)],
        out_specs=[row_spec(ATTN_WIDTH), row_spec(KV_WIDTH), row_spec(KV_WIDTH),
                   row_spec(SSM_WIDTH), row_spec(D_MODEL), row_spec(D_MODEL)],
        out_shape=out_shape,
        compiler_params=pltpu.CompilerParams(vmem_limit_bytes=VMEM_LIMIT),
        name="in_projection",
    )(h, mod, g, w_in_bf16, cos, sa, sb, qg, kg)


def rope_tables(n_ctx, n_lat):
    pos = jnp.arange(n_lat)
    inv_freq = ROPE_THETA ** (-jnp.arange(ROPE_FREQS, dtype=jnp.float32) / ROPE_FREQS)
    ang_row = (pos // GRID_W).astype(jnp.float32)[:, None] * inv_freq
    ang_col = (pos % GRID_W).astype(jnp.float32)[:, None] * inv_freq
    ang = jnp.concatenate([ang_row, ang_row, ang_col, ang_col], axis=-1)
    cos, sin = jnp.cos(ang), jnp.sin(ang)
    even = ((jnp.arange(HEAD_DIM) // ROPE_FREQS) % 2 == 0)[None, :]
    sa = jnp.where(even, -sin, 0.0)
    sb = jnp.where(even, 0.0, sin)
    pad = lambda t, v: jnp.concatenate([jnp.full((n_ctx, HEAD_DIM), v, jnp.float32), t], axis=0)
    return pad(cos, 1.0), pad(sa, 0.0), pad(sb, 0.0)


def _attn_kernel(n_ctx, n_lat, key_chunk, ctx_blocks, q_ref, k_ref, v_ref, o_ref, m_scr, l_scr, acc_scr):
    f32 = jnp.float32
    j = pl.program_id(2)
    q = jnp.concatenate([q_ref[0, :, hd * HEAD_DIM:(hd + 1) * HEAD_DIM]
                         for hd in range(GQA_GROUP)], axis=0)

    kc, vc = k_ref[0, 0:n_ctx, :], v_ref[0, 0:n_ctx, :]
    s = lax.dot_general(q, kc, _NT, preferred_element_type=f32)
    m = jnp.max(s, axis=-1, keepdims=True)
    p = jnp.exp2(s - m)
    m_scr[...] = m
    l_scr[...] = jnp.sum(p, axis=-1, keepdims=True)
    acc_scr[...] = jnp.dot(p.astype(vc.dtype), vc, preferred_element_type=f32)

    @pl.when(j >= ctx_blocks)
    def _():
        def body(c, carry):
            r0 = pl.multiple_of(n_ctx + c * key_chunk, ROW_BLOCK)
            kc = k_ref[0, pl.ds(r0, key_chunk), :]
            vc = v_ref[0, pl.ds(r0, key_chunk), :]
            s = lax.dot_general(q, kc, _NT, preferred_element_type=f32)
            m_old = m_scr[...]
            m_new = jnp.maximum(m_old, jnp.max(s, axis=-1, keepdims=True))
            alpha = jnp.exp2(m_old - m_new)
            p = jnp.exp2(s - m_new)
            m_scr[...] = m_new
            l_scr[...] = alpha * l_scr[...] + jnp.sum(p, axis=-1, keepdims=True)
            acc_scr[...] = alpha * acc_scr[...] + jnp.dot(p.astype(vc.dtype), vc,
                                                          preferred_element_type=f32)
            return carry

        lax.fori_loop(0, n_lat // key_chunk, body, 0)

    out = acc_scr[...] / l_scr[...]
    for hd in range(GQA_GROUP):
        o_ref[0, :, hd * HEAD_DIM:(hd + 1) * HEAD_DIM] = (
            out[hd * ROW_BLOCK:(hd + 1) * ROW_BLOCK]).astype(o_ref.dtype)


def attention(q, k, v, n_ctx):
    bsz, n_tok, _ = q.shape
    nblk = n_tok // ROW_BLOCK
    gw = GQA_GROUP * HEAD_DIM
    rows = GQA_GROUP * ROW_BLOCK
    key_chunk = math.gcd(n_tok - n_ctx, ATTN_KEYS)
    return pl.pallas_call(
        functools.partial(_attn_kernel, n_ctx, n_tok - n_ctx, key_chunk, n_ctx // ROW_BLOCK),
        grid=(bsz, N_KV_HEADS, nblk),
        in_specs=[pl.BlockSpec((1, ROW_BLOCK, gw), lambda b, g, j: (b, j, g)),
                  pl.BlockSpec((1, n_tok, HEAD_DIM), lambda b, g, j: (b, 0, g)),
                  pl.BlockSpec((1, n_tok, HEAD_DIM), lambda b, g, j: (b, 0, g))],
        out_specs=pl.BlockSpec((1, ROW_BLOCK, gw), lambda b, g, j: (b, j, g)),
        out_shape=jax.ShapeDtypeStruct((bsz, n_tok, ATTN_WIDTH), jnp.bfloat16),
        scratch_shapes=[pltpu.VMEM((rows, 1), jnp.float32), pltpu.VMEM((rows, 1), jnp.float32),
                        pltpu.VMEM((rows, HEAD_DIM), jnp.float32)],
        compiler_params=pltpu.CompilerParams(vmem_limit_bytes=VMEM_LIMIT),
        name="attention",
    )(q, k, v)


def _cpow(e, lam_re_dt, lam_im_dt):
    mag = jnp.exp(e * lam_re_dt)
    ang = e * lam_im_dt
    return mag * jnp.cos(ang), mag * jnp.sin(ang)


def _s5_kernel(n_batch, ctx_chunks, n_chunks,
               u_ref, arow_ref, acol_ref, b_ref, c_ref, ct_ref, y_ref,
               sfr, sfi, sbr, sbi, hfr, hfi, hbr, hbi):
    f32, bf16 = jnp.float32, jnp.bfloat16
    width = S5_CHUNK * SSM_GROUP
    lane = lax.broadcasted_iota(jnp.int32, (1, width), 1)
    lane_blk = lane // SSM_GROUP
    expand = (lax.broadcasted_iota(jnp.int32, (SSM_GROUP, width), 1) % SSM_GROUP
              == lax.broadcasted_iota(jnp.int32, (SSM_GROUP, width), 0)).astype(f32)
    rowi = lax.broadcasted_iota(jnp.int32, (width, 1), 0) // SSM_GROUP
    u = u_ref[0].astype(bf16)

    m_tot = jnp.zeros((width, width), f32)
    proj, read, step = [], [], []
    for d in range(2):
        fwd = d == 0
        are_c, aim_c, dt_c = acol_ref[d, 0, 0], acol_ref[d, 0, 1], jnp.exp(acol_ref[d, 0, 2])
        lr_c, li_c = are_c * dt_c, aim_c * dt_c
        abr, abi = _cpow(1.0, lr_c, li_c)
        den = are_c * are_c + aim_c * aim_c
        nr, ni = abr - 1.0, abi
        k_r = (nr * are_c + ni * aim_c) / den
        k_i = (ni * are_c - nr * aim_c) / den
        b_re, b_im = b_ref[d, 0, 0], b_ref[d, 0, 1]
        bb_r = k_r * b_re - k_i * b_im
        bb_i = k_r * b_im + k_i * b_re
        bt_r = jnp.dot(bb_r, expand, preferred_element_type=f32)
        bt_i = jnp.dot(bb_i, expand, preferred_element_type=f32)
        e_in = (S5_CHUNK - 1 - lane_blk if fwd else lane_blk).astype(f32)
        ap_r, ap_i = _cpow(e_in, lr_c, li_c)
        proj.append(((ap_r * bt_r - ap_i * bt_i).astype(bf16),
                     (ap_r * bt_i + ap_i * bt_r).astype(bf16)))
        ct_r = jnp.dot(ct_ref[d, 0, 0], expand, preferred_element_type=f32)
        ct_i = jnp.dot(ct_ref[d, 0, 1], expand, preferred_element_type=f32)
        e_out = (lane_blk + 1 if fwd else S5_CHUNK - lane_blk).astype(f32)
        aq_r, aq_i = _cpow(e_out, lr_c, li_c)
        read.append(((ct_r * aq_r - ct_i * aq_i).astype(bf16),
                     (-(ct_r * aq_i + ct_i * aq_r)).astype(bf16)))
        are_r, aim_r, dt_r = arow_ref[d, 0, 0], arow_ref[d, 0, 1], jnp.exp(arow_ref[d, 0, 2])
        lr_r, li_r = are_r * dt_r, aim_r * dt_r
        step.append(_cpow(float(S5_CHUNK), lr_r, li_r))
        lag = (rowi if fwd else S5_CHUNK - 1 - rowi).astype(f32)
        al_r, al_i = _cpow(lag, lr_r, li_r)
        c_re = jnp.concatenate([c_ref[d, 0, 0]] * S5_CHUNK, axis=0)
        c_im = jnp.concatenate([c_ref[d, 0, 1]] * S5_CHUNK, axis=0)
        ca_r = c_re * al_r - c_im * al_i
        ca_i = c_re * al_i + c_im * al_r
        kmat = (jnp.dot(ca_r, bb_r, preferred_element_type=f32)
                - jnp.dot(ca_i, bb_i, preferred_element_type=f32))
        kt = jnp.dot(kmat, expand, preferred_element_type=f32)
        for s in range(S5_CHUNK):
            sh = (s if fwd else S5_CHUNK - 1 - s) * SSM_GROUP
            if sh == 0:
                shifted = kt
            elif fwd:
                shifted = jnp.concatenate([jnp.zeros((sh, width), f32), kt[:width - sh]], axis=0)
            else:
                shifted = jnp.concatenate([kt[sh:], jnp.zeros((sh, width), f32)], axis=0)
            m_tot = m_tot + jnp.where(lane_blk == s, shifted, 0.0)

    y_ref[0] = lax.dot_general(u, m_tot.astype(bf16), _NT, preferred_element_type=f32)
    sfr[...] = lax.dot_general(u, proj[0][0], _NT, preferred_element_type=f32)
    sfi[...] = lax.dot_general(u, proj[0][1], _NT, preferred_element_type=f32)
    sbr[...] = lax.dot_general(u, proj[1][0], _NT, preferred_element_type=f32)
    sbi[...] = lax.dot_general(u, proj[1][1], _NT, preferred_element_type=f32)

    (afr, afi), (abr_, abi_) = step

    def scan_body(i, carry):
        fr, fi, br, bi = carry
        rf = pl.multiple_of(i * n_batch, n_batch)
        hfr[pl.ds(rf, n_batch), :] = fr
        hfi[pl.ds(rf, n_batch), :] = fi
        nfr = afr * fr - afi * fi + sfr[pl.ds(rf, n_batch), :]
        nfi = afr * fi + afi * fr + sfi[pl.ds(rf, n_batch), :]
        cb = jnp.where(i < ctx_chunks, ctx_chunks - 1 - i, n_chunks - 1 + ctx_chunks - i)
        rb = pl.multiple_of(cb * n_batch, n_batch)
        hbr[pl.ds(rb, n_batch), :] = br
        hbi[pl.ds(rb, n_batch), :] = bi
        nbr = abr_ * br - abi_ * bi + sbr[pl.ds(rb, n_batch), :]
        nbi = abr_ * bi + abi_ * br + sbi[pl.ds(rb, n_batch), :]
        return nfr, nfi, nbr, nbi

    z = jnp.zeros((n_batch, SSM_STATE), f32)
    lax.fori_loop(0, n_chunks, scan_body, (z, z, z, z))

    y_ref[0] += (jnp.dot(hfr[...].astype(bf16), read[0][0], preferred_element_type=f32)
                 + jnp.dot(hfi[...].astype(bf16), read[0][1], preferred_element_type=f32)
                 + jnp.dot(hbr[...].astype(bf16), read[1][0], preferred_element_type=f32)
                 + jnp.dot(hbi[...].astype(bf16), read[1][1], preferred_element_type=f32))


def s5_scan(u, a_re, a_im, log_dt, b_re, b_im, c_re, c_im, n_ctx):
    bsz, n_tok, _ = u.shape
    n_chunks = n_tok // S5_CHUNK
    rows = n_chunks * bsz
    width = S5_CHUNK * SSM_GROUP
    ug = u.reshape(bsz, n_chunks, S5_CHUNK, SSM_GROUPS, SSM_GROUP)
    ug = ug.transpose(3, 1, 0, 2, 4).reshape(SSM_GROUPS, rows, width)
    ldt = jnp.broadcast_to(log_dt[..., None], a_re.shape)
    arow = jnp.stack([a_re, a_im, ldt], axis=2)[:, :, :, None, :]
    acol = jnp.stack([a_re, a_im, ldt], axis=2)[..., None]
    bmat = jnp.stack([b_re, b_im], axis=2)
    cmat = jnp.stack([c_re, c_im], axis=2)
    ctmat = jnp.swapaxes(cmat, -1, -2)
    grp = lambda a: pl.BlockSpec((2, 1) + a.shape[2:], lambda g: (0, g) + (0,) * (a.ndim - 2))
    state = pltpu.VMEM((rows, SSM_STATE), jnp.float32)
    y = pl.pallas_call(
        functools.partial(_s5_kernel, bsz, n_ctx // S5_CHUNK, n_chunks),
        grid=(SSM_GROUPS,),
        in_specs=[pl.BlockSpec((1, rows, width), lambda g: (g, 0, 0)),
                  grp(arow), grp(acol), grp(bmat), grp(cmat), grp(ctmat)],
        out_specs=pl.BlockSpec((1, rows, width), lambda g: (g, 0, 0)),
        out_shape=jax.ShapeDtypeStruct((SSM_GROUPS, rows, width), jnp.float32),
        scratch_shapes=[state] * 8,
        compiler_params=pltpu.CompilerParams(vmem_limit_bytes=VMEM_LIMIT),
        name="s5_scan",
    )(ug, arow, acol, bmat, cmat, ctmat)
    y = y.reshape(SSM_GROUPS, n_chunks, bsz, S5_CHUNK, SSM_GROUP)
    return y.transpose(2, 1, 3, 0, 4).reshape(bsz, n_tok, SSM_WIDTH)


def _merge_kernel(ctx_blocks, h_ref, mod_ref, attn_ref, y_ref, u_ref, ga_ref, gb_ref,
                  dskip_ref, wglu_ref, bglu_ref, wba_ref, wbs_ref, wo_ref, o_ref):
    f32, bf16 = jnp.float32, jnp.bfloat16
    b, j = pl.program_id(0), pl.program_id(1)
    row = _mod_row(b, j, ctx_blocks)
    gate = mod_ref[pl.ds(row, 1), 2 * D_MODEL:3 * D_MODEL]
    y = jax.nn.gelu(y_ref[0] + dskip_ref[...] * u_ref[0])
    glu = jnp.dot(y.astype(bf16), wglu_ref[...], preferred_element_type=f32) + bglu_ref[...]
    ssm = y * jax.nn.sigmoid(glu)
    ba = jnp.dot(attn_ref[0], wba_ref[...], preferred_element_type=f32)
    bs = jnp.dot(ssm.astype(bf16), wbs_ref[...], preferred_element_type=f32)
    mix = jax.nn.sigmoid(ga_ref[0]) * ba + jax.nn.sigmoid(gb_ref[0]) * bs
    out = jnp.dot(mix.astype(bf16), wo_ref[...], preferred_element_type=f32)
    o_ref[0] = h_ref[0] + gate * out


def branch_merge(h, mod, attn, y, u, ga, gb, d_skip, w_glu, b_glu, w_ba, w_bs, w_o, n_ctx):
    bsz, n_tok, _ = h.shape
    nblk = n_tok // ROW_BLOCK
    row_spec = lambda w: pl.BlockSpec((1, ROW_BLOCK, w), lambda b, j: (b, j, 0))
    full = lambda a: pl.BlockSpec(a.shape, lambda b, j: (0,) * a.ndim)
    return pl.pallas_call(
        functools.partial(_merge_kernel, n_ctx // ROW_BLOCK),
        grid=(bsz, nblk),
        in_specs=[row_spec(D_MODEL), full(mod), row_spec(ATTN_WIDTH), row_spec(SSM_WIDTH),
                  row_spec(SSM_WIDTH), row_spec(D_MODEL), row_spec(D_MODEL),
                  full(d_skip), full(w_glu), full(b_glu), full(w_ba), full(w_bs), full(w_o)],
        out_specs=row_spec(D_MODEL),
        out_shape=jax.ShapeDtypeStruct(h.shape, jnp.float32),
        compiler_params=pltpu.CompilerParams(vmem_limit_bytes=VMEM_LIMIT),
        name="branch_merge",
    )(h, mod, attn, y, u, ga, gb, d_skip, w_glu, b_glu, w_ba, w_bs, w_o)


def _top16(jobs, vals_ref, pay_ref):
    rows = [lax.broadcasted_iota(jnp.int32, s.shape, 0).astype(jnp.float32) for s, _, _ in jobs]

    def body(r, carry):
        nxt = []
        for s, (s0, payload, base), row in zip(carry, jobs, rows):
            m = jnp.max(s, axis=0, keepdims=True)
            pos = jnp.min(jnp.where(s == m, row, float(s0.shape[0])), axis=0, keepdims=True)
            hit = row == pos
            vals_ref[pl.ds(base + r, 1), :] = m
            if payload is None:
                pay_ref[pl.ds(base + r, 1), :] = pos
            else:
                pay_ref[pl.ds(base + r, 1), :] = jnp.max(jnp.where(hit, payload, -1.0), axis=0,
                                                         keepdims=True)
            nxt.append(jnp.where(hit, -jnp.inf, s))
        return tuple(nxt)

    lax.fori_loop(0, PEER_TOPK, body, tuple(s for s, _, _ in jobs))


def _peer_query_kernel(ctx_blocks, h_ref, mod_ref, g_ref, wq_ref, keys_ref,
                       xw_ref, idx_ref, gate_ref, q_scr, sv, si, bv, be):
    f32, bf16 = jnp.float32, jnp.bfloat16
    b, j = pl.program_id(0), pl.program_id(1)
    row = _mod_row(b, j, ctx_blocks)
    shift = mod_ref[pl.ds(row, 1), 3 * D_MODEL:4 * D_MODEL]
    scale = mod_ref[pl.ds(row, 1), 4 * D_MODEL:5 * D_MODEL]
    x = h_ref[0]
    xn = x * lax.rsqrt(jnp.mean(x * x, axis=-1, keepdims=True) + EPS) * g_ref[...]
    xm = xn * (1.0 + scale) + shift
    xb = xm.astype(bf16)
    bits = lax.bitcast_convert_type(xb.astype(f32), jnp.int32)
    for r in range(EXPERT_ROWS):
        lo = (bits[:, (2 * r) * 128:(2 * r + 1) * 128] >> 16) & 0xFFFF
        hi = bits[:, (2 * r + 1) * 128:(2 * r + 2) * 128] & jnp.int32(-65536)
        xw_ref[r] = lo | hi
    for hp in range(2 * PEER_HEADS):
        lo = hp * PEER_HALF
        q_scr[hp] = jnp.dot(xb, wq_ref[:, lo:lo + PEER_HALF],
                            preferred_element_type=f32).astype(bf16)

    half = PEER_TOPK // 2

    def candidates(s1, s2, i1, i2):
        cand, cexp = [], []
        for i in range(half):
            size = PEER_TOPK if i == 0 else half
            blk = s1[i:i + 1, :] + s2[0:size, :]
            keep = PEER_TOPK // (i + 1)
            if keep < size:
                jrow = lax.broadcasted_iota(jnp.int32, (size, 1), 0)
                blk = jnp.where(jrow < keep, blk, -jnp.inf)
            cand.append(blk)
            cexp.append(i1[i:i + 1, :] * PEER_N_KEYS + i2[0:size, :])
        cand.append(s1[half:PEER_TOPK, :] + s2[0:1, :])
        cexp.append(i1[half:PEER_TOPK, :] * PEER_N_KEYS + i2[0:1, :])
        return jnp.concatenate(cand, axis=0), jnp.concatenate(cexp, axis=0)

    def head_pair_body(hp, carry):
        for d in range(2):
            _top16([(lax.dot_general(keys_ref[p], q_scr[4 * hp + 2 * d + p], _NT,
                                     preferred_element_type=f32),
                     None, (2 * d + p) * PEER_TOPK) for p in range(2)], sv, si)
        jobs = []
        for d in range(2):
            lo = 2 * d * PEER_TOPK
            s1, s2 = sv[lo:lo + PEER_TOPK, :], sv[lo + PEER_TOPK:lo + 2 * PEER_TOPK, :]
            i1, i2 = si[lo:lo + PEER_TOPK, :], si[lo + PEER_TOPK:lo + 2 * PEER_TOPK, :]
            cand, cexp = candidates(s1, s2, i1, i2)
            jobs.append((cand, cexp, pl.multiple_of((2 * hp + d) * PEER_TOPK, PEER_TOPK)))
        _top16(jobs, bv, be)
        return carry

    lax.fori_loop(0, PEER_HEADS // 2, head_pair_body, 0)

    for hd in range(PEER_HEADS):
        best = bv[hd * PEER_TOPK:(hd + 1) * PEER_TOPK, :]
        e = jnp.exp(best - jnp.max(best, axis=0, keepdims=True))
        bv[hd * PEER_TOPK:(hd + 1) * PEER_TOPK, :] = e / jnp.sum(e, axis=0, keepdims=True)
    for t in range(ROW_BLOCK // PEER_TOK):
        idx_ref[t] = (be[:, t * PEER_TOK:(t + 1) * PEER_TOK].T * EXPERT_ROWS).astype(jnp.int32)
        gate_ref[t] = bv[:, t * PEER_TOK:(t + 1) * PEER_TOK]


def peer_query(h, mod, g, wq_bf16, keys_bf16, n_ctx):
    bsz, n_tok, _ = h.shape
    nblk = n_tok // ROW_BLOCK
    sub = ROW_BLOCK // PEER_TOK
    n_pblk = bsz * n_tok // PEER_TOK
    row_spec = pl.BlockSpec((1, ROW_BLOCK, D_MODEL), lambda b, j: (b, j, 0))
    full = lambda a: pl.BlockSpec(a.shape, lambda b, j: (0,) * a.ndim)
    pick_spec = pl.BlockSpec((sub, PEER_PICKS, PEER_TOK), lambda b, j: (b * nblk + j, 0, 0))
    f32, i32 = jnp.float32, jnp.int32
    return pl.pallas_call(
        functools.partial(_peer_query_kernel, n_ctx // ROW_BLOCK),
        grid=(bsz, nblk),
        in_specs=[row_spec, full(mod), full(g), full(wq_bf16), full(keys_bf16)],
        out_specs=[pl.BlockSpec((EXPERT_ROWS, ROW_BLOCK, 128), lambda b, j: (0, b * nblk + j, 0)),
                   pick_spec, pick_spec],
        out_shape=[jax.ShapeDtypeStruct((EXPERT_ROWS, bsz * n_tok, 128), i32),
                   jax.ShapeDtypeStruct((n_pblk, PEER_PICKS, PEER_TOK), i32),
                   jax.ShapeDtypeStruct((n_pblk, PEER_PICKS, PEER_TOK), f32)],
        scratch_shapes=[pltpu.VMEM((2 * PEER_HEADS, ROW_BLOCK, PEER_HALF), jnp.bfloat16),
                        pltpu.VMEM((4 * PEER_TOPK, ROW_BLOCK), f32),
                        pltpu.VMEM((4 * PEER_TOPK, ROW_BLOCK), f32),
                        pltpu.VMEM((PEER_PICKS, ROW_BLOCK), f32),
                        pltpu.VMEM((PEER_PICKS, ROW_BLOCK), f32)],
        compiler_params=pltpu.CompilerParams(vmem_limit_bytes=VMEM_LIMIT),
        name="peer_query",
    )(h, mod, g, wq_bf16, keys_bf16)


def pack_rows(rows):
    bits = lax.bitcast_convert_type(rows.astype(jnp.bfloat16), jnp.uint16).astype(jnp.uint32)
    bits = bits.reshape(-1, EXPERT_ROWS, 2, 128)
    words = bits[:, :, 0, :] | (bits[:, :, 1, :] << 16)
    return lax.bitcast_convert_type(words, jnp.int32)


def _expert_chunks(idx_ref, tab_ref, t):
    row = idx_ref.at[0, 0, pl.ds(t * PEER_PICKS, PEER_PICKS)]
    for c in range(PEER_PICKS // PEER_CHUNK):
        tiles = [tab_ref[pl.ds(pl.multiple_of(row[c * PEER_CHUNK + j], EXPERT_ROWS),
                               EXPERT_ROWS), :] for j in range(PEER_CHUNK)]
        yield c, jnp.concatenate(tiles, axis=0)


def _group_chunks(idx_ref, tab_ref, r0):
    rows = [idx_ref.at[0, 0, pl.ds((r0 + a) * PEER_PICKS, PEER_PICKS)] for a in range(PEER_STAGE)]
    for c in range(PEER_PICKS // PEER_CHUNK):
        tiles = [[None] * PEER_CHUNK for _ in range(PEER_STAGE)]
        for j in range(PEER_CHUNK):
            for a in range(PEER_STAGE):
                off = pl.multiple_of(rows[a][c * PEER_CHUNK + j], EXPERT_ROWS)
                tiles[a][j] = tab_ref[pl.ds(off, EXPERT_ROWS), :]
        yield c, [jnp.concatenate(tiles[a], axis=0) for a in range(PEER_STAGE)]


def _pick_rows_mask():
    tile_rows = 2 * EXPERT_ROWS
    shape = (PEER_PICKS, PEER_PICKS * tile_rows)
    return (lax.broadcasted_iota(jnp.int32, shape, 1) // tile_rows
            == lax.broadcasted_iota(jnp.int32, shape, 0))


def _peer_score_kernel(idx_ref, x_ref, gate_ref, tab_ref, w_ref):
    f32, bf16, i32 = jnp.float32, jnp.bfloat16, jnp.int32
    sel = _pick_rows_mask().astype(bf16)
    lane = lax.broadcasted_iota(i32, (PEER_PICKS, PEER_TOK), 1)
    cols = PEER_CHUNK * 2 * EXPERT_ROWS

    def group_body(g, acc):
        r0 = pl.multiple_of(g * PEER_STAGE, PEER_STAGE)
        xg = x_ref[:, pl.ds(r0, PEER_STAGE), :]
        xts = []
        for a in range(PEER_STAGE):
            xw = jnp.concatenate([xg[r, a:a + 1, :] for r in range(EXPERT_ROWS)], axis=0)
            xts.append(pltpu.bitcast(jnp.concatenate([xw] * PEER_CHUNK, axis=0), bf16))
        parts = [[] for _ in range(PEER_STAGE)]
        for c, words in _group_chunks(idx_ref, tab_ref, r0):
            for a in range(PEER_STAGE):
                prod = pltpu.bitcast(words[a], bf16) * xts[a]
                parts[a].append(jnp.dot(sel[0:PEER_CHUNK, 0:cols], prod,
                                        preferred_element_type=f32))
        for a in range(PEER_STAGE):
            part = jnp.concatenate(parts[a], axis=0)
            acc = jnp.where(lane == r0 + a, jnp.sum(part, axis=1, keepdims=True), acc)
        return acc

    scores = lax.fori_loop(0, PEER_TOK // PEER_STAGE, group_body,
                           jnp.zeros((PEER_PICKS, PEER_TOK), f32))
    w = gate_ref[0] * jax.nn.gelu(scores)
    w_ref[...] = jnp.dot(w.T.astype(bf16), sel, preferred_element_type=f32)


def _peer_mix_kernel(idx_ref, wexp_ref, h_ref, gate_ref, tab_ref, o_ref):
    f32, bf16, i32 = jnp.float32, jnp.bfloat16, jnp.int32
    tile_rows = 2 * EXPERT_ROWS
    shape = (tile_rows, PEER_PICKS * tile_rows)
    diag = (lax.broadcasted_iota(i32, shape, 1) % tile_rows
            == lax.broadcasted_iota(i32, shape, 0))
    cols = PEER_CHUNK * tile_rows

    def group_body(g, carry):
        r0 = pl.multiple_of(g * PEER_STAGE, PEER_STAGE)
        tiles = []
        wmats = [jnp.where(diag, wexp_ref[pl.ds(r0 + a, 1), :], 0.0).astype(bf16)
                 for a in range(PEER_STAGE)]
        tiles = [jnp.zeros((tile_rows, 128), f32)] * PEER_STAGE
        for c, words in _group_chunks(idx_ref, tab_ref, r0):
            tiles = [tiles[a] + jnp.dot(wmats[a][:, c * cols:(c + 1) * cols],
                                        pltpu.bitcast(words[a], bf16),
                                        preferred_element_type=f32) for a in range(PEER_STAGE)]
        rows = jnp.concatenate(
            [jnp.concatenate([tile[r:r + 1, :] for tile in tiles], axis=0)
             for r in range(tile_rows)], axis=1)
        o_ref[pl.ds(r0, PEER_STAGE), :] = (h_ref[pl.ds(r0, PEER_STAGE), :]
                                           + gate_ref[0, 0] * rows)
        return carry

    lax.fori_loop(0, PEER_TOK // PEER_STAGE, group_body, 0)


def _table_spec(tab):
    return pl.BlockSpec(tab.shape, lambda i: (0,) * tab.ndim, pipeline_mode=pl.Buffered(1))


def peer_experts(h, xw, idx, gates, mod, u_packed, v_packed, n_ctx):
    bsz, n_tok, _ = h.shape
    n_pblk = idx.shape[0]
    blk_per_batch = n_tok // PEER_TOK
    ctx_pblk = n_ctx // PEER_TOK
    flat = PEER_PICKS * PEER_TOK
    smem_spec = pl.BlockSpec((1, 1, flat), lambda i: (i, 0, 0), memory_space=pltpu.SMEM)
    pick_spec = pl.BlockSpec((1, PEER_PICKS, PEER_TOK), lambda i: (i, 0, 0))
    word_spec = pl.BlockSpec((EXPERT_ROWS, PEER_TOK, 128), lambda i: (0, i, 0))
    row_spec = pl.BlockSpec((PEER_TOK, D_MODEL), lambda i: (i, 0))
    idx_flat = idx.reshape(n_pblk, 1, flat)
    wexp = pl.pallas_call(
        _peer_score_kernel,
        grid=(n_pblk,),
        in_specs=[smem_spec, word_spec, pick_spec, _table_spec(u_packed)],
        out_specs=row_spec,
        out_shape=jax.ShapeDtypeStruct((bsz * n_tok, D_MODEL), jnp.float32),
        compiler_params=pltpu.CompilerParams(vmem_limit_bytes=VMEM_LIMIT),
        name="peer_scores",
    )(idx_flat, xw, gates, u_packed)

    def gate_map(i):
        bi, ji = i // blk_per_batch, i % blk_per_batch
        return (jnp.where(ji < ctx_pblk, 0, bi + 1), N_MOD - 1, 0, 0)

    out = pl.pallas_call(
        _peer_mix_kernel,
        grid=(n_pblk,),
        in_specs=[smem_spec, row_spec, row_spec,
                  pl.BlockSpec((1, 1, 1, D_MODEL), gate_map), _table_spec(v_packed)],
        out_specs=row_spec,
        out_shape=jax.ShapeDtypeStruct((bsz * n_tok, D_MODEL), jnp.float32),
        compiler_params=pltpu.CompilerParams(vmem_limit_bytes=VMEM_LIMIT),
        name="peer_mix",
    )(idx_flat, wexp, h.reshape(-1, D_MODEL), mod.reshape(MOD_ROWS, N_MOD, 1, D_MODEL), v_packed)
    return out.reshape(h.shape)


def kernel(x, c, ctx, c_ctx, w_mod, b_mod, norm_mix_g, w_in, q_norm_g, k_norm_g, ssm_a_re, ssm_a_im, ssm_log_dt, ssm_b_re, ssm_b_im, ssm_c_re, ssm_c_im, ssm_d, w_glu, b_glu, w_branch_attn, w_branch_ssm, w_out, norm_ffn_g, peer_w_q, peer_keys, peer_u, peer_v):
    bf16 = jnp.bfloat16
    bsz, n_lat, _ = x.shape
    n_ctx = ctx.shape[1]
    depth = w_mod.shape[0]
    assert bsz + 1 <= MOD_ROWS and n_ctx % ROW_BLOCK == 0 and n_lat % ROW_BLOCK == 0

    cc = jnp.zeros((MOD_ROWS, D_MODEL), jnp.float32).at[0].set(c_ctx).at[1:bsz + 1].set(c)
    mods = modulation(cc, w_mod, b_mod)
    rope = rope_tables(n_ctx, n_lat)
    h = jnp.concatenate([ctx, x], axis=1)
    row = lambda a: a.reshape(1, -1)

    for i in range(depth):
        mod = mods[i]
        q, k, v, u, ga, gb = in_projection(h, mod, row(norm_mix_g[i]), w_in[i].astype(bf16), rope,
                                           row(q_norm_g[i]), row(k_norm_g[i]), n_ctx)
        attn = attention(q, k, v, n_ctx)
        y = s5_scan(u.astype(bf16), ssm_a_re[i], ssm_a_im[i], ssm_log_dt[i],
                    ssm_b_re[i], ssm_b_im[i], ssm_c_re[i], ssm_c_im[i], n_ctx)
        h = branch_merge(h, mod, attn, y, u, ga, gb, row(ssm_d[i]), w_glu[i].astype(bf16),
                         row(b_glu[i]), w_branch_attn[i].astype(bf16),
                         w_branch_ssm[i].astype(bf16), w_out[i].astype(bf16), n_ctx)
        n_peer_ctx = n_ctx
        if i == depth - 1:
            h, n_peer_ctx = h[:, n_ctx:, :], 0
        xw, idx, gates = peer_query(h, mod, row(norm_ffn_g[i]), peer_w_q[i].astype(bf16),
                                    peer_keys[i].astype(bf16), n_peer_ctx)
        h = peer_experts(h, xw, idx, gates, mod, pack_rows(peer_u[i]).reshape(-1, 128),
                         pack_rows(peer_v[i]).reshape(-1, 128), n_peer_ctx)
    return h
```

```python
import functools
import math

import jax
import jax.numpy as jnp
from jax import lax
from jax.experimental import pallas as pl
from jax.experimental.pallas import tpu as pltpu

D_MODEL = 1024
GRID_W = 64
HEAD_DIM = 128
N_Q_HEADS = 8
N_KV_HEADS = 2
GQA_GROUP = N_Q_HEADS // N_KV_HEADS
ATTN_WIDTH = N_Q_HEADS * HEAD_DIM
KV_WIDTH = N_KV_HEADS * HEAD_DIM
ROPE_THETA = 10000.0
ROPE_FREQS = HEAD_DIM // 4
ATTN_SCALE = HEAD_DIM ** -0.5
SSM_WIDTH = D_MODEL // 2
SSM_GROUP = 16
SSM_GROUPS = SSM_WIDTH // SSM_GROUP
SSM_STATE = 64
IN_WIDTH = ATTN_WIDTH + 2 * KV_WIDTH + SSM_WIDTH + 2 * D_MODEL
PEER_HEADS = 8
PEER_N_KEYS = 128
PEER_N_EXPERTS = PEER_N_KEYS ** 2
PEER_QUERY_DIM = 256
PEER_HALF = PEER_QUERY_DIM // 2
PEER_TOPK = 16
PEER_PICKS = PEER_HEADS * PEER_TOPK
N_MOD = 6
EPS = 1e-6

ROW_BLOCK = 256
ATTN_KEYS = 4096
MOD_ROWS = 16
S5_CHUNK = 16
PEER_TOK = 128
EXPERT_ROWS = 4
PEER_STAGE = 64
PEER_CHUNK = 16
VMEM_LIMIT = 56 * 1024 * 1024

_NT = (((1,), (1,)), ((), ()))


def _mod_row(b, j, ctx_blocks):
    return jnp.where(j < ctx_blocks, 0, b + 1)


def _mod_kernel(c_ref, w_ref, b_ref, o_ref):
    c = c_ref[...]
    s = c * jax.nn.sigmoid(c)
    o_ref[0] = jnp.dot(s.astype(jnp.bfloat16), w_ref[0].astype(jnp.bfloat16),
                       preferred_element_type=jnp.float32) + b_ref[0]


def modulation(cc, w_mod, b_mod):
    depth = w_mod.shape[0]
    nblk = 1536
    width = N_MOD * D_MODEL
    return pl.pallas_call(
        _mod_kernel,
        grid=(depth, width // nblk),
        in_specs=[pl.BlockSpec((MOD_ROWS, D_MODEL), lambda l, n: (0, 0)),
                  pl.BlockSpec((1, D_MODEL, nblk), lambda l, n: (l, 0, n)),
                  pl.BlockSpec((1, 1, nblk), lambda l, n: (l, 0, n))],
        out_specs=pl.BlockSpec((1, MOD_ROWS, nblk), lambda l, n: (l, 0, n)),
        out_shape=jax.ShapeDtypeStruct((depth, MOD_ROWS, width), jnp.float32),
        name="modulation",
    )(cc, w_mod, b_mod.reshape(depth, 1, width))


def _inproj_kernel(ctx_blocks, h_ref, mod_ref, g_ref, w_ref, cos_ref, sa_ref, sb_ref,
                   qg_ref, kg_ref, q_ref, k_ref, v_ref, u_ref, ga_ref, gb_ref):
    b, j = pl.program_id(0), pl.program_id(1)
    row = _mod_row(b, j, ctx_blocks)
    shift = mod_ref[pl.ds(row, 1), 0:D_MODEL]
    scale = mod_ref[pl.ds(row, 1), D_MODEL:2 * D_MODEL]
    x = h_ref[0]
    xn = x * lax.rsqrt(jnp.mean(x * x, axis=-1, keepdims=True) + EPS) * g_ref[...]
    xm = (xn * (1.0 + scale) + shift).astype(jnp.bfloat16)

    cos, sa, sb = cos_ref[...], sa_ref[...], sb_ref[...]

    def norm_rope(z, g, out_scale):
        zn = z * lax.rsqrt(jnp.mean(z * z, axis=-1, keepdims=True) + EPS) * g
        r = zn * cos + pltpu.roll(zn, 96, 1) * sa + pltpu.roll(zn, 32, 1) * sb
        return r * out_scale

    for hd in range(N_Q_HEADS):
        lo = hd * HEAD_DIM
        z = jnp.dot(xm, w_ref[:, lo:lo + HEAD_DIM], preferred_element_type=jnp.float32)
        q_ref[0, :, lo:lo + HEAD_DIM] = norm_rope(z, qg_ref[...], ATTN_SCALE * math.log2(math.e)).astype(q_ref.dtype)
    for hd in range(N_KV_HEADS):
        lo = hd * HEAD_DIM
        z = jnp.dot(xm, w_ref[:, ATTN_WIDTH + lo:ATTN_WIDTH + lo + HEAD_DIM],
                    preferred_element_type=jnp.float32)
        k_ref[0, :, lo:lo + HEAD_DIM] = norm_rope(z, kg_ref[...], 1.0).astype(k_ref.dtype)
    off = ATTN_WIDTH + KV_WIDTH
    v_ref[0] = jnp.dot(xm, w_ref[:, off:off + KV_WIDTH],
                       preferred_element_type=jnp.float32).astype(v_ref.dtype)
    off += KV_WIDTH
    u_ref[0] = jnp.dot(xm, w_ref[:, off:off + SSM_WIDTH], preferred_element_type=jnp.float32)
    off += SSM_WIDTH
    ga_ref[0] = jnp.dot(xm, w_ref[:, off:off + D_MODEL], preferred_element_type=jnp.float32)
    off += D_MODEL
    gb_ref[0] = jnp.dot(xm, w_ref[:, off:off + D_MODEL], preferred_element_type=jnp.float32)


def in_projection(h, mod, g, w_in_bf16, rope, qg, kg, n_ctx):
    bsz, n_tok, _ = h.shape
    nblk = n_tok // ROW_BLOCK
    cos, sa, sb = rope
    row_spec = lambda w: pl.BlockSpec((1, ROW_BLOCK, w), lambda b, j: (b, j, 0))
    full = lambda a: pl.BlockSpec(a.shape, lambda b, j: (0,) * a.ndim)
    rope_spec = pl.BlockSpec((ROW_BLOCK, HEAD_DIM), lambda b, j: (j, 0))
    f32, bf16 = jnp.float32, jnp.bfloat16
    out_shape = [jax.ShapeDtypeStruct((bsz, n_tok, ATTN_WIDTH), bf16),
                 jax.ShapeDtypeStruct((bsz, n_tok, KV_WIDTH), bf16),
                 jax.ShapeDtypeStruct((bsz, n_tok, KV_WIDTH), bf16),
                 jax.ShapeDtypeStruct((bsz, n_tok, SSM_WIDTH), f32),
                 jax.ShapeDtypeStruct((bsz, n_tok, D_MODEL), f32),
                 jax.ShapeDtypeStruct((bsz, n_tok, D_MODEL), f32)]
    return pl.pallas_call(
        functools.partial(_inproj_kernel, n_ctx // ROW_BLOCK),
        grid=(bsz, nblk),
        in_specs=[row_spec(D_MODEL), full(mod), full(g), full(w_in_bf16),
                  rope_spec, rope_spec, rope_spec, full(qg), full(kg)],
        out_specs=[row_spec(ATTN_WIDTH), row_spec(KV_WIDTH), row_spec(KV_WIDTH),
                   row_spec(SSM_WIDTH), row_spec(D_MODEL), row_spec(D_MODEL)],
        out_shape=out_shape,
        compiler_params=pltpu.CompilerParams(vmem_limit_bytes=VMEM_LIMIT),
        name="in_projection",
    )(h, mod, g, w_in_bf16, cos, sa, sb, qg, kg)


def rope_tables(n_ctx, n_lat):
    pos = jnp.arange(n_lat)
    inv_freq = ROPE_THETA ** (-jnp.arange(ROPE_FREQS, dtype=jnp.float32) / ROPE_FREQS)
    ang_row = (pos // GRID_W).astype(jnp.float32)[:, None] * inv_freq
    ang_col = (pos % GRID_W).astype(jnp.float32)[:, None] * inv_freq
    ang = jnp.concatenate([ang_row, ang_row, ang_col, ang_col], axis=-1)
    cos, sin = jnp.cos(ang), jnp.sin(ang)
    even = ((jnp.arange(HEAD_DIM) // ROPE_FREQS) % 2 == 0)[None, :]
    sa = jnp.where(even, -sin, 0.0)
    sb = jnp.where(even, 0.0, sin)
    pad = lambda t, v: jnp.concatenate([jnp.full((n_ctx, HEAD_DIM), v, jnp.float32), t], axis=0)
    return pad(cos, 1.0), pad(sa, 0.0), pad(sb, 0.0)


def _attn_kernel(n_ctx, n_lat, key_chunk, ctx_blocks, q_ref, k_ref, v_ref, o_ref, m_scr, l_scr, acc_scr):
    f32 = jnp.float32
    j = pl.program_id(2)
    q = jnp.concatenate([q_ref[0, :, hd * HEAD_DIM:(hd + 1) * HEAD_DIM]
                         for hd in range(GQA_GROUP)], axis=0)

    kc, vc = k_ref[0, 0:n_ctx, :], v_ref[0, 0:n_ctx, :]
    s = lax.dot_general(q, kc, _NT, preferred_element_type=f32)
    m = jnp.max(s, axis=-1, keepdims=True)
    p = jnp.exp2(s - m)
    m_scr[...] = m
    l_scr[...] = jnp.sum(p, axis=-1, keepdims=True)
    acc_scr[...] = jnp.dot(p.astype(vc.dtype), vc, preferred_element_type=f32)

    @pl.when(j >= ctx_blocks)
    def _():
        def body(c, carry):
            r0 = pl.multiple_of(n_ctx + c * key_chunk, ROW_BLOCK)
            kc = k_ref[0, pl.ds(r0, key_chunk), :]
            vc = v_ref[0, pl.ds(r0, key_chunk), :]
            s = lax.dot_general(q, kc, _NT, preferred_element_type=f32)
            m_old = m_scr[...]
            m_new = jnp.maximum(m_old, jnp.max(s, axis=-1, keepdims=True))
            alpha = jnp.exp2(m_old - m_new)
            p = jnp.exp2(s - m_new)
            m_scr[...] = m_new
            l_scr[...] = alpha * l_scr[...] + jnp.sum(p, axis=-1, keepdims=True)
            acc_scr[...] = alpha * acc_scr[...] + jnp.dot(p.astype(vc.dtype), vc,
                                                          preferred_element_type=f32)
            return carry

        lax.fori_loop(0, n_lat // key_chunk, body, 0)

    out = acc_scr[...] / l_scr[...]
    for hd in range(GQA_GROUP):
        o_ref[0, :, hd * HEAD_DIM:(hd + 1) * HEAD_DIM] = (
            out[hd * ROW_BLOCK:(hd + 1) * ROW_BLOCK]).astype(o_ref.dtype)


def attention(q, k, v, n_ctx):
    bsz, n_tok, _ = q.shape
    nblk = n_tok // ROW_BLOCK
    gw = GQA_GROUP * HEAD_DIM
    rows = GQA_GROUP * ROW_BLOCK
    key_chunk = math.gcd(n_tok - n_ctx, ATTN_KEYS)
    return pl.pallas_call(
        functools.partial(_attn_kernel, n_ctx, n_tok - n_ctx, key_chunk, n_ctx // ROW_BLOCK),
        grid=(bsz, N_KV_HEADS, nblk),
        in_specs=[pl.BlockSpec((1, ROW_BLOCK, gw), lambda b, g, j: (b, j, g)),
                  pl.BlockSpec((1, n_tok, HEAD_DIM), lambda b, g, j: (b, 0, g)),
                  pl.BlockSpec((1, n_tok, HEAD_DIM), lambda b, g, j: (b, 0, g))],
        out_specs=pl.BlockSpec((1, ROW_BLOCK, gw), lambda b, g, j: (b, j, g)),
        out_shape=jax.ShapeDtypeStruct((bsz, n_tok, ATTN_WIDTH), jnp.bfloat16),
        scratch_shapes=[pltpu.VMEM((rows, 1), jnp.float32), pltpu.VMEM((rows, 1), jnp.float32),
                        pltpu.VMEM((rows, HEAD_DIM), jnp.float32)],
        compiler_params=pltpu.CompilerParams(vmem_limit_bytes=VMEM_LIMIT),
        name="attention",
    )(q, k, v)


def _cpow(e, lam_re_dt, lam_im_dt):
    mag = jnp.exp(e * lam_re_dt)
    ang = e * lam_im_dt
    return mag * jnp.cos(ang), mag * jnp.sin(ang)


def _s5_kernel(n_batch, ctx_chunks, n_chunks,
               u_ref, arow_ref, acol_ref, b_ref, c_ref, ct_ref, y_ref,
               sfr, sfi, sbr, sbi, hfr, hfi, hbr, hbi):
    f32, bf16 = jnp.float32, jnp.bfloat16
    width = S5_CHUNK * SSM_GROUP
    lane = lax.broadcasted_iota(jnp.int32, (1, width), 1)
    lane_blk = lane // SSM_GROUP
    expand = (lax.broadcasted_iota(jnp.int32, (SSM_GROUP, width), 1) % SSM_GROUP
              == lax.broadcasted_iota(jnp.int32, (SSM_GROUP, width), 0)).astype(f32)
    rowi = lax.broadcasted_iota(jnp.int32, (width, 1), 0) // SSM_GROUP
    u = u_ref[0].astype(bf16)

    m_tot = jnp.zeros((width, width), f32)
    proj, read, step = [], [], []
    for d in range(2):
        fwd = d == 0
        are_c, aim_c, dt_c = acol_ref[d, 0, 0], acol_ref[d, 0, 1], jnp.exp(acol_ref[d, 0, 2])
        lr_c, li_c = are_c * dt_c, aim_c * dt_c
        abr, abi = _cpow(1.0, lr_c, li_c)
        den = are_c * are_c + aim_c * aim_c
        nr, ni = abr - 1.0, abi
        k_r = (nr * are_c + ni * aim_c) / den
        k_i = (ni * are_c - nr * aim_c) / den
        b_re, b_im = b_ref[d, 0, 0], b_ref[d, 0, 1]
        bb_r = k_r * b_re - k_i * b_im
        bb_i = k_r * b_im + k_i * b_re
        bt_r = jnp.dot(bb_r, expand, preferred_element_type=f32)
        bt_i = jnp.dot(bb_i, expand, preferred_element_type=f32)
        e_in = (S5_CHUNK - 1 - lane_blk if fwd else lane_blk).astype(f32)
        ap_r, ap_i = _cpow(e_in, lr_c, li_c)
        proj.append(((ap_r * bt_r - ap_i * bt_i).astype(bf16),
                     (ap_r * bt_i + ap_i * bt_r).astype(bf16)))
        ct_r = jnp.dot(ct_ref[d, 0, 0], expand, preferred_element_type=f32)
        ct_i = jnp.dot(ct_ref[d, 0, 1], expand, preferred_element_type=f32)
        e_out = (lane_blk + 1 if fwd else S5_CHUNK - lane_blk).astype(f32)
        aq_r, aq_i = _cpow(e_out, lr_c, li_c)
        read.append(((ct_r * aq_r - ct_i * aq_i).astype(bf16),
                     (-(ct_r * aq_i + ct_i * aq_r)).astype(bf16)))
        are_r, aim_r, dt_r = arow_ref[d, 0, 0], arow_ref[d, 0, 1], jnp.exp(arow_ref[d, 0, 2])
        lr_r, li_r = are_r * dt_r, aim_r * dt_r
        step.append(_cpow(float(S5_CHUNK), lr_r, li_r))
        lag = (rowi if fwd else S5_CHUNK - 1 - rowi).astype(f32)
        al_r, al_i = _cpow(lag, lr_r, li_r)
        c_re = jnp.concatenate([c_ref[d, 0, 0]] * S5_CHUNK, axis=0)
        c_im = jnp.concatenate([c_ref[d, 0, 1]] * S5_CHUNK, axis=0)
        ca_r = c_re * al_r - c_im * al_i
        ca_i = c_re * al_i + c_im * al_r
        kmat = (jnp.dot(ca_r, bb_r, preferred_element_type=f32)
                - jnp.dot(ca_i, bb_i, preferred_element_type=f32))
        kt = jnp.dot(kmat, expand, preferred_element_type=f32)
        for s in range(S5_CHUNK):
            sh = (s if fwd else S5_CHUNK - 1 - s) * SSM_GROUP
            if sh == 0:
                shifted = kt
            elif fwd:
                shifted = jnp.concatenate([jnp.zeros((sh, width), f32), kt[:width - sh]], axis=0)
            else:
                shifted = jnp.concatenate([kt[sh:], jnp.zeros((sh, width), f32)], axis=0)
            m_tot = m_tot + jnp.where(lane_blk == s, shifted, 0.0)

    y_ref[0] = lax.dot_general(u, m_tot.astype(bf16), _NT, preferred_element_type=f32)
    sfr[...] = lax.dot_general(u, proj[0][0], _NT, preferred_element_type=f32)
    sfi[...] = lax.dot_general(u, proj[0][1], _NT, preferred_element_type=f32)
    sbr[...] = lax.dot_general(u, proj[1][0], _NT, preferred_element_type=f32)
    sbi[...] = lax.dot_general(u, proj[1][1], _NT, preferred_element_type=f32)

    (afr, afi), (abr_, abi_) = step

    def scan_body(i, carry):
        fr, fi, br, bi = carry
        rf = pl.multiple_of(i * n_batch, n_batch)
        hfr[pl.ds(rf, n_batch), :] = fr
        hfi[pl.ds(rf, n_batch), :] = fi
        nfr = afr * fr - afi * fi + sfr[pl.ds(rf, n_batch), :]
        nfi = afr * fi + afi * fr + sfi[pl.ds(rf, n_batch), :]
        cb = jnp.where(i < ctx_chunks, ctx_chunks - 1 - i, n_chunks - 1 + ctx_chunks - i)
        rb = pl.multiple_of(cb * n_batch, n_batch)
        hbr[pl.ds(rb, n_batch), :] = br
        hbi[pl.ds(rb, n_batch), :] = bi
        nbr = abr_ * br - abi_ * bi + sbr[pl.ds(rb, n_batch), :]
        nbi = abr_ * bi + abi_ * br + sbi[pl.ds(rb, n_batch), :]
        return nfr, nfi, nbr, nbi

    z = jnp.zeros((n_batch, SSM_STATE), f32)
    lax.fori_loop(0, n_chunks, scan_body, (z, z, z, z))

    y_ref[0] += (jnp.dot(hfr[...].astype(bf16), read[0][0], preferred_element_type=f32)
                 + jnp.dot(hfi[...].astype(bf16), read[0][1], preferred_element_type=f32)
                 + jnp.dot(hbr[...].astype(bf16), read[1][0], preferred_element_type=f32)
                 + jnp.dot(hbi[...].astype(bf16), read[1][1], preferred_element_type=f32))


def s5_scan(u, a_re, a_im, log_dt, b_re, b_im, c_re, c_im, n_ctx):
    bsz, n_tok, _ = u.shape
    n_chunks = n_tok // S5_CHUNK
    rows = n_chunks * bsz
    width = S5_CHUNK * SSM_GROUP
    ug = u.reshape(bsz, n_chunks, S5_CHUNK, SSM_GROUPS, SSM_GROUP)
    ug = ug.transpose(3, 1, 0, 2, 4).reshape(SSM_GROUPS, rows, width)
    ldt = jnp.broadcast_to(log_dt[..., None], a_re.shape)
    arow = jnp.stack([a_re, a_im, ldt], axis=2)[:, :, :, None, :]
    acol = jnp.stack([a_re, a_im, ldt], axis=2)[..., None]
    bmat = jnp.stack([b_re, b_im], axis=2)
    cmat = jnp.stack([c_re, c_im], axis=2)
    ctmat = jnp.swapaxes(cmat, -1, -2)
    grp = lambda a: pl.BlockSpec((2, 1) + a.shape[2:], lambda g: (0, g) + (0,) * (a.ndim - 2))
    state = pltpu.VMEM((rows, SSM_STATE), jnp.float32)
    y = pl.pallas_call(
        functools.partial(_s5_kernel, bsz, n_ctx // S5_CHUNK, n_chunks),
        grid=(SSM_GROUPS,),
        in_specs=[pl.BlockSpec((1, rows, width), lambda g: (g, 0, 0)),
                  grp(arow), grp(acol), grp(bmat), grp(cmat), grp(ctmat)],
        out_specs=pl.BlockSpec((1, rows, width), lambda g: (g, 0, 0)),
        out_shape=jax.ShapeDtypeStruct((SSM_GROUPS, rows, width), jnp.float32),
        scratch_shapes=[state] * 8,
        compiler_params=pltpu.CompilerParams(vmem_limit_bytes=VMEM_LIMIT),
        name="s5_scan",
    )(ug, arow, acol, bmat, cmat, ctmat)
    y = y.reshape(SSM_GROUPS, n_chunks, bsz, S5_CHUNK, SSM_GROUP)
    return y.transpose(2, 1, 3, 0, 4).reshape(bsz, n_tok, SSM_WIDTH)


def _merge_kernel(ctx_blocks, h_ref, mod_ref, attn_ref, y_ref, u_ref, ga_ref, gb_ref,
                  dskip_ref, wglu_ref, bglu_ref, wba_ref, wbs_ref, wo_ref, o_ref):
    f32, bf16 = jnp.float32, jnp.bfloat16
    b, j = pl.program_id(0), pl.program_id(1)
    row = _mod_row(b, j, ctx_blocks)
    gate = mod_ref[pl.ds(row, 1), 2 * D_MODEL:3 * D_MODEL]
    y = jax.nn.gelu(y_ref[0] + dskip_ref[...] * u_ref[0])
    glu = jnp.dot(y.astype(bf16), wglu_ref[...], preferred_element_type=f32) + bglu_ref[...]
    ssm = y * jax.nn.sigmoid(glu)
    ba = jnp.dot(attn_ref[0], wba_ref[...], preferred_element_type=f32)
    bs = jnp.dot(ssm.astype(bf16), wbs_ref[...], preferred_element_type=f32)
    mix = jax.nn.sigmoid(ga_ref[0]) * ba + jax.nn.sigmoid(gb_ref[0]) * bs
    out = jnp.dot(mix.astype(bf16), wo_ref[...], preferred_element_type=f32)
    o_ref[0] = h_ref[0] + gate * out


def branch_merge(h, mod, attn, y, u, ga, gb, d_skip, w_glu, b_glu, w_ba, w_bs, w_o, n_ctx):
    bsz, n_tok, _ = h.shape
    nblk = n_tok // ROW_BLOCK
    row_spec = lambda w: pl.BlockSpec((1, ROW_BLOCK, w), lambda b, j: (b, j, 0))
    full = lambda a: pl.BlockSpec(a.shape, lambda b, j: (0,) * a.ndim)
    return pl.pallas_call(
        functools.partial(_merge_kernel, n_ctx // ROW_BLOCK),
        grid=(bsz, nblk),
        in_specs=[row_spec(D_MODEL), full(mod), row_spec(ATTN_WIDTH), row_spec(SSM_WIDTH),
                  row_spec(SSM_WIDTH), row_spec(D_MODEL), row_spec(D_MODEL),
                  full(d_skip), full(w_glu), full(b_glu), full(w_ba), full(w_bs), full(w_o)],
        out_specs=row_spec(D_MODEL),
        out_shape=jax.ShapeDtypeStruct(h.shape, jnp.float32),
        compiler_params=pltpu.CompilerParams(vmem_limit_bytes=VMEM_LIMIT),
        name="branch_merge",
    )(h, mod, attn, y, u, ga, gb, d_skip, w_glu, b_glu, w_ba, w_bs, w_o)


def _top16(jobs, vals_ref, pay_ref):
    rows = [lax.broadcasted_iota(jnp.int32, s.shape, 0).astype(jnp.float32) for s, _, _ in jobs]

    def body(r, carry):
        nxt = []
        for s, (s0, payload, base), row in zip(carry, jobs, rows):
            m = jnp.max(s, axis=0, keepdims=True)
            pos = jnp.min(jnp.where(s == m, row, float(s0.shape[0])), axis=0, keepdims=True)
            hit = row == pos
            vals_ref[pl.ds(base + r, 1), :] = m
            if payload is None:
                pay_ref[pl.ds(base + r, 1), :] = pos
            else:
                pay_ref[pl.ds(base + r, 1), :] = jnp.max(jnp.where(hit, payload, -1.0), axis=0,
                                                         keepdims=True)
            nxt.append(jnp.where(hit, -jnp.inf, s))
        return tuple(nxt)

    lax.fori_loop(0, PEER_TOPK, body, tuple(s for s, _, _ in jobs))


def _peer_query_kernel(ctx_blocks, h_ref, mod_ref, g_ref, wq_ref, keys_ref,
                       xw_ref, idx_ref, gate_ref, q_scr, sv, si, bv, be):
    f32, bf16 = jnp.float32, jnp.bfloat16
    b, j = pl.program_id(0), pl.program_id(1)
    row = _mod_row(b, j, ctx_blocks)
    shift = mod_ref[pl.ds(row, 1), 3 * D_MODEL:4 * D_MODEL]
    scale = mod_ref[pl.ds(row, 1), 4 * D_MODEL:5 * D_MODEL]
    x = h_ref[0]
    xn = x * lax.rsqrt(jnp.mean(x * x, axis=-1, keepdims=True) + EPS) * g_ref[...]
    xm = xn * (1.0 + scale) + shift
    xb = xm.astype(bf16)
    bits = lax.bitcast_convert_type(xb.astype(f32), jnp.int32)
    for r in range(EXPERT_ROWS):
        lo = (bits[:, (2 * r) * 128:(2 * r + 1) * 128] >> 16) & 0xFFFF
        hi = bits[:, (2 * r + 1) * 128:(2 * r + 2) * 128] & jnp.int32(-65536)
        xw_ref[r] = lo | hi
    for hp in range(2 * PEER_HEADS):
        lo = hp * PEER_HALF
        q_scr[hp] = jnp.dot(xb, wq_ref[:, lo:lo + PEER_HALF],
                            preferred_element_type=f32).astype(bf16)

    half = PEER_TOPK // 2

    def candidates(s1, s2, i1, i2):
        cand, cexp = [], []
        for i in range(half):
            size = PEER_TOPK if i == 0 else half
            blk = s1[i:i + 1, :] + s2[0:size, :]
            keep = PEER_TOPK // (i + 1)
            if keep < size:
                jrow = lax.broadcasted_iota(jnp.int32, (size, 1), 0)
                blk = jnp.where(jrow < keep, blk, -jnp.inf)
            cand.append(blk)
            cexp.append(i1[i:i + 1, :] * PEER_N_KEYS + i2[0:size, :])
        cand.append(s1[half:PEER_TOPK, :] + s2[0:1, :])
        cexp.append(i1[half:PEER_TOPK, :] * PEER_N_KEYS + i2[0:1, :])
        return jnp.concatenate(cand, axis=0), jnp.concatenate(cexp, axis=0)

    def head_pair_body(hp, carry):
        for d in range(2):
            _top16([(lax.dot_general(keys_ref[p], q_scr[4 * hp + 2 * d + p], _NT,
                                     preferred_element_type=f32),
                     None, (2 * d + p) * PEER_TOPK) for p in range(2)], sv, si)
        jobs = []
        for d in range(2):
            lo = 2 * d * PEER_TOPK
            s1, s2 = sv[lo:lo + PEER_TOPK, :], sv[lo + PEER_TOPK:lo + 2 * PEER_TOPK, :]
            i1, i2 = si[lo:lo + PEER_TOPK, :], si[lo + PEER_TOPK:lo + 2 * PEER_TOPK, :]
            cand, cexp = candidates(s1, s2, i1, i2)
            jobs.append((cand, cexp, pl.multiple_of((2 * hp + d) * PEER_TOPK, PEER_TOPK)))
        _top16(jobs, bv, be)
        return carry

    lax.fori_loop(0, PEER_HEADS // 2, head_pair_body, 0)

    for hd in range(PEER_HEADS):
        best = bv[hd * PEER_TOPK:(hd + 1) * PEER_TOPK, :]
        e = jnp.exp(best - jnp.max(best, axis=0, keepdims=True))
        bv[hd * PEER_TOPK:(hd + 1) * PEER_TOPK, :] = e / jnp.sum(e, axis=0, keepdims=True)
    for t in range(ROW_BLOCK // PEER_TOK):
        idx_ref[t] = (be[:, t * PEER_TOK:(t + 1) * PEER_TOK].T * EXPERT_ROWS).astype(jnp.int32)
        gate_ref[t] = bv[:, t * PEER_TOK:(t + 1) * PEER_TOK]


def peer_query(h, mod, g, wq_bf16, keys_bf16, n_ctx):
    bsz, n_tok, _ = h.shape
    nblk = n_tok // ROW_BLOCK
    sub = ROW_BLOCK // PEER_TOK
    n_pblk = bsz * n_tok // PEER_TOK
    row_spec = pl.BlockSpec((1, ROW_BLOCK, D_MODEL), lambda b, j: (b, j, 0))
    full = lambda a: pl.BlockSpec(a.shape, lambda b, j: (0,) * a.ndim)
    pick_spec = pl.BlockSpec((sub, PEER_PICKS, PEER_TOK), lambda b, j: (b * nblk + j, 0, 0))
    f32, i32 = jnp.float32, jnp.int32
    return pl.pallas_call(
        functools.partial(_peer_query_kernel, n_ctx // ROW_BLOCK),
        grid=(bsz, nblk),
        in_specs=[row_spec, full(mod), full(g), full(wq_bf16), full(keys_bf16)],
        out_specs=[pl.BlockSpec((EXPERT_ROWS, ROW_BLOCK, 128), lambda b, j: (0, b * nblk + j, 0)),
                   pick_spec, pick_spec],
        out_shape=[jax.ShapeDtypeStruct((EXPERT_ROWS, bsz * n_tok, 128), i32),
                   jax.ShapeDtypeStruct((n_pblk, PEER_PICKS, PEER_TOK), i32),
                   jax.ShapeDtypeStruct((n_pblk, PEER_PICKS, PEER_TOK), f32)],
        scratch_shapes=[pltpu.VMEM((2 * PEER_HEADS, ROW_BLOCK, PEER_HALF), jnp.bfloat16),
                        pltpu.VMEM((4 * PEER_TOPK, ROW_BLOCK), f32),
                        pltpu.VMEM((4 * PEER_TOPK, ROW_BLOCK), f32),
                        pltpu.VMEM((PEER_PICKS, ROW_BLOCK), f32),
                        pltpu.VMEM((PEER_PICKS, ROW_BLOCK), f32)],
        compiler_params=pltpu.CompilerParams(vmem_limit_bytes=VMEM_LIMIT),
        name="peer_query",
    )(h, mod, g, wq_bf16, keys_bf16)


def pack_rows(rows):
    bits = lax.bitcast_convert_type(rows.astype(jnp.bfloat16), jnp.uint16).astype(jnp.uint32)
    bits = bits.reshape(-1, EXPERT_ROWS, 2, 128)
    words = bits[:, :, 0, :] | (bits[:, :, 1, :] << 16)
    return lax.bitcast_convert_type(words, jnp.int32)


def _group_chunks(idx_ref, tab_ref, r0):
    rows = [idx_ref.at[0, 0, pl.ds((r0 + a) * PEER_PICKS, PEER_PICKS)] for a in range(PEER_STAGE)]
    for c in range(PEER_PICKS // PEER_CHUNK):
        tiles = [[None] * PEER_CHUNK for _ in range(PEER_STAGE)]
        for j in range(PEER_CHUNK):
            for a in range(PEER_STAGE):
                off = pl.multiple_of(rows[a][c * PEER_CHUNK + j], EXPERT_ROWS)
                tiles[a][j] = tab_ref[pl.ds(off, EXPERT_ROWS), :]
        yield c, [jnp.concatenate(tiles[a], axis=0) for a in range(PEER_STAGE)]


def _pick_rows_mask():
    tile_rows = 2 * EXPERT_ROWS
    shape = (PEER_PICKS, PEER_PICKS * tile_rows)
    return (lax.broadcasted_iota(jnp.int32, shape, 1) // tile_rows
            == lax.broadcasted_iota(jnp.int32, shape, 0))


def _peer_score_kernel(idx_ref, x_ref, gate_ref, tab_ref, w_ref):
    f32, bf16, i32 = jnp.float32, jnp.bfloat16, jnp.int32
    sel = _pick_rows_mask().astype(bf16)
    lane = lax.broadcasted_iota(i32, (PEER_PICKS, PEER_TOK), 1)
    cols = PEER_CHUNK * 2 * EXPERT_ROWS

    def group_body(g, acc):
        r0 = pl.multiple_of(g * PEER_STAGE, PEER_STAGE)
        xg = x_ref[:, pl.ds(r0, PEER_STAGE), :]
        xts = []
        for a in range(PEER_STAGE):
            xw = jnp.concatenate([xg[r, a:a + 1, :] for r in range(EXPERT_ROWS)], axis=0)
            xts.append(pltpu.bitcast(jnp.concatenate([xw] * PEER_CHUNK, axis=0), bf16))
        parts = [[] for _ in range(PEER_STAGE)]
        for c, words in _group_chunks(idx_ref, tab_ref, r0):
            for a in range(PEER_STAGE):
                prod = pltpu.bitcast(words[a], bf16) * xts[a]
                parts[a].append(jnp.dot(sel[0:PEER_CHUNK, 0:cols], prod,
                                        preferred_element_type=f32))
        for a in range(PEER_STAGE):
            part = jnp.concatenate(parts[a], axis=0)
            acc = jnp.where(lane == r0 + a, jnp.sum(part, axis=1, keepdims=True), acc)
        return acc

    scores = lax.fori_loop(0, PEER_TOK // PEER_STAGE, group_body,
                           jnp.zeros((PEER_PICKS, PEER_TOK), f32))
    w = gate_ref[0] * jax.nn.gelu(scores)
    w_ref[...] = jnp.dot(w.T.astype(bf16), sel, preferred_element_type=f32)


def _peer_mix_kernel(idx_ref, wexp_ref, h_ref, gate_ref, tab_ref, o_ref):
    f32, bf16, i32 = jnp.float32, jnp.bfloat16, jnp.int32
    tile_rows = 2 * EXPERT_ROWS
    shape = (tile_rows, PEER_PICKS * tile_rows)
    diag = (lax.broadcasted_iota(i32, shape, 1) % tile_rows
            == lax.broadcasted_iota(i32, shape, 0))
    cols = PEER_CHUNK * tile_rows

    def group_body(g, carry):
        r0 = pl.multiple_of(g * PEER_STAGE, PEER_STAGE)
        tiles = []
        wmats = [jnp.where(diag, wexp_ref[pl.ds(r0 + a, 1), :], 0.0).astype(bf16)
                 for a in range(PEER_STAGE)]
        tiles = [jnp.zeros((tile_rows, 128), f32)] * PEER_STAGE
        for c, words in _group_chunks(idx_ref, tab_ref, r0):
            tiles = [tiles[a] + jnp.dot(wmats[a][:, c * cols:(c + 1) * cols],
                                        pltpu.bitcast(words[a], bf16),
                                        preferred_element_type=f32) for a in range(PEER_STAGE)]
        rows = jnp.concatenate(
            [jnp.concatenate([tile[r:r + 1, :] for tile in tiles], axis=0)
             for r in range(tile_rows)], axis=1)
        o_ref[pl.ds(r0, PEER_STAGE), :] = (h_ref[pl.ds(r0, PEER_STAGE), :]
                                           + gate_ref[0, 0] * rows)
        return carry

    lax.fori_loop(0, PEER_TOK // PEER_STAGE, group_body, 0)


def _table_spec(tab):
    return pl.BlockSpec(tab.shape, lambda i: (0,) * tab.ndim, pipeline_mode=pl.Buffered(1))


def peer_experts(h, xw, idx, gates, mod, u_packed, v_packed, n_ctx):
    bsz, n_tok, _ = h.shape
    n_pblk = idx.shape[0]
    blk_per_batch = n_tok // PEER_TOK
    ctx_pblk = n_ctx // PEER_TOK
    flat = PEER_PICKS * PEER_TOK
    smem_spec = pl.BlockSpec((1, 1, flat), lambda i: (i, 0, 0), memory_space=pltpu.SMEM)
    pick_spec = pl.BlockSpec((1, PEER_PICKS, PEER_TOK), lambda i: (i, 0, 0))
    word_spec = pl.BlockSpec((EXPERT_ROWS, PEER_TOK, 128), lambda i: (0, i, 0))
    row_spec = pl.BlockSpec((PEER_TOK, D_MODEL), lambda i: (i, 0))
    idx_flat = idx.reshape(n_pblk, 1, flat)
    wexp = pl.pallas_call(
        _peer_score_kernel,
        grid=(n_pblk,),
        in_specs=[smem_spec, word_spec, pick_spec, _table_spec(u_packed)],
        out_specs=row_spec,
        out_shape=jax.ShapeDtypeStruct((bsz * n_tok, D_MODEL), jnp.float32),
        compiler_params=pltpu.CompilerParams(vmem_limit_bytes=VMEM_LIMIT),
        name="peer_scores",
    )(idx_flat, xw, gates, u_packed)

    def gate_map(i):
        bi, ji = i // blk_per_batch, i % blk_per_batch
        return (jnp.where(ji < ctx_pblk, 0, bi + 1), N_MOD - 1, 0, 0)

    out = pl.pallas_call(
        _peer_mix_kernel,
        grid=(n_pblk,),
        in_specs=[smem_spec, row_spec, row_spec,
                  pl.BlockSpec((1, 1, 1, D_MODEL), gate_map), _table_spec(v_packed)],
        out_specs=row_spec,
        out_shape=jax.ShapeDtypeStruct((bsz * n_tok, D_MODEL), jnp.float32),
        compiler_params=pltpu.CompilerParams(vmem_limit_bytes=VMEM_LIMIT),
        name="peer_mix",
    )(idx_flat, wexp, h.reshape(-1, D_MODEL), mod.reshape(MOD_ROWS, N_MOD, 1, D_MODEL), v_packed)
    return out.reshape(h.shape)


def kernel(x, c, ctx, c_ctx, w_mod, b_mod, norm_mix_g, w_in, q_norm_g, k_norm_g, ssm_a_re, ssm_a_im, ssm_log_dt, ssm_b_re, ssm_b_im, ssm_c_re, ssm_c_im, ssm_d, w_glu, b_glu, w_branch_attn, w_branch_ssm, w_out, norm_ffn_g, peer_w_q, peer_keys, peer_u, peer_v):
    bf16 = jnp.bfloat16
    bsz, n_lat, _ = x.shape
    n_ctx = ctx.shape[1]
    depth = w_mod.shape[0]
    assert bsz + 1 <= MOD_ROWS and n_ctx % ROW_BLOCK == 0 and n_lat % ROW_BLOCK == 0

    cc = jnp.zeros((MOD_ROWS, D_MODEL), jnp.float32).at[0].set(c_ctx).at[1:bsz + 1].set(c)
    mods = modulation(cc, w_mod, b_mod)
    rope = rope_tables(n_ctx, n_lat)
    h = jnp.concatenate([ctx, x], axis=1)
    row = lambda a: a.reshape(1, -1)

    for i in range(depth):
        mod = mods[i]
        q, k, v, u, ga, gb = in_projection(h, mod, row(norm_mix_g[i]), w_in[i].astype(bf16), rope,
                                           row(q_norm_g[i]), row(k_norm_g[i]), n_ctx)
        attn = attention(q, k, v, n_ctx)
        y = s5_scan(u.astype(bf16), ssm_a_re[i], ssm_a_im[i], ssm_log_dt[i],
                    ssm_b_re[i], ssm_b_im[i], ssm_c_re[i], ssm_c_im[i], n_ctx)
        h = branch_merge(h, mod, attn, y, u, ga, gb, row(ssm_d[i]), w_glu[i].astype(bf16),
                         row(b_glu[i]), w_branch_attn[i].astype(bf16),
                         w_branch_ssm[i].astype(bf16), w_out[i].astype(bf16), n_ctx)
        n_peer_ctx = n_ctx
        if i == depth - 1:
            h, n_peer_ctx = h[:, n_ctx:, :], 0
        xw, idx, gates = peer_query(h, mod, row(norm_ffn_g[i]), peer_w_q[i].astype(bf16),
                                    peer_keys[i].astype(bf16), n_peer_ctx)
        h = peer_experts(h, xw, idx, gates, mod, pack_rows(peer_u[i]).reshape(-1, 128),
                         pack_rows(peer_v[i]).reshape(-1, 128), n_peer_ctx)
    return h
```

```python
import functools
import math

import jax
import jax.numpy as jnp
from jax import lax
from jax.experimental import pallas as pl
from jax.experimental.pallas import tpu as pltpu

D_MODEL = 1024
GRID_W = 64
HEAD_DIM = 128
N_Q_HEADS = 8
N_KV_HEADS = 2
GQA_GROUP = N_Q_HEADS // N_KV_HEADS
ATTN_WIDTH = N_Q_HEADS * HEAD_DIM
KV_WIDTH = N_KV_HEADS * HEAD_DIM
ROPE_THETA = 10000.0
ROPE_FREQS = HEAD_DIM // 4
ATTN_SCALE = HEAD_DIM ** -0.5
SSM_WIDTH = D_MODEL // 2
SSM_GROUP = 16
SSM_GROUPS = SSM_WIDTH // SSM_GROUP
SSM_STATE = 64
IN_WIDTH = ATTN_WIDTH + 2 * KV_WIDTH + SSM_WIDTH + 2 * D_MODEL
PEER_HEADS = 8
PEER_N_KEYS = 128
PEER_N_EXPERTS = PEER_N_KEYS ** 2
PEER_QUERY_DIM = 256
PEER_HALF = PEER_QUERY_DIM // 2
PEER_TOPK = 16
PEER_PICKS = PEER_HEADS * PEER_TOPK
N_MOD = 6
EPS = 1e-6

ROW_BLOCK = 256
ATTN_KEYS = 4096
MOD_ROWS = 16
S5_CHUNK = 16
PEER_TOK = 128
EXPERT_ROWS = 4
PEER_STAGE = 128
PEER_CHUNK = 16
VMEM_LIMIT = 56 * 1024 * 1024

_NT = (((1,), (1,)), ((), ()))


def _mod_row(b, j, ctx_blocks):
    return jnp.where(j < ctx_blocks, 0, b + 1)


def _mod_kernel(c_ref, w_ref, b_ref, o_ref):
    c = c_ref[...]
    s = c * jax.nn.sigmoid(c)
    o_ref[0] = jnp.dot(s.astype(jnp.bfloat16), w_ref[0].astype(jnp.bfloat16),
                       preferred_element_type=jnp.float32) + b_ref[0]


def modulation(cc, w_mod, b_mod):
    depth = w_mod.shape[0]
    nblk = 1536
    width = N_MOD * D_MODEL
    return pl.pallas_call(
        _mod_kernel,
        grid=(depth, width // nblk),
        in_specs=[pl.BlockSpec((MOD_ROWS, D_MODEL), lambda l, n: (0, 0)),
                  pl.BlockSpec((1, D_MODEL, nblk), lambda l, n: (l, 0, n)),
                  pl.BlockSpec((1, 1, nblk), lambda l, n: (l, 0, n))],
        out_specs=pl.BlockSpec((1, MOD_ROWS, nblk), lambda l, n: (l, 0, n)),
        out_shape=jax.ShapeDtypeStruct((depth, MOD_ROWS, width), jnp.float32),
        name="modulation",
    )(cc, w_mod, b_mod.reshape(depth, 1, width))


def _inproj_kernel(ctx_blocks, h_ref, mod_ref, g_ref, w_ref, cos_ref, sa_ref, sb_ref,
                   qg_ref, kg_ref, q_ref, k_ref, v_ref, u_ref, ga_ref, gb_ref):
    b, j = pl.program_id(0), pl.program_id(1)
    row = _mod_row(b, j, ctx_blocks)
    shift = mod_ref[pl.ds(row, 1), 0:D_MODEL]
    scale = mod_ref[pl.ds(row, 1), D_MODEL:2 * D_MODEL]
    x = h_ref[0]
    xn = x * lax.rsqrt(jnp.mean(x * x, axis=-1, keepdims=True) + EPS) * g_ref[...]
    xm = (xn * (1.0 + scale) + shift).astype(jnp.bfloat16)

    cos, sa, sb = cos_ref[...], sa_ref[...], sb_ref[...]

    def norm_rope(z, g, out_scale):
        zn = z * lax.rsqrt(jnp.mean(z * z, axis=-1, keepdims=True) + EPS) * g
        r = zn * cos + pltpu.roll(zn, 96, 1) * sa + pltpu.roll(zn, 32, 1) * sb
        return r * out_scale

    for hd in range(N_Q_HEADS):
        lo = hd * HEAD_DIM
        z = jnp.dot(xm, w_ref[:, lo:lo + HEAD_DIM], preferred_element_type=jnp.float32)
        q_ref[0, :, lo:lo + HEAD_DIM] = norm_rope(z, qg_ref[...], ATTN_SCALE * math.log2(math.e)).astype(q_ref.dtype)
    for hd in range(N_KV_HEADS):
        lo = hd * HEAD_DIM
        z = jnp.dot(xm, w_ref[:, ATTN_WIDTH + lo:ATTN_WIDTH + lo + HEAD_DIM],
                    preferred_element_type=jnp.float32)
        k_ref[0, :, lo:lo + HEAD_DIM] = norm_rope(z, kg_ref[...], 1.0).astype(k_ref.dtype)
    off = ATTN_WIDTH + KV_WIDTH
    v_ref[0] = jnp.dot(xm, w_ref[:, off:off + KV_WIDTH],
                       preferred_element_type=jnp.float32).astype(v_ref.dtype)
    off += KV_WIDTH
    u_ref[0] = jnp.dot(xm, w_ref[:, off:off + SSM_WIDTH], preferred_element_type=jnp.float32)
    off += SSM_WIDTH
    ga_ref[0] = jnp.dot(xm, w_ref[:, off:off + D_MODEL], preferred_element_type=jnp.float32)
    off += D_MODEL
    gb_ref[0] = jnp.dot(xm, w_ref[:, off:off + D_MODEL], preferred_element_type=jnp.float32)


def in_projection(h, mod, g, w_in_bf16, rope, qg, kg, n_ctx):
    bsz, n_tok, _ = h.shape
    nblk = n_tok // ROW_BLOCK
    cos, sa, sb = rope
    row_spec = lambda w: pl.BlockSpec((1, ROW_BLOCK, w), lambda b, j: (b, j, 0))
    full = lambda a: pl.BlockSpec(a.shape, lambda b, j: (0,) * a.ndim)
    rope_spec = pl.BlockSpec((ROW_BLOCK, HEAD_DIM), lambda b, j: (j, 0))
    f32, bf16 = jnp.float32, jnp.bfloat16
    out_shape = [jax.ShapeDtypeStruct((bsz, n_tok, ATTN_WIDTH), bf16),
                 jax.ShapeDtypeStruct((bsz, n_tok, KV_WIDTH), bf16),
                 jax.ShapeDtypeStruct((bsz, n_tok, KV_WIDTH), bf16),
                 jax.ShapeDtypeStruct((bsz, n_tok, SSM_WIDTH), f32),
                 jax.ShapeDtypeStruct((bsz, n_tok, D_MODEL), f32),
                 jax.ShapeDtypeStruct((bsz, n_tok, D_MODEL), f32)]
    return pl.pallas_call(
        functools.partial(_inproj_kernel, n_ctx // ROW_BLOCK),
        grid=(bsz, nblk),
        in_specs=[row_spec(D_MODEL), full(mod), full(g), full(w_in_bf16),
                  rope_spec, rope_spec, rope_spec, full(qg), full(kg)],
        out_specs=[row_spec(ATTN_WIDTH), row_spec(KV_WIDTH), row_spec(KV_WIDTH),
                   row_spec(SSM_WIDTH), row_spec(D_MODEL), row_spec(D_MODEL)],
        out_shape=out_shape,
        compiler_params=pltpu.CompilerParams(vmem_limit_bytes=VMEM_LIMIT),
        name="in_projection",
    )(h, mod, g, w_in_bf16, cos, sa, sb, qg, kg)


def rope_tables(n_ctx, n_lat):
    pos = jnp.arange(n_lat)
    inv_freq = ROPE_THETA ** (-jnp.arange(ROPE_FREQS, dtype=jnp.float32) / ROPE_FREQS)
    ang_row = (pos // GRID_W).astype(jnp.float32)[:, None] * inv_freq
    ang_col = (pos % GRID_W).astype(jnp.float32)[:, None] * inv_freq
    ang = jnp.concatenate([ang_row, ang_row, ang_col, ang_col], axis=-1)
    cos, sin = jnp.cos(ang), jnp.sin(ang)
    even = ((jnp.arange(HEAD_DIM) // ROPE_FREQS) % 2 == 0)[None, :]
    sa = jnp.where(even, -sin, 0.0)
    sb = jnp.where(even, 0.0, sin)
    pad = lambda t, v: jnp.concatenate([jnp.full((n_ctx, HEAD_DIM), v, jnp.float32), t], axis=0)
    return pad(cos, 1.0), pad(sa, 0.0), pad(sb, 0.0)


def _attn_kernel(n_ctx, n_lat, key_chunk, ctx_blocks, q_ref, k_ref, v_ref, o_ref, m_scr, l_scr, acc_scr):
    f32 = jnp.float32
    j = pl.program_id(2)
    q = jnp.concatenate([q_ref[0, :, hd * HEAD_DIM:(hd + 1) * HEAD_DIM]
                         for hd in range(GQA_GROUP)], axis=0)

    kc, vc = k_ref[0, 0:n_ctx, :], v_ref[0, 0:n_ctx, :]
    s = lax.dot_general(q, kc, _NT, preferred_element_type=f32)
    m = jnp.max(s, axis=-1, keepdims=True)
    p = jnp.exp2(s - m)
    m_scr[...] = m
    l_scr[...] = jnp.sum(p, axis=-1, keepdims=True)
    acc_scr[...] = jnp.dot(p.astype(vc.dtype), vc, preferred_element_type=f32)

    @pl.when(j >= ctx_blocks)
    def _():
        def body(c, carry):
            r0 = pl.multiple_of(n_ctx + c * key_chunk, ROW_BLOCK)
            kc = k_ref[0, pl.ds(r0, key_chunk), :]
            vc = v_ref[0, pl.ds(r0, key_chunk), :]
            s = lax.dot_general(q, kc, _NT, preferred_element_type=f32)
            m_old = m_scr[...]
            m_new = jnp.maximum(m_old, jnp.max(s, axis=-1, keepdims=True))
            alpha = jnp.exp2(m_old - m_new)
            p = jnp.exp2(s - m_new)
            m_scr[...] = m_new
            l_scr[...] = alpha * l_scr[...] + jnp.sum(p, axis=-1, keepdims=True)
            acc_scr[...] = alpha * acc_scr[...] + jnp.dot(p.astype(vc.dtype), vc,
                                                          preferred_element_type=f32)
            return carry

        lax.fori_loop(0, n_lat // key_chunk, body, 0)

    out = acc_scr[...] / l_scr[...]
    for hd in range(GQA_GROUP):
        o_ref[0, :, hd * HEAD_DIM:(hd + 1) * HEAD_DIM] = (
            out[hd * ROW_BLOCK:(hd + 1) * ROW_BLOCK]).astype(o_ref.dtype)


def attention(q, k, v, n_ctx):
    bsz, n_tok, _ = q.shape
    nblk = n_tok // ROW_BLOCK
    gw = GQA_GROUP * HEAD_DIM
    rows = GQA_GROUP * ROW_BLOCK
    key_chunk = math.gcd(n_tok - n_ctx, ATTN_KEYS)
    return pl.pallas_call(
        functools.partial(_attn_kernel, n_ctx, n_tok - n_ctx, key_chunk, n_ctx // ROW_BLOCK),
        grid=(bsz, N_KV_HEADS, nblk),
        in_specs=[pl.BlockSpec((1, ROW_BLOCK, gw), lambda b, g, j: (b, j, g)),
                  pl.BlockSpec((1, n_tok, HEAD_DIM), lambda b, g, j: (b, 0, g)),
                  pl.BlockSpec((1, n_tok, HEAD_DIM), lambda b, g, j: (b, 0, g))],
        out_specs=pl.BlockSpec((1, ROW_BLOCK, gw), lambda b, g, j: (b, j, g)),
        out_shape=jax.ShapeDtypeStruct((bsz, n_tok, ATTN_WIDTH), jnp.bfloat16),
        scratch_shapes=[pltpu.VMEM((rows, 1), jnp.float32), pltpu.VMEM((rows, 1), jnp.float32),
                        pltpu.VMEM((rows, HEAD_DIM), jnp.float32)],
        compiler_params=pltpu.CompilerParams(vmem_limit_bytes=VMEM_LIMIT),
        name="attention",
    )(q, k, v)


def _cpow(e, lam_re_dt, lam_im_dt):
    mag = jnp.exp(e * lam_re_dt)
    ang = e * lam_im_dt
    return mag * jnp.cos(ang), mag * jnp.sin(ang)


def _s5_kernel(n_batch, ctx_chunks, n_chunks,
               u_ref, arow_ref, acol_ref, b_ref, c_ref, ct_ref, y_ref,
               sfr, sfi, sbr, sbi, hfr, hfi, hbr, hbi):
    f32, bf16 = jnp.float32, jnp.bfloat16
    width = S5_CHUNK * SSM_GROUP
    lane = lax.broadcasted_iota(jnp.int32, (1, width), 1)
    lane_blk = lane // SSM_GROUP
    expand = (lax.broadcasted_iota(jnp.int32, (SSM_GROUP, width), 1) % SSM_GROUP
              == lax.broadcasted_iota(jnp.int32, (SSM_GROUP, width), 0)).astype(f32)
    rowi = lax.broadcasted_iota(jnp.int32, (width, 1), 0) // SSM_GROUP
    u = u_ref[0].astype(bf16)

    m_tot = jnp.zeros((width, width), f32)
    proj, read, step = [], [], []
    for d in range(2):
        fwd = d == 0
        are_c, aim_c, dt_c = acol_ref[d, 0, 0], acol_ref[d, 0, 1], jnp.exp(acol_ref[d, 0, 2])
        lr_c, li_c = are_c * dt_c, aim_c * dt_c
        abr, abi = _cpow(1.0, lr_c, li_c)
        den = are_c * are_c + aim_c * aim_c
        nr, ni = abr - 1.0, abi
        k_r = (nr * are_c + ni * aim_c) / den
        k_i = (ni * are_c - nr * aim_c) / den
        b_re, b_im = b_ref[d, 0, 0], b_ref[d, 0, 1]
        bb_r = k_r * b_re - k_i * b_im
        bb_i = k_r * b_im + k_i * b_re
        bt_r = jnp.dot(bb_r, expand, preferred_element_type=f32)
        bt_i = jnp.dot(bb_i, expand, preferred_element_type=f32)
        e_in = (S5_CHUNK - 1 - lane_blk if fwd else lane_blk).astype(f32)
        ap_r, ap_i = _cpow(e_in, lr_c, li_c)
        proj.append(((ap_r * bt_r - ap_i * bt_i).astype(bf16),
                     (ap_r * bt_i + ap_i * bt_r).astype(bf16)))
        ct_r = jnp.dot(ct_ref[d, 0, 0], expand, preferred_element_type=f32)
        ct_i = jnp.dot(ct_ref[d, 0, 1], expand, preferred_element_type=f32)
        e_out = (lane_blk + 1 if fwd else S5_CHUNK - lane_blk).astype(f32)
        aq_r, aq_i = _cpow(e_out, lr_c, li_c)
        read.append(((ct_r * aq_r - ct_i * aq_i).astype(bf16),
                     (-(ct_r * aq_i + ct_i * aq_r)).astype(bf16)))
        are_r, aim_r, dt_r = arow_ref[d, 0, 0], arow_ref[d, 0, 1], jnp.exp(arow_ref[d, 0, 2])
        lr_r, li_r = are_r * dt_r, aim_r * dt_r
        step.append(_cpow(float(S5_CHUNK), lr_r, li_r))
        lag = (rowi if fwd else S5_CHUNK - 1 - rowi).astype(f32)
        al_r, al_i = _cpow(lag, lr_r, li_r)
        c_re = jnp.concatenate([c_ref[d, 0, 0]] * S5_CHUNK, axis=0)
        c_im = jnp.concatenate([c_ref[d, 0, 1]] * S5_CHUNK, axis=0)
        ca_r = c_re * al_r - c_im * al_i
        ca_i = c_re * al_i + c_im * al_r
        kmat = (jnp.dot(ca_r, bb_r, preferred_element_type=f32)
                - jnp.dot(ca_i, bb_i, preferred_element_type=f32))
        kt = jnp.dot(kmat, expand, preferred_element_type=f32)
        for s in range(S5_CHUNK):
            sh = (s if fwd else S5_CHUNK - 1 - s) * SSM_GROUP
            if sh == 0:
                shifted = kt
            elif fwd:
                shifted = jnp.concatenate([jnp.zeros((sh, width), f32), kt[:width - sh]], axis=0)
            else:
                shifted = jnp.concatenate([kt[sh:], jnp.zeros((sh, width), f32)], axis=0)
            m_tot = m_tot + jnp.where(lane_blk == s, shifted, 0.0)

    y_ref[0] = lax.dot_general(u, m_tot.astype(bf16), _NT, preferred_element_type=f32)
    sfr[...] = lax.dot_general(u, proj[0][0], _NT, preferred_element_type=f32)
    sfi[...] = lax.dot_general(u, proj[0][1], _NT, preferred_element_type=f32)
    sbr[...] = lax.dot_general(u, proj[1][0], _NT, preferred_element_type=f32)
    sbi[...] = lax.dot_general(u, proj[1][1], _NT, preferred_element_type=f32)

    (afr, afi), (abr_, abi_) = step

    def scan_body(i, carry):
        fr, fi, br, bi = carry
        rf = pl.multiple_of(i * n_batch, n_batch)
        hfr[pl.ds(rf, n_batch), :] = fr
        hfi[pl.ds(rf, n_batch), :] = fi
        nfr = afr * fr - afi * fi + sfr[pl.ds(rf, n_batch), :]
        nfi = afr * fi + afi * fr + sfi[pl.ds(rf, n_batch), :]
        cb = jnp.where(i < ctx_chunks, ctx_chunks - 1 - i, n_chunks - 1 + ctx_chunks - i)
        rb = pl.multiple_of(cb * n_batch, n_batch)
        hbr[pl.ds(rb, n_batch), :] = br
        hbi[pl.ds(rb, n_batch), :] = bi
        nbr = abr_ * br - abi_ * bi + sbr[pl.ds(rb, n_batch), :]
        nbi = abr_ * bi + abi_ * br + sbi[pl.ds(rb, n_batch), :]
        return nfr, nfi, nbr, nbi

    z = jnp.zeros((n_batch, SSM_STATE), f32)
    lax.fori_loop(0, n_chunks, scan_body, (z, z, z, z))

    y_ref[0] += (jnp.dot(hfr[...].astype(bf16), read[0][0], preferred_element_type=f32)
                 + jnp.dot(hfi[...].astype(bf16), read[0][1], preferred_element_type=f32)
                 + jnp.dot(hbr[...].astype(bf16), read[1][0], preferred_element_type=f32)
                 + jnp.dot(hbi[...].astype(bf16), read[1][1], preferred_element_type=f32))


def s5_scan(u, a_re, a_im, log_dt, b_re, b_im, c_re, c_im, n_ctx):
    bsz, n_tok, _ = u.shape
    n_chunks = n_tok // S5_CHUNK
    rows = n_chunks * bsz
    width = S5_CHUNK * SSM_GROUP
    ug = u.reshape(bsz, n_chunks, S5_CHUNK, SSM_GROUPS, SSM_GROUP)
    ug = ug.transpose(3, 1, 0, 2, 4).reshape(SSM_GROUPS, rows, width)
    ldt = jnp.broadcast_to(log_dt[..., None], a_re.shape)
    arow = jnp.stack([a_re, a_im, ldt], axis=2)[:, :, :, None, :]
    acol = jnp.stack([a_re, a_im, ldt], axis=2)[..., None]
    bmat = jnp.stack([b_re, b_im], axis=2)
    cmat = jnp.stack([c_re, c_im], axis=2)
    ctmat = jnp.swapaxes(cmat, -1, -2)
    grp = lambda a: pl.BlockSpec((2, 1) + a.shape[2:], lambda g: (0, g) + (0,) * (a.ndim - 2))
    state = pltpu.VMEM((rows, SSM_STATE), jnp.float32)
    y = pl.pallas_call(
        functools.partial(_s5_kernel, bsz, n_ctx // S5_CHUNK, n_chunks),
        grid=(SSM_GROUPS,),
        in_specs=[pl.BlockSpec((1, rows, width), lambda g: (g, 0, 0)),
                  grp(arow), grp(acol), grp(bmat), grp(cmat), grp(ctmat)],
        out_specs=pl.BlockSpec((1, rows, width), lambda g: (g, 0, 0)),
        out_shape=jax.ShapeDtypeStruct((SSM_GROUPS, rows, width), jnp.float32),
        scratch_shapes=[state] * 8,
        compiler_params=pltpu.CompilerParams(vmem_limit_bytes=VMEM_LIMIT),
        name="s5_scan",
    )(ug, arow, acol, bmat, cmat, ctmat)
    y = y.reshape(SSM_GROUPS, n_chunks, bsz, S5_CHUNK, SSM_GROUP)
    return y.transpose(2, 1, 3, 0, 4).reshape(bsz, n_tok, SSM_WIDTH)


def _merge_kernel(ctx_blocks, h_ref, mod_ref, attn_ref, y_ref, u_ref, ga_ref, gb_ref,
                  dskip_ref, wglu_ref, bglu_ref, wba_ref, wbs_ref, wo_ref, o_ref):
    f32, bf16 = jnp.float32, jnp.bfloat16
    b, j = pl.program_id(0), pl.program_id(1)
    row = _mod_row(b, j, ctx_blocks)
    gate = mod_ref[pl.ds(row, 1), 2 * D_MODEL:3 * D_MODEL]
    y = jax.nn.gelu(y_ref[0] + dskip_ref[...] * u_ref[0])
    glu = jnp.dot(y.astype(bf16), wglu_ref[...], preferred_element_type=f32) + bglu_ref[...]
    ssm = y * jax.nn.sigmoid(glu)
    ba = jnp.dot(attn_ref[0], wba_ref[...], preferred_element_type=f32)
    bs = jnp.dot(ssm.astype(bf16), wbs_ref[...], preferred_element_type=f32)
    mix = jax.nn.sigmoid(ga_ref[0]) * ba + jax.nn.sigmoid(gb_ref[0]) * bs
    out = jnp.dot(mix.astype(bf16), wo_ref[...], preferred_element_type=f32)
    o_ref[0] = h_ref[0] + gate * out


def branch_merge(h, mod, attn, y, u, ga, gb, d_skip, w_glu, b_glu, w_ba, w_bs, w_o, n_ctx):
    bsz, n_tok, _ = h.shape
    nblk = n_tok // ROW_BLOCK
    row_spec = lambda w: pl.BlockSpec((1, ROW_BLOCK, w), lambda b, j: (b, j, 0))
    full = lambda a: pl.BlockSpec(a.shape, lambda b, j: (0,) * a.ndim)
    return pl.pallas_call(
        functools.partial(_merge_kernel, n_ctx // ROW_BLOCK),
        grid=(bsz, nblk),
        in_specs=[row_spec(D_MODEL), full(mod), row_spec(ATTN_WIDTH), row_spec(SSM_WIDTH),
                  row_spec(SSM_WIDTH), row_spec(D_MODEL), row_spec(D_MODEL),
                  full(d_skip), full(w_glu), full(b_glu), full(w_ba), full(w_bs), full(w_o)],
        out_specs=row_spec(D_MODEL),
        out_shape=jax.ShapeDtypeStruct(h.shape, jnp.float32),
        compiler_params=pltpu.CompilerParams(vmem_limit_bytes=VMEM_LIMIT),
        name="branch_merge",
    )(h, mod, attn, y, u, ga, gb, d_skip, w_glu, b_glu, w_ba, w_bs, w_o)


def _top16(jobs, vals_ref, pay_ref):
    rows = [lax.broadcasted_iota(jnp.int32, s.shape, 0).astype(jnp.float32) for s, _, _ in jobs]

    def body(r, carry):
        nxt = []
        for s, (s0, payload, base), row in zip(carry, jobs, rows):
            m = jnp.max(s, axis=0, keepdims=True)
            pos = jnp.min(jnp.where(s == m, row, float(s0.shape[0])), axis=0, keepdims=True)
            hit = row == pos
            vals_ref[pl.ds(base + r, 1), :] = m
            if payload is None:
                pay_ref[pl.ds(base + r, 1), :] = pos
            else:
                pay_ref[pl.ds(base + r, 1), :] = jnp.max(jnp.where(hit, payload, -1.0), axis=0,
                                                         keepdims=True)
            nxt.append(jnp.where(hit, -jnp.inf, s))
        return tuple(nxt)

    lax.fori_loop(0, PEER_TOPK, body, tuple(s for s, _, _ in jobs))


def _peer_query_kernel(ctx_blocks, h_ref, mod_ref, g_ref, wq_ref, keys_ref,
                       xw_ref, idx_ref, gate_ref, q_scr, sv, si, bv, be):
    f32, bf16 = jnp.float32, jnp.bfloat16
    b, j = pl.program_id(0), pl.program_id(1)
    row = _mod_row(b, j, ctx_blocks)
    shift = mod_ref[pl.ds(row, 1), 3 * D_MODEL:4 * D_MODEL]
    scale = mod_ref[pl.ds(row, 1), 4 * D_MODEL:5 * D_MODEL]
    x = h_ref[0]
    xn = x * lax.rsqrt(jnp.mean(x * x, axis=-1, keepdims=True) + EPS) * g_ref[...]
    xm = xn * (1.0 + scale) + shift
    xb = xm.astype(bf16)
    bits = lax.bitcast_convert_type(xb.astype(f32), jnp.int32)
    for r in range(EXPERT_ROWS):
        lo = (bits[:, (2 * r) * 128:(2 * r + 1) * 128] >> 16) & 0xFFFF
        hi = bits[:, (2 * r + 1) * 128:(2 * r + 2) * 128] & jnp.int32(-65536)
        xw_ref[r] = lo | hi
    for hp in range(2 * PEER_HEADS):
        lo = hp * PEER_HALF
        q_scr[hp] = jnp.dot(xb, wq_ref[:, lo:lo + PEER_HALF],
                            preferred_element_type=f32).astype(bf16)

    half = PEER_TOPK // 2

    def candidates(s1, s2, i1, i2):
        cand, cexp = [], []
        for i in range(half):
            size = PEER_TOPK if i == 0 else half
            blk = s1[i:i + 1, :] + s2[0:size, :]
            keep = PEER_TOPK // (i + 1)
            if keep < size:
                jrow = lax.broadcasted_iota(jnp.int32, (size, 1), 0)
                blk = jnp.where(jrow < keep, blk, -jnp.inf)
            cand.append(blk)
            cexp.append(i1[i:i + 1, :] * PEER_N_KEYS + i2[0:size, :])
        cand.append(s1[half:PEER_TOPK, :] + s2[0:1, :])
        cexp.append(i1[half:PEER_TOPK, :] * PEER_N_KEYS + i2[0:1, :])
        return jnp.concatenate(cand, axis=0), jnp.concatenate(cexp, axis=0)

    def head_pair_body(hp, carry):
        for d in range(2):
            _top16([(lax.dot_general(keys_ref[p], q_scr[4 * hp + 2 * d + p], _NT,
                                     preferred_element_type=f32),
                     None, (2 * d + p) * PEER_TOPK) for p in range(2)], sv, si)
        jobs = []
        for d in range(2):
            lo = 2 * d * PEER_TOPK
            s1, s2 = sv[lo:lo + PEER_TOPK, :], sv[lo + PEER_TOPK:lo + 2 * PEER_TOPK, :]
            i1, i2 = si[lo:lo + PEER_TOPK, :], si[lo + PEER_TOPK:lo + 2 * PEER_TOPK, :]
            cand, cexp = candidates(s1, s2, i1, i2)
            jobs.append((cand, cexp, pl.multiple_of((2 * hp + d) * PEER_TOPK, PEER_TOPK)))
        _top16(jobs, bv, be)
        return carry

    lax.fori_loop(0, PEER_HEADS // 2, head_pair_body, 0)

    for hd in range(PEER_HEADS):
        best = bv[hd * PEER_TOPK:(hd + 1) * PEER_TOPK, :]
        e = jnp.exp(best - jnp.max(best, axis=0, keepdims=True))
        bv[hd * PEER_TOPK:(hd + 1) * PEER_TOPK, :] = e / jnp.sum(e, axis=0, keepdims=True)
    for t in range(ROW_BLOCK // PEER_TOK):
        idx_ref[t] = (be[:, t * PEER_TOK:(t + 1) * PEER_TOK].T * EXPERT_ROWS).astype(jnp.int32)
        gate_ref[t] = bv[:, t * PEER_TOK:(t + 1) * PEER_TOK]


def peer_query(h, mod, g, wq_bf16, keys_bf16, n_ctx):
    bsz, n_tok, _ = h.shape
    nblk = n_tok // ROW_BLOCK
    sub = ROW_BLOCK // PEER_TOK
    n_pblk = bsz * n_tok // PEER_TOK
    row_spec = pl.BlockSpec((1, ROW_BLOCK, D_MODEL), lambda b, j: (b, j, 0))
    full = lambda a: pl.BlockSpec(a.shape, lambda b, j: (0,) * a.ndim)
    pick_spec = pl.BlockSpec((sub, PEER_PICKS, PEER_TOK), lambda b, j: (b * nblk + j, 0, 0))
    f32, i32 = jnp.float32, jnp.int32
    return pl.pallas_call(
        functools.partial(_peer_query_kernel, n_ctx // ROW_BLOCK),
        grid=(bsz, nblk),
        in_specs=[row_spec, full(mod), full(g), full(wq_bf16), full(keys_bf16)],
        out_specs=[pl.BlockSpec((EXPERT_ROWS, ROW_BLOCK, 128), lambda b, j: (0, b * nblk + j, 0)),
                   pick_spec, pick_spec],
        out_shape=[jax.ShapeDtypeStruct((EXPERT_ROWS, bsz * n_tok, 128), i32),
                   jax.ShapeDtypeStruct((n_pblk, PEER_PICKS, PEER_TOK), i32),
                   jax.ShapeDtypeStruct((n_pblk, PEER_PICKS, PEER_TOK), f32)],
        scratch_shapes=[pltpu.VMEM((2 * PEER_HEADS, ROW_BLOCK, PEER_HALF), jnp.bfloat16),
                        pltpu.VMEM((4 * PEER_TOPK, ROW_BLOCK), f32),
                        pltpu.VMEM((4 * PEER_TOPK, ROW_BLOCK), f32),
                        pltpu.VMEM((PEER_PICKS, ROW_BLOCK), f32),
                        pltpu.VMEM((PEER_PICKS, ROW_BLOCK), f32)],
        compiler_params=pltpu.CompilerParams(vmem_limit_bytes=VMEM_LIMIT),
        name="peer_query",
    )(h, mod, g, wq_bf16, keys_bf16)


def pack_rows(rows):
    bits = lax.bitcast_convert_type(rows.astype(jnp.bfloat16), jnp.uint16).astype(jnp.uint32)
    bits = bits.reshape(-1, EXPERT_ROWS, 2, 128)
    words = bits[:, :, 0, :] | (bits[:, :, 1, :] << 16)
    return lax.bitcast_convert_type(words, jnp.int32)


def _group_chunks(idx_ref, tab_ref, r0):
    rows = [idx_ref.at[0, 0, pl.ds((r0 + a) * PEER_PICKS, PEER_PICKS)] for a in range(PEER_STAGE)]
    for c in range(PEER_PICKS // PEER_CHUNK):
        tiles = [[None] * PEER_CHUNK for _ in range(PEER_STAGE)]
        for j in range(PEER_CHUNK):
            for a in range(PEER_STAGE):
                off = pl.multiple_of(rows[a][c * PEER_CHUNK + j], EXPERT_ROWS)
                tiles[a][j] = tab_ref[pl.ds(off, EXPERT_ROWS), :]
        yield c, [jnp.concatenate(tiles[a], axis=0) for a in range(PEER_STAGE)]


def _pick_rows_mask():
    tile_rows = 2 * EXPERT_ROWS
    shape = (PEER_PICKS, PEER_PICKS * tile_rows)
    return (lax.broadcasted_iota(jnp.int32, shape, 1) // tile_rows
            == lax.broadcasted_iota(jnp.int32, shape, 0))


def _peer_score_kernel(idx_ref, x_ref, gate_ref, tab_ref, w_ref):
    f32, bf16, i32 = jnp.float32, jnp.bfloat16, jnp.int32
    sel = _pick_rows_mask().astype(bf16)
    lane = lax.broadcasted_iota(i32, (PEER_PICKS, PEER_TOK), 1)
    cols = PEER_CHUNK * 2 * EXPERT_ROWS

    def group_body(g, acc):
        r0 = pl.multiple_of(g * PEER_STAGE, PEER_STAGE)
        xg = x_ref[:, pl.ds(r0, PEER_STAGE), :]
        xts = []
        for a in range(PEER_STAGE):
            xw = jnp.concatenate([xg[r, a:a + 1, :] for r in range(EXPERT_ROWS)], axis=0)
            xts.append(pltpu.bitcast(jnp.concatenate([xw] * PEER_CHUNK, axis=0), bf16))
        parts = [[] for _ in range(PEER_STAGE)]
        for c, words in _group_chunks(idx_ref, tab_ref, r0):
            for a in range(PEER_STAGE):
                prod = pltpu.bitcast(words[a], bf16) * xts[a]
                parts[a].append(jnp.dot(sel[0:PEER_CHUNK, 0:cols], prod,
                                        preferred_element_type=f32))
        for a in range(PEER_STAGE):
            part = jnp.concatenate(parts[a], axis=0)
            acc = jnp.where(lane == r0 + a, jnp.sum(part, axis=1, keepdims=True), acc)
        return acc

    scores = lax.fori_loop(0, PEER_TOK // PEER_STAGE, group_body,
                           jnp.zeros((PEER_PICKS, PEER_TOK), f32))
    w = gate_ref[0] * jax.nn.gelu(scores)
    w_ref[...] = jnp.dot(w.T.astype(bf16), sel, preferred_element_type=f32)


def _peer_mix_kernel(idx_ref, wexp_ref, h_ref, gate_ref, tab_ref, o_ref):
    f32, bf16, i32 = jnp.float32, jnp.bfloat16, jnp.int32
    tile_rows = 2 * EXPERT_ROWS
    shape = (tile_rows, PEER_PICKS * tile_rows)
    diag = (lax.broadcasted_iota(i32, shape, 1) % tile_rows
            == lax.broadcasted_iota(i32, shape, 0))
    cols = PEER_CHUNK * tile_rows

    def group_body(g, carry):
        r0 = pl.multiple_of(g * PEER_STAGE, PEER_STAGE)
        tiles = []
        wmats = [jnp.where(diag, wexp_ref[pl.ds(r0 + a, 1), :], 0.0).astype(bf16)
                 for a in range(PEER_STAGE)]
        tiles = [jnp.zeros((tile_rows, 128), f32)] * PEER_STAGE
        for c, words in _group_chunks(idx_ref, tab_ref, r0):
            tiles = [tiles[a] + jnp.dot(wmats[a][:, c * cols:(c + 1) * cols],
                                        pltpu.bitcast(words[a], bf16),
                                        preferred_element_type=f32) for a in range(PEER_STAGE)]
        rows = jnp.concatenate(
            [jnp.concatenate([tile[r:r + 1, :] for tile in tiles], axis=0)
             for r in range(tile_rows)], axis=1)
        o_ref[pl.ds(r0, PEER_STAGE), :] = (h_ref[pl.ds(r0, PEER_STAGE), :]
                                           + gate_ref[0, 0] * rows)
        return carry

    lax.fori_loop(0, PEER_TOK // PEER_STAGE, group_body, 0)


def _table_spec(tab):
    return pl.BlockSpec(tab.shape, lambda i: (0,) * tab.ndim, pipeline_mode=pl.Buffered(1))


def peer_experts(h, xw, idx, gates, mod, u_packed, v_packed, n_ctx):
    bsz, n_tok, _ = h.shape
    n_pblk = idx.shape[0]
    blk_per_batch = n_tok // PEER_TOK
    ctx_pblk = n_ctx // PEER_TOK
    flat = PEER_PICKS * PEER_TOK
    smem_spec = pl.BlockSpec((1, 1, flat), lambda i: (i, 0, 0), memory_space=pltpu.SMEM)
    pick_spec = pl.BlockSpec((1, PEER_PICKS, PEER_TOK), lambda i: (i, 0, 0))
    word_spec = pl.BlockSpec((EXPERT_ROWS, PEER_TOK, 128), lambda i: (0, i, 0))
    row_spec = pl.BlockSpec((PEER_TOK, D_MODEL), lambda i: (i, 0))
    idx_flat = idx.reshape(n_pblk, 1, flat)
    wexp = pl.pallas_call(
        _peer_score_kernel,
        grid=(n_pblk,),
        in_specs=[smem_spec, word_spec, pick_spec, _table_spec(u_packed)],
        out_specs=row_spec,
        out_shape=jax.ShapeDtypeStruct((bsz * n_tok, D_MODEL), jnp.float32),
        compiler_params=pltpu.CompilerParams(vmem_limit_bytes=VMEM_LIMIT),
        name="peer_scores",
    )(idx_flat, xw, gates, u_packed)

    def gate_map(i):
        bi, ji = i // blk_per_batch, i % blk_per_batch
        return (jnp.where(ji < ctx_pblk, 0, bi + 1), N_MOD - 1, 0, 0)

    out = pl.pallas_call(
        _peer_mix_kernel,
        grid=(n_pblk,),
        in_specs=[smem_spec, row_spec, row_spec,
                  pl.BlockSpec((1, 1, 1, D_MODEL), gate_map), _table_spec(v_packed)],
        out_specs=row_spec,
        out_shape=jax.ShapeDtypeStruct((bsz * n_tok, D_MODEL), jnp.float32),
        compiler_params=pltpu.CompilerParams(vmem_limit_bytes=VMEM_LIMIT),
        name="peer_mix",
    )(idx_flat, wexp, h.reshape(-1, D_MODEL), mod.reshape(MOD_ROWS, N_MOD, 1, D_MODEL), v_packed)
    return out.reshape(h.shape)


def kernel(x, c, ctx, c_ctx, w_mod, b_mod, norm_mix_g, w_in, q_norm_g, k_norm_g, ssm_a_re, ssm_a_im, ssm_log_dt, ssm_b_re, ssm_b_im, ssm_c_re, ssm_c_im, ssm_d, w_glu, b_glu, w_branch_attn, w_branch_ssm, w_out, norm_ffn_g, peer_w_q, peer_keys, peer_u, peer_v):
    bf16 = jnp.bfloat16
    bsz, n_lat, _ = x.shape
    n_ctx = ctx.shape[1]
    depth = w_mod.shape[0]
    assert bsz + 1 <= MOD_ROWS and n_ctx % ROW_BLOCK == 0 and n_lat % ROW_BLOCK == 0

    cc = jnp.zeros((MOD_ROWS, D_MODEL), jnp.float32).at[0].set(c_ctx).at[1:bsz + 1].set(c)
    mods = modulation(cc, w_mod, b_mod)
    rope = rope_tables(n_ctx, n_lat)
    h = jnp.concatenate([ctx, x], axis=1)
    row = lambda a: a.reshape(1, -1)

    for i in range(depth):
        mod = mods[i]
        q, k, v, u, ga, gb = in_projection(h, mod, row(norm_mix_g[i]), w_in[i].astype(bf16), rope,
                                           row(q_norm_g[i]), row(k_norm_g[i]), n_ctx)
        attn = attention(q, k, v, n_ctx)
        y = s5_scan(u.astype(bf16), ssm_a_re[i], ssm_a_im[i], ssm_log_dt[i],
                    ssm_b_re[i], ssm_b_im[i], ssm_c_re[i], ssm_c_im[i], n_ctx)
        h = branch_merge(h, mod, attn, y, u, ga, gb, row(ssm_d[i]), w_glu[i].astype(bf16),
                         row(b_glu[i]), w_branch_attn[i].astype(bf16),
                         w_branch_ssm[i].astype(bf16), w_out[i].astype(bf16), n_ctx)
        n_peer_ctx = n_ctx
        if i == depth - 1:
            h, n_peer_ctx = h[:, n_ctx:, :], 0
        xw, idx, gates = peer_query(h, mod, row(norm_ffn_g[i]), peer_w_q[i].astype(bf16),
                                    peer_keys[i].astype(bf16), n_peer_ctx)
        h = peer_experts(h, xw, idx, gates, mod, pack_rows(peer_u[i]).reshape(-1, 128),
                         pack_rows(peer_v[i]).reshape(-1, 128), n_peer_ctx)
    return h
```

```python
import functools
import math

import jax
import jax.numpy as jnp
from jax import lax
from jax.experimental import pallas as pl
from jax.experimental.pallas import tpu as pltpu

D_MODEL = 1024
GRID_W = 64
HEAD_DIM = 128
N_Q_HEADS = 8
N_KV_HEADS = 2
GQA_GROUP = N_Q_HEADS // N_KV_HEADS
ATTN_WIDTH = N_Q_HEADS * HEAD_DIM
KV_WIDTH = N_KV_HEADS * HEAD_DIM
ROPE_THETA = 10000.0
ROPE_FREQS = HEAD_DIM // 4
ATTN_SCALE = HEAD_DIM ** -0.5
SSM_WIDTH = D_MODEL // 2
SSM_GROUP = 16
SSM_GROUPS = SSM_WIDTH // SSM_GROUP
SSM_STATE = 64
IN_WIDTH = ATTN_WIDTH + 2 * KV_WIDTH + SSM_WIDTH + 2 * D_MODEL
PEER_HEADS = 8
PEER_N_KEYS = 128
PEER_N_EXPERTS = PEER_N_KEYS ** 2
PEER_QUERY_DIM = 256
PEER_HALF = PEER_QUERY_DIM // 2
PEER_TOPK = 16
PEER_PICKS = PEER_HEADS * PEER_TOPK
N_MOD = 6
EPS = 1e-6

ROW_BLOCK = 256
ATTN_KEYS = 4096
MOD_ROWS = 16
S5_CHUNK = 16
PEER_TOK = 128
EXPERT_ROWS = 4
PEER_STAGE = 128
PEER_CHUNK = 16
VMEM_LIMIT = 56 * 1024 * 1024

_NT = (((1,), (1,)), ((), ()))


def _mod_row(b, j, ctx_blocks):
    return jnp.where(j < ctx_blocks, 0, b + 1)


def _mod_kernel(c_ref, w_ref, b_ref, o_ref):
    c = c_ref[...]
    s = c * jax.nn.sigmoid(c)
    o_ref[0] = jnp.dot(s.astype(jnp.bfloat16), w_ref[0].astype(jnp.bfloat16),
                       preferred_element_type=jnp.float32) + b_ref[0]


def modulation(cc, w_mod, b_mod):
    depth = w_mod.shape[0]
    nblk = 1536
    width = N_MOD * D_MODEL
    return pl.pallas_call(
        _mod_kernel,
        grid=(depth, width // nblk),
        in_specs=[pl.BlockSpec((MOD_ROWS, D_MODEL), lambda l, n: (0, 0)),
                  pl.BlockSpec((1, D_MODEL, nblk), lambda l, n: (l, 0, n)),
                  pl.BlockSpec((1, 1, nblk), lambda l, n: (l, 0, n))],
        out_specs=pl.BlockSpec((1, MOD_ROWS, nblk), lambda l, n: (l, 0, n)),
        out_shape=jax.ShapeDtypeStruct((depth, MOD_ROWS, width), jnp.float32),
        name="modulation",
    )(cc, w_mod, b_mod.reshape(depth, 1, width))


def _inproj_kernel(ctx_blocks, h_ref, mod_ref, g_ref, w_ref, cos_ref, sa_ref, sb_ref,
                   qg_ref, kg_ref, q_ref, k_ref, v_ref, u_ref, ga_ref, gb_ref):
    b, j = pl.program_id(0), pl.program_id(1)
    row = _mod_row(b, j, ctx_blocks)
    shift = mod_ref[pl.ds(row, 1), 0:D_MODEL]
    scale = mod_ref[pl.ds(row, 1), D_MODEL:2 * D_MODEL]
    x = h_ref[0]
    xn = x * lax.rsqrt(jnp.mean(x * x, axis=-1, keepdims=True) + EPS) * g_ref[...]
    xm = (xn * (1.0 + scale) + shift).astype(jnp.bfloat16)

    cos, sa, sb = cos_ref[...], sa_ref[...], sb_ref[...]

    def norm_rope(z, g, out_scale):
        zn = z * lax.rsqrt(jnp.mean(z * z, axis=-1, keepdims=True) + EPS) * g
        r = zn * cos + pltpu.roll(zn, 96, 1) * sa + pltpu.roll(zn, 32, 1) * sb
        return r * out_scale

    for hd in range(N_Q_HEADS):
        lo = hd * HEAD_DIM
        z = jnp.dot(xm, w_ref[:, lo:lo + HEAD_DIM], preferred_element_type=jnp.float32)
        q_ref[0, :, lo:lo + HEAD_DIM] = norm_rope(z, qg_ref[...], ATTN_SCALE * math.log2(math.e)).astype(q_ref.dtype)
    for hd in range(N_KV_HEADS):
        lo = hd * HEAD_DIM
        z = jnp.dot(xm, w_ref[:, ATTN_WIDTH + lo:ATTN_WIDTH + lo + HEAD_DIM],
                    preferred_element_type=jnp.float32)
        k_ref[0, :, lo:lo + HEAD_DIM] = norm_rope(z, kg_ref[...], 1.0).astype(k_ref.dtype)
    off = ATTN_WIDTH + KV_WIDTH
    v_ref[0] = jnp.dot(xm, w_ref[:, off:off + KV_WIDTH],
                       preferred_element_type=jnp.float32).astype(v_ref.dtype)
    off += KV_WIDTH
    u_ref[0] = jnp.dot(xm, w_ref[:, off:off + SSM_WIDTH], preferred_element_type=jnp.float32)
    off += SSM_WIDTH
    ga_ref[0] = jnp.dot(xm, w_ref[:, off:off + D_MODEL], preferred_element_type=jnp.float32)
    off += D_MODEL
    gb_ref[0] = jnp.dot(xm, w_ref[:, off:off + D_MODEL], preferred_element_type=jnp.float32)


def in_projection(h, mod, g, w_in_bf16, rope, qg, kg, n_ctx):
    bsz, n_tok, _ = h.shape
    nblk = n_tok // ROW_BLOCK
    cos, sa, sb = rope
    row_spec = lambda w: pl.BlockSpec((1, ROW_BLOCK, w), lambda b, j: (b, j, 0))
    full = lambda a: pl.BlockSpec(a.shape, lambda b, j: (0,) * a.ndim)
    rope_spec = pl.BlockSpec((ROW_BLOCK, HEAD_DIM), lambda b, j: (j, 0))
    f32, bf16 = jnp.float32, jnp.bfloat16
    out_shape = [jax.ShapeDtypeStruct((bsz, n_tok, ATTN_WIDTH), bf16),
                 jax.ShapeDtypeStruct((bsz, n_tok, KV_WIDTH), bf16),
                 jax.ShapeDtypeStruct((bsz, n_tok, KV_WIDTH), bf16),
                 jax.ShapeDtypeStruct((bsz, n_tok, SSM_WIDTH), f32),
                 jax.ShapeDtypeStruct((bsz, n_tok, D_MODEL), f32),
                 jax.ShapeDtypeStruct((bsz, n_tok, D_MODEL), f32)]
    return pl.pallas_call(
        functools.partial(_inproj_kernel, n_ctx // ROW_BLOCK),
        grid=(bsz, nblk),
        in_specs=[row_spec(D_MODEL), full(mod), full(g), full(w_in_bf16),
                  rope_spec, rope_spec, rope_spec, full(qg), full(kg)],
        out_specs=[row_spec(ATTN_WIDTH), row_spec(KV_WIDTH), row_spec(KV_WIDTH),
                   row_spec(SSM_WIDTH), row_spec(D_MODEL), row_spec(D_MODEL)],
        out_shape=out_shape,
        compiler_params=pltpu.CompilerParams(vmem_limit_bytes=VMEM_LIMIT),
        name="in_projection",
    )(h, mod, g, w_in_bf16, cos, sa, sb, qg, kg)


def rope_tables(n_ctx, n_lat):
    pos = jnp.arange(n_lat)
    inv_freq = ROPE_THETA ** (-jnp.arange(ROPE_FREQS, dtype=jnp.float32) / ROPE_FREQS)
    ang_row = (pos // GRID_W).astype(jnp.float32)[:, None] * inv_freq
    ang_col = (pos % GRID_W).astype(jnp.float32)[:, None] * inv_freq
    ang = jnp.concatenate([ang_row, ang_row, ang_col, ang_col], axis=-1)
    cos, sin = jnp.cos(ang), jnp.sin(ang)
    even = ((jnp.arange(HEAD_DIM) // ROPE_FREQS) % 2 == 0)[None, :]
    sa = jnp.where(even, -sin, 0.0)
    sb = jnp.where(even, 0.0, sin)
    pad = lambda t, v: jnp.concatenate([jnp.full((n_ctx, HEAD_DIM), v, jnp.float32), t], axis=0)
    return pad(cos, 1.0), pad(sa, 0.0), pad(sb, 0.0)


def _attn_kernel(n_ctx, n_lat, key_chunk, ctx_blocks, q_ref, k_ref, v_ref, o_ref, m_scr, l_scr, acc_scr):
    f32 = jnp.float32
    j = pl.program_id(2)
    q = jnp.concatenate([q_ref[0, :, hd * HEAD_DIM:(hd + 1) * HEAD_DIM]
                         for hd in range(GQA_GROUP)], axis=0)

    kc, vc = k_ref[0, 0:n_ctx, :], v_ref[0, 0:n_ctx, :]
    s = lax.dot_general(q, kc, _NT, preferred_element_type=f32)
    m = jnp.max(s, axis=-1, keepdims=True)
    p = jnp.exp2(s - m)
    m_scr[...] = m
    l_scr[...] = jnp.sum(p, axis=-1, keepdims=True)
    acc_scr[...] = jnp.dot(p.astype(vc.dtype), vc, preferred_element_type=f32)

    @pl.when(j >= ctx_blocks)
    def _():
        def body(c, carry):
            r0 = pl.multiple_of(n_ctx + c * key_chunk, ROW_BLOCK)
            kc = k_ref[0, pl.ds(r0, key_chunk), :]
            vc = v_ref[0, pl.ds(r0, key_chunk), :]
            s = lax.dot_general(q, kc, _NT, preferred_element_type=f32)
            m_old = m_scr[...]
            m_new = jnp.maximum(m_old, jnp.max(s, axis=-1, keepdims=True))
            alpha = jnp.exp2(m_old - m_new)
            p = jnp.exp2(s - m_new)
            m_scr[...] = m_new
            l_scr[...] = alpha * l_scr[...] + jnp.sum(p, axis=-1, keepdims=True)
            acc_scr[...] = alpha * acc_scr[...] + jnp.dot(p.astype(vc.dtype), vc,
                                                          preferred_element_type=f32)
            return carry

        lax.fori_loop(0, n_lat // key_chunk, body, 0)

    out = acc_scr[...] / l_scr[...]
    for hd in range(GQA_GROUP):
        o_ref[0, :, hd * HEAD_DIM:(hd + 1) * HEAD_DIM] = (
            out[hd * ROW_BLOCK:(hd + 1) * ROW_BLOCK]).astype(o_ref.dtype)


def attention(q, k, v, n_ctx):
    bsz, n_tok, _ = q.shape
    nblk = n_tok // ROW_BLOCK
    gw = GQA_GROUP * HEAD_DIM
    rows = GQA_GROUP * ROW_BLOCK
    key_chunk = math.gcd(n_tok - n_ctx, ATTN_KEYS)
    return pl.pallas_call(
        functools.partial(_attn_kernel, n_ctx, n_tok - n_ctx, key_chunk, n_ctx // ROW_BLOCK),
        grid=(bsz, N_KV_HEADS, nblk),
        in_specs=[pl.BlockSpec((1, ROW_BLOCK, gw), lambda b, g, j: (b, j, g)),
                  pl.BlockSpec((1, n_tok, HEAD_DIM), lambda b, g, j: (b, 0, g)),
                  pl.BlockSpec((1, n_tok, HEAD_DIM), lambda b, g, j: (b, 0, g))],
        out_specs=pl.BlockSpec((1, ROW_BLOCK, gw), lambda b, g, j: (b, j, g)),
        out_shape=jax.ShapeDtypeStruct((bsz, n_tok, ATTN_WIDTH), jnp.bfloat16),
        scratch_shapes=[pltpu.VMEM((rows, 1), jnp.float32), pltpu.VMEM((rows, 1), jnp.float32),
                        pltpu.VMEM((rows, HEAD_DIM), jnp.float32)],
        compiler_params=pltpu.CompilerParams(vmem_limit_bytes=VMEM_LIMIT),
        name="attention",
    )(q, k, v)


def _cpow(e, lam_re_dt, lam_im_dt):
    mag = jnp.exp(e * lam_re_dt)
    ang = e * lam_im_dt
    return mag * jnp.cos(ang), mag * jnp.sin(ang)


def _s5_kernel(n_batch, ctx_chunks, n_chunks,
               u_ref, arow_ref, acol_ref, b_ref, c_ref, ct_ref, y_ref,
               sfr, sfi, sbr, sbi, hfr, hfi, hbr, hbi, yacc):
    f32, bf16 = jnp.float32, jnp.bfloat16
    width = S5_CHUNK * SSM_GROUP
    lane = lax.broadcasted_iota(jnp.int32, (1, width), 1)
    lane_blk = lane // SSM_GROUP
    expand = (lax.broadcasted_iota(jnp.int32, (SSM_GROUP, width), 1) % SSM_GROUP
              == lax.broadcasted_iota(jnp.int32, (SSM_GROUP, width), 0)).astype(f32)
    rowi = lax.broadcasted_iota(jnp.int32, (width, 1), 0) // SSM_GROUP
    u = u_ref[0].astype(bf16)

    m_tot = jnp.zeros((width, width), f32)
    proj, read, step = [], [], []
    for d in range(2):
        fwd = d == 0
        are_c, aim_c, dt_c = acol_ref[d, 0, 0], acol_ref[d, 0, 1], jnp.exp(acol_ref[d, 0, 2])
        lr_c, li_c = are_c * dt_c, aim_c * dt_c
        abr, abi = _cpow(1.0, lr_c, li_c)
        den = are_c * are_c + aim_c * aim_c
        nr, ni = abr - 1.0, abi
        k_r = (nr * are_c + ni * aim_c) / den
        k_i = (ni * are_c - nr * aim_c) / den
        b_re, b_im = b_ref[d, 0, 0], b_ref[d, 0, 1]
        bb_r = k_r * b_re - k_i * b_im
        bb_i = k_r * b_im + k_i * b_re
        bt_r = jnp.dot(bb_r, expand, preferred_element_type=f32)
        bt_i = jnp.dot(bb_i, expand, preferred_element_type=f32)
        e_in = (S5_CHUNK - 1 - lane_blk if fwd else lane_blk).astype(f32)
        ap_r, ap_i = _cpow(e_in, lr_c, li_c)
        proj.append(((ap_r * bt_r - ap_i * bt_i).astype(bf16),
                     (ap_r * bt_i + ap_i * bt_r).astype(bf16)))
        ct_r = jnp.dot(ct_ref[d, 0, 0], expand, preferred_element_type=f32)
        ct_i = jnp.dot(ct_ref[d, 0, 1], expand, preferred_element_type=f32)
        e_out = (lane_blk + 1 if fwd else S5_CHUNK - lane_blk).astype(f32)
        aq_r, aq_i = _cpow(e_out, lr_c, li_c)
        read.append(((ct_r * aq_r - ct_i * aq_i).astype(bf16),
                     (-(ct_r * aq_i + ct_i * aq_r)).astype(bf16)))
        are_r, aim_r, dt_r = arow_ref[d, 0, 0], arow_ref[d, 0, 1], jnp.exp(arow_ref[d, 0, 2])
        lr_r, li_r = are_r * dt_r, aim_r * dt_r
        step.append(_cpow(float(S5_CHUNK), lr_r, li_r))
        lag = (rowi if fwd else S5_CHUNK - 1 - rowi).astype(f32)
        al_r, al_i = _cpow(lag, lr_r, li_r)
        c_re = jnp.concatenate([c_ref[d, 0, 0]] * S5_CHUNK, axis=0)
        c_im = jnp.concatenate([c_ref[d, 0, 1]] * S5_CHUNK, axis=0)
        ca_r = c_re * al_r - c_im * al_i
        ca_i = c_re * al_i + c_im * al_r
        kmat = (jnp.dot(ca_r, bb_r, preferred_element_type=f32)
                - jnp.dot(ca_i, bb_i, preferred_element_type=f32))
        kt = jnp.dot(kmat, expand, preferred_element_type=f32)
        for s in range(S5_CHUNK):
            sh = (s if fwd else S5_CHUNK - 1 - s) * SSM_GROUP
            if sh == 0:
                shifted = kt
            elif fwd:
                shifted = jnp.concatenate([jnp.zeros((sh, width), f32), kt[:width - sh]], axis=0)
            else:
                shifted = jnp.concatenate([kt[sh:], jnp.zeros((sh, width), f32)], axis=0)
            m_tot = m_tot + jnp.where(lane_blk == s, shifted, 0.0)

    yacc[...] = lax.dot_general(u, m_tot.astype(bf16), _NT, preferred_element_type=f32)
    sfr[...] = lax.dot_general(u, proj[0][0], _NT, preferred_element_type=f32)
    sfi[...] = lax.dot_general(u, proj[0][1], _NT, preferred_element_type=f32)
    sbr[...] = lax.dot_general(u, proj[1][0], _NT, preferred_element_type=f32)
    sbi[...] = lax.dot_general(u, proj[1][1], _NT, preferred_element_type=f32)

    (afr, afi), (abr_, abi_) = step

    def scan_body(i, carry):
        fr, fi, br, bi = carry
        rf = pl.multiple_of(i * n_batch, n_batch)
        hfr[pl.ds(rf, n_batch), :] = fr
        hfi[pl.ds(rf, n_batch), :] = fi
        nfr = afr * fr - afi * fi + sfr[pl.ds(rf, n_batch), :]
        nfi = afr * fi + afi * fr + sfi[pl.ds(rf, n_batch), :]
        cb = jnp.where(i < ctx_chunks, ctx_chunks - 1 - i, n_chunks - 1 + ctx_chunks - i)
        rb = pl.multiple_of(cb * n_batch, n_batch)
        hbr[pl.ds(rb, n_batch), :] = br
        hbi[pl.ds(rb, n_batch), :] = bi
        nbr = abr_ * br - abi_ * bi + sbr[pl.ds(rb, n_batch), :]
        nbi = abr_ * bi + abi_ * br + sbi[pl.ds(rb, n_batch), :]
        return nfr, nfi, nbr, nbi

    z = jnp.zeros((n_batch, SSM_STATE), f32)
    lax.fori_loop(0, n_chunks, scan_body, (z, z, z, z))

    y_ref[0] = (yacc[...]
                + jnp.dot(hfr[...].astype(bf16), read[0][0], preferred_element_type=f32)
                + jnp.dot(hfi[...].astype(bf16), read[0][1], preferred_element_type=f32)
                + jnp.dot(hbr[...].astype(bf16), read[1][0], preferred_element_type=f32)
                + jnp.dot(hbi[...].astype(bf16), read[1][1], preferred_element_type=f32)
                ).astype(y_ref.dtype)


def s5_scan(u, a_re, a_im, log_dt, b_re, b_im, c_re, c_im, n_ctx):
    bsz, n_tok, _ = u.shape
    n_chunks = n_tok // S5_CHUNK
    rows = n_chunks * bsz
    width = S5_CHUNK * SSM_GROUP
    ug = u.reshape(bsz, n_chunks, S5_CHUNK, SSM_GROUPS, SSM_GROUP)
    ug = ug.transpose(3, 1, 0, 2, 4).reshape(SSM_GROUPS, rows, width)
    ldt = jnp.broadcast_to(log_dt[..., None], a_re.shape)
    arow = jnp.stack([a_re, a_im, ldt], axis=2)[:, :, :, None, :]
    acol = jnp.stack([a_re, a_im, ldt], axis=2)[..., None]
    bmat = jnp.stack([b_re, b_im], axis=2)
    cmat = jnp.stack([c_re, c_im], axis=2)
    ctmat = jnp.swapaxes(cmat, -1, -2)
    grp = lambda a: pl.BlockSpec((2, 1) + a.shape[2:], lambda g: (0, g) + (0,) * (a.ndim - 2))
    state = pltpu.VMEM((rows, SSM_STATE), jnp.float32)
    y = pl.pallas_call(
        functools.partial(_s5_kernel, bsz, n_ctx // S5_CHUNK, n_chunks),
        grid=(SSM_GROUPS,),
        in_specs=[pl.BlockSpec((1, rows, width), lambda g: (g, 0, 0)),
                  grp(arow), grp(acol), grp(bmat), grp(cmat), grp(ctmat)],
        out_specs=pl.BlockSpec((1, rows, width), lambda g: (g, 0, 0)),
        out_shape=jax.ShapeDtypeStruct((SSM_GROUPS, rows, width), jnp.bfloat16),
        scratch_shapes=[state] * 8 + [pltpu.VMEM((rows, width), jnp.float32)],
        compiler_params=pltpu.CompilerParams(vmem_limit_bytes=VMEM_LIMIT),
        name="s5_scan",
    )(ug, arow, acol, bmat, cmat, ctmat)
    y = y.reshape(SSM_GROUPS, n_chunks, bsz, S5_CHUNK, SSM_GROUP)
    return y.transpose(2, 1, 3, 0, 4).reshape(bsz, n_tok, SSM_WIDTH)


def _merge_kernel(ctx_blocks, h_ref, mod_ref, attn_ref, y_ref, u_ref, ga_ref, gb_ref,
                  dskip_ref, wglu_ref, bglu_ref, wba_ref, wbs_ref, wo_ref, o_ref):
    f32, bf16 = jnp.float32, jnp.bfloat16
    b, j = pl.program_id(0), pl.program_id(1)
    row = _mod_row(b, j, ctx_blocks)
    gate = mod_ref[pl.ds(row, 1), 2 * D_MODEL:3 * D_MODEL]
    y = jax.nn.gelu(y_ref[0] + dskip_ref[...] * u_ref[0])
    glu = jnp.dot(y.astype(bf16), wglu_ref[...], preferred_element_type=f32) + bglu_ref[...]
    ssm = y * jax.nn.sigmoid(glu)
    ba = jnp.dot(attn_ref[0], wba_ref[...], preferred_element_type=f32)
    bs = jnp.dot(ssm.astype(bf16), wbs_ref[...], preferred_element_type=f32)
    mix = jax.nn.sigmoid(ga_ref[0]) * ba + jax.nn.sigmoid(gb_ref[0]) * bs
    out = jnp.dot(mix.astype(bf16), wo_ref[...], preferred_element_type=f32)
    o_ref[0] = h_ref[0] + gate * out


def branch_merge(h, mod, attn, y, u, ga, gb, d_skip, w_glu, b_glu, w_ba, w_bs, w_o, n_ctx):
    bsz, n_tok, _ = h.shape
    nblk = n_tok // ROW_BLOCK
    row_spec = lambda w: pl.BlockSpec((1, ROW_BLOCK, w), lambda b, j: (b, j, 0))
    full = lambda a: pl.BlockSpec(a.shape, lambda b, j: (0,) * a.ndim)
    return pl.pallas_call(
        functools.partial(_merge_kernel, n_ctx // ROW_BLOCK),
        grid=(bsz, nblk),
        in_specs=[row_spec(D_MODEL), full(mod), row_spec(ATTN_WIDTH), row_spec(SSM_WIDTH),
                  row_spec(SSM_WIDTH), row_spec(D_MODEL), row_spec(D_MODEL),
                  full(d_skip), full(w_glu), full(b_glu), full(w_ba), full(w_bs), full(w_o)],
        out_specs=row_spec(D_MODEL),
        out_shape=jax.ShapeDtypeStruct(h.shape, jnp.float32),
        compiler_params=pltpu.CompilerParams(vmem_limit_bytes=VMEM_LIMIT),
        name="branch_merge",
    )(h, mod, attn, y, u, ga, gb, d_skip, w_glu, b_glu, w_ba, w_bs, w_o)


def _top16(jobs, vals_ref, pay_ref):
    rows = [lax.broadcasted_iota(jnp.int32, s.shape, 0).astype(jnp.float32) for s, _, _ in jobs]

    def body(r, carry):
        nxt = []
        for s, (s0, payload, base), row in zip(carry, jobs, rows):
            m = jnp.max(s, axis=0, keepdims=True)
            pos = jnp.min(jnp.where(s == m, row, float(s0.shape[0])), axis=0, keepdims=True)
            hit = row == pos
            vals_ref[pl.ds(base + r, 1), :] = m
            if payload is None:
                pay_ref[pl.ds(base + r, 1), :] = pos
            else:
                pay_ref[pl.ds(base + r, 1), :] = jnp.max(jnp.where(hit, payload, -1.0), axis=0,
                                                         keepdims=True)
            nxt.append(jnp.where(hit, -jnp.inf, s))
        return tuple(nxt)

    lax.fori_loop(0, PEER_TOPK, body, tuple(s for s, _, _ in jobs))


def _peer_query_kernel(ctx_blocks, h_ref, mod_ref, g_ref, wq_ref, keys_ref,
                       xw_ref, idx_ref, gate_ref, q_scr, sv, si, bv, be):
    f32, bf16 = jnp.float32, jnp.bfloat16
    b, j = pl.program_id(0), pl.program_id(1)
    row = _mod_row(b, j, ctx_blocks)
    shift = mod_ref[pl.ds(row, 1), 3 * D_MODEL:4 * D_MODEL]
    scale = mod_ref[pl.ds(row, 1), 4 * D_MODEL:5 * D_MODEL]
    x = h_ref[0]
    xn = x * lax.rsqrt(jnp.mean(x * x, axis=-1, keepdims=True) + EPS) * g_ref[...]
    xm = xn * (1.0 + scale) + shift
    xb = xm.astype(bf16)
    bits = lax.bitcast_convert_type(xb.astype(f32), jnp.int32)
    for r in range(EXPERT_ROWS):
        lo = (bits[:, (2 * r) * 128:(2 * r + 1) * 128] >> 16) & 0xFFFF
        hi = bits[:, (2 * r + 1) * 128:(2 * r + 2) * 128] & jnp.int32(-65536)
        xw_ref[r] = lo | hi
    for hp in range(2 * PEER_HEADS):
        lo = hp * PEER_HALF
        q_scr[hp] = jnp.dot(xb, wq_ref[:, lo:lo + PEER_HALF],
                            preferred_element_type=f32).astype(bf16)

    half = PEER_TOPK // 2

    def candidates(s1, s2, i1, i2):
        cand, cexp = [], []
        for i in range(half):
            size = PEER_TOPK if i == 0 else half
            blk = s1[i:i + 1, :] + s2[0:size, :]
            keep = PEER_TOPK // (i + 1)
            if keep < size:
                jrow = lax.broadcasted_iota(jnp.int32, (size, 1), 0)
                blk = jnp.where(jrow < keep, blk, -jnp.inf)
            cand.append(blk)
            cexp.append(i1[i:i + 1, :] * PEER_N_KEYS + i2[0:size, :])
        cand.append(s1[half:PEER_TOPK, :] + s2[0:1, :])
        cexp.append(i1[half:PEER_TOPK, :] * PEER_N_KEYS + i2[0:1, :])
        return jnp.concatenate(cand, axis=0), jnp.concatenate(cexp, axis=0)

    def head_pair_body(hp, carry):
        for d in range(2):
            _top16([(lax.dot_general(keys_ref[p], q_scr[4 * hp + 2 * d + p], _NT,
                                     preferred_element_type=f32),
                     None, (2 * d + p) * PEER_TOPK) for p in range(2)], sv, si)
        jobs = []
        for d in range(2):
            lo = 2 * d * PEER_TOPK
            s1, s2 = sv[lo:lo + PEER_TOPK, :], sv[lo + PEER_TOPK:lo + 2 * PEER_TOPK, :]
            i1, i2 = si[lo:lo + PEER_TOPK, :], si[lo + PEER_TOPK:lo + 2 * PEER_TOPK, :]
            cand, cexp = candidates(s1, s2, i1, i2)
            jobs.append((cand, cexp, pl.multiple_of((2 * hp + d) * PEER_TOPK, PEER_TOPK)))
        _top16(jobs, bv, be)
        return carry

    lax.fori_loop(0, PEER_HEADS // 2, head_pair_body, 0)

    for hd in range(PEER_HEADS):
        best = bv[hd * PEER_TOPK:(hd + 1) * PEER_TOPK, :]
        e = jnp.exp(best - jnp.max(best, axis=0, keepdims=True))
        bv[hd * PEER_TOPK:(hd + 1) * PEER_TOPK, :] = e / jnp.sum(e, axis=0, keepdims=True)
    for t in range(ROW_BLOCK // PEER_TOK):
        idx_ref[t] = (be[:, t * PEER_TOK:(t + 1) * PEER_TOK].T * EXPERT_ROWS).astype(jnp.int32)
        gate_ref[t] = bv[:, t * PEER_TOK:(t + 1) * PEER_TOK]


def peer_query(h, mod, g, wq_bf16, keys_bf16, n_ctx):
    bsz, n_tok, _ = h.shape
    nblk = n_tok // ROW_BLOCK
    sub = ROW_BLOCK // PEER_TOK
    n_pblk = bsz * n_tok // PEER_TOK
    row_spec = pl.BlockSpec((1, ROW_BLOCK, D_MODEL), lambda b, j: (b, j, 0))
    full = lambda a: pl.BlockSpec(a.shape, lambda b, j: (0,) * a.ndim)
    pick_spec = pl.BlockSpec((sub, PEER_PICKS, PEER_TOK), lambda b, j: (b * nblk + j, 0, 0))
    f32, i32 = jnp.float32, jnp.int32
    return pl.pallas_call(
        functools.partial(_peer_query_kernel, n_ctx // ROW_BLOCK),
        grid=(bsz, nblk),
        in_specs=[row_spec, full(mod), full(g), full(wq_bf16), full(keys_bf16)],
        out_specs=[pl.BlockSpec((EXPERT_ROWS, ROW_BLOCK, 128), lambda b, j: (0, b * nblk + j, 0)),
                   pick_spec, pick_spec],
        out_shape=[jax.ShapeDtypeStruct((EXPERT_ROWS, bsz * n_tok, 128), i32),
                   jax.ShapeDtypeStruct((n_pblk, PEER_PICKS, PEER_TOK), i32),
                   jax.ShapeDtypeStruct((n_pblk, PEER_PICKS, PEER_TOK), f32)],
        scratch_shapes=[pltpu.VMEM((2 * PEER_HEADS, ROW_BLOCK, PEER_HALF), jnp.bfloat16),
                        pltpu.VMEM((4 * PEER_TOPK, ROW_BLOCK), f32),
                        pltpu.VMEM((4 * PEER_TOPK, ROW_BLOCK), f32),
                        pltpu.VMEM((PEER_PICKS, ROW_BLOCK), f32),
                        pltpu.VMEM((PEER_PICKS, ROW_BLOCK), f32)],
        compiler_params=pltpu.CompilerParams(vmem_limit_bytes=VMEM_LIMIT),
        name="peer_query",
    )(h, mod, g, wq_bf16, keys_bf16)


def pack_rows(rows):
    bits = lax.bitcast_convert_type(rows.astype(jnp.bfloat16), jnp.uint16).astype(jnp.uint32)
    bits = bits.reshape(-1, EXPERT_ROWS, 2, 128)
    words = bits[:, :, 0, :] | (bits[:, :, 1, :] << 16)
    return lax.bitcast_convert_type(words, jnp.int32)


def _group_chunks(idx_ref, tab_ref, r0):
    rows = [idx_ref.at[0, 0, pl.ds((r0 + a) * PEER_PICKS, PEER_PICKS)] for a in range(PEER_STAGE)]
    for c in range(PEER_PICKS // PEER_CHUNK):
        tiles = [[None] * PEER_CHUNK for _ in range(PEER_STAGE)]
        for j in range(PEER_CHUNK):
            for a in range(PEER_STAGE):
                off = pl.multiple_of(rows[a][c * PEER_CHUNK + j], EXPERT_ROWS)
                tiles[a][j] = tab_ref[pl.ds(off, EXPERT_ROWS), :]
        yield c, [jnp.concatenate(tiles[a], axis=0) for a in range(PEER_STAGE)]


def _pick_rows_mask():
    tile_rows = 2 * EXPERT_ROWS
    shape = (PEER_PICKS, PEER_PICKS * tile_rows)
    return (lax.broadcasted_iota(jnp.int32, shape, 1) // tile_rows
            == lax.broadcasted_iota(jnp.int32, shape, 0))


def _peer_score_kernel(idx_ref, x_ref, gate_ref, tab_ref, w_ref):
    f32, bf16, i32 = jnp.float32, jnp.bfloat16, jnp.int32
    sel = _pick_rows_mask().astype(bf16)
    lane = lax.broadcasted_iota(i32, (PEER_PICKS, PEER_TOK), 1)
    cols = PEER_CHUNK * 2 * EXPERT_ROWS

    def group_body(g, acc):
        r0 = pl.multiple_of(g * PEER_STAGE, PEER_STAGE)
        xg = x_ref[:, pl.ds(r0, PEER_STAGE), :]
        xts = []
        for a in range(PEER_STAGE):
            xw = jnp.concatenate([xg[r, a:a + 1, :] for r in range(EXPERT_ROWS)], axis=0)
            xts.append(pltpu.bitcast(jnp.concatenate([xw] * PEER_CHUNK, axis=0), bf16))
        parts = [[] for _ in range(PEER_STAGE)]
        for c, words in _group_chunks(idx_ref, tab_ref, r0):
            for a in range(PEER_STAGE):
                prod = pltpu.bitcast(words[a], bf16) * xts[a]
                parts[a].append(jnp.dot(sel[0:PEER_CHUNK, 0:cols], prod,
                                        preferred_element_type=f32))
        for a in range(PEER_STAGE):
            part = jnp.concatenate(parts[a], axis=0)
            acc = jnp.where(lane == r0 + a, jnp.sum(part, axis=1, keepdims=True), acc)
        return acc

    scores = lax.fori_loop(0, PEER_TOK // PEER_STAGE, group_body,
                           jnp.zeros((PEER_PICKS, PEER_TOK), f32))
    w = gate_ref[0] * jax.nn.gelu(scores)
    w_ref[...] = jnp.dot(w.T.astype(bf16), sel, preferred_element_type=f32)


def _peer_mix_kernel(idx_ref, wexp_ref, h_ref, gate_ref, tab_ref, o_ref):
    f32, bf16, i32 = jnp.float32, jnp.bfloat16, jnp.int32
    tile_rows = 2 * EXPERT_ROWS
    shape = (tile_rows, PEER_PICKS * tile_rows)
    diag = (lax.broadcasted_iota(i32, shape, 1) % tile_rows
            == lax.broadcasted_iota(i32, shape, 0))
    cols = PEER_CHUNK * tile_rows

    def group_body(g, carry):
        r0 = pl.multiple_of(g * PEER_STAGE, PEER_STAGE)
        tiles = []
        wmats = [jnp.where(diag, wexp_ref[pl.ds(r0 + a, 1), :], 0.0).astype(bf16)
                 for a in range(PEER_STAGE)]
        tiles = [jnp.zeros((tile_rows, 128), f32)] * PEER_STAGE
        for c, words in _group_chunks(idx_ref, tab_ref, r0):
            tiles = [tiles[a] + jnp.dot(wmats[a][:, c * cols:(c + 1) * cols],
                                        pltpu.bitcast(words[a], bf16),
                                        preferred_element_type=f32) for a in range(PEER_STAGE)]
        rows = jnp.concatenate(
            [jnp.concatenate([tile[r:r + 1, :] for tile in tiles], axis=0)
             for r in range(tile_rows)], axis=1)
        o_ref[pl.ds(r0, PEER_STAGE), :] = (h_ref[pl.ds(r0, PEER_STAGE), :]
                                           + gate_ref[0, 0] * rows)
        return carry

    lax.fori_loop(0, PEER_TOK // PEER_STAGE, group_body, 0)


def _table_spec(tab):
    return pl.BlockSpec(tab.shape, lambda i: (0,) * tab.ndim, pipeline_mode=pl.Buffered(1))


def peer_experts(h, xw, idx, gates, mod, u_packed, v_packed, n_ctx):
    bsz, n_tok, _ = h.shape
    n_pblk = idx.shape[0]
    blk_per_batch = n_tok // PEER_TOK
    ctx_pblk = n_ctx // PEER_TOK
    flat = PEER_PICKS * PEER_TOK
    smem_spec = pl.BlockSpec((1, 1, flat), lambda i: (i, 0, 0), memory_space=pltpu.SMEM)
    pick_spec = pl.BlockSpec((1, PEER_PICKS, PEER_TOK), lambda i: (i, 0, 0))
    word_spec = pl.BlockSpec((EXPERT_ROWS, PEER_TOK, 128), lambda i: (0, i, 0))
    row_spec = pl.BlockSpec((PEER_TOK, D_MODEL), lambda i: (i, 0))
    idx_flat = idx.reshape(n_pblk, 1, flat)
    wexp = pl.pallas_call(
        _peer_score_kernel,
        grid=(n_pblk,),
        in_specs=[smem_spec, word_spec, pick_spec, _table_spec(u_packed)],
        out_specs=row_spec,
        out_shape=jax.ShapeDtypeStruct((bsz * n_tok, D_MODEL), jnp.float32),
        compiler_params=pltpu.CompilerParams(vmem_limit_bytes=VMEM_LIMIT),
        name="peer_scores",
    )(idx_flat, xw, gates, u_packed)

    def gate_map(i):
        bi, ji = i // blk_per_batch, i % blk_per_batch
        return (jnp.where(ji < ctx_pblk, 0, bi + 1), N_MOD - 1, 0, 0)

    out = pl.pallas_call(
        _peer_mix_kernel,
        grid=(n_pblk,),
        in_specs=[smem_spec, row_spec, row_spec,
                  pl.BlockSpec((1, 1, 1, D_MODEL), gate_map), _table_spec(v_packed)],
        out_specs=row_spec,
        out_shape=jax.ShapeDtypeStruct((bsz * n_tok, D_MODEL), jnp.float32),
        compiler_params=pltpu.CompilerParams(vmem_limit_bytes=VMEM_LIMIT),
        name="peer_mix",
    )(idx_flat, wexp, h.reshape(-1, D_MODEL), mod.reshape(MOD_ROWS, N_MOD, 1, D_MODEL), v_packed)
    return out.reshape(h.shape)


def kernel(x, c, ctx, c_ctx, w_mod, b_mod, norm_mix_g, w_in, q_norm_g, k_norm_g, ssm_a_re, ssm_a_im, ssm_log_dt, ssm_b_re, ssm_b_im, ssm_c_re, ssm_c_im, ssm_d, w_glu, b_glu, w_branch_attn, w_branch_ssm, w_out, norm_ffn_g, peer_w_q, peer_keys, peer_u, peer_v):
    bf16 = jnp.bfloat16
    bsz, n_lat, _ = x.shape
    n_ctx = ctx.shape[1]
    depth = w_mod.shape[0]
    assert bsz + 1 <= MOD_ROWS and n_ctx % ROW_BLOCK == 0 and n_lat % ROW_BLOCK == 0

    cc = jnp.zeros((MOD_ROWS, D_MODEL), jnp.float32).at[0].set(c_ctx).at[1:bsz + 1].set(c)
    mods = modulation(cc, w_mod, b_mod)
    rope = rope_tables(n_ctx, n_lat)
    h = jnp.concatenate([ctx, x], axis=1)
    row = lambda a: a.reshape(1, -1)

    for i in range(depth):
        mod = mods[i]
        q, k, v, u, ga, gb = in_projection(h, mod, row(norm_mix_g[i]), w_in[i].astype(bf16), rope,
                                           row(q_norm_g[i]), row(k_norm_g[i]), n_ctx)
        attn = attention(q, k, v, n_ctx)
        y = s5_scan(u.astype(bf16), ssm_a_re[i], ssm_a_im[i], ssm_log_dt[i],
                    ssm_b_re[i], ssm_b_im[i], ssm_c_re[i], ssm_c_im[i], n_ctx)
        h = branch_merge(h, mod, attn, y, u, ga, gb, row(ssm_d[i]), w_glu[i].astype(bf16),
                         row(b_glu[i]), w_branch_attn[i].astype(bf16),
                         w_branch_ssm[i].astype(bf16), w_out[i].astype(bf16), n_ctx)
        n_peer_ctx = n_ctx
        if i == depth - 1:
            h, n_peer_ctx = h[:, n_ctx:, :], 0
        xw, idx, gates = peer_query(h, mod, row(norm_ffn_g[i]), peer_w_q[i].astype(bf16),
                                    peer_keys[i].astype(bf16), n_peer_ctx)
        h = peer_experts(h, xw, idx, gates, mod, pack_rows(peer_u[i]).reshape(-1, 128),
                         pack_rows(peer_v[i]).reshape(-1, 128), n_peer_ctx)
    return h
```
